```python
import math
import jax, jax.numpy as jnp
from jax import lax
import numpy as np

D_MODEL = 1024
BATCH = 2
SEQ = 8192
DEPTH = 1

D_FF = 2816
NORM_EPS = 1e-6
SSD_HEAD_DIM = 64
SSD_HEADS = D_MODEL // SSD_HEAD_DIM
SSD_WIDTH = SSD_HEADS * SSD_HEAD_DIM
SSD_GROUPS = 2
SSD_STATE = 128
SSD_CONV = 4
SSD_CHUNK = 128
ATT_QK_DIM = 64
ATT_V_DIM = 2 * ATT_QK_DIM
ATT_HEADS = D_MODEL // ATT_V_DIM
ATT_WIDTH = ATT_HEADS * ATT_V_DIM
ATT_BLOCK = 128
MIX_WIDTH = SSD_WIDTH + ATT_WIDTH
CONV_CH = SSD_WIDTH + 2 * SSD_GROUPS * SSD_STATE
QK_WIDTH = ATT_HEADS * 2 * ATT_QK_DIM
IN_SIZES = (SSD_WIDTH, CONV_CH, SSD_HEADS, QK_WIDTH, QK_WIDTH, ATT_WIDTH)
IN_WIDTH = SSD_WIDTH + CONV_CH + SSD_HEADS + 2 * QK_WIDTH + ATT_WIDTH

kernel_name = "hymba_ssd_diffattn_macaron"


def rms_norm(x, w):
    xf = x.astype(jnp.float32)
    y = xf * lax.rsqrt(jnp.mean(xf * xf, axis=-1, keepdims=True) + NORM_EPS)
    return (y * w.astype(jnp.float32)).astype(x.dtype)


def swiglu(h, w_gate, w_up, w_down):
    return (jax.nn.silu(h @ w_gate) * (h @ w_up)) @ w_down


def causal_dwconv(u, w, b):
    k = w.shape[0]
    out = lax.conv_general_dilated(
        u, w[:, None, :].astype(u.dtype), window_strides=(1,), padding=[(k - 1, 0)],
        dimension_numbers=("NWC", "WIO", "NWC"), feature_group_count=u.shape[-1])
    return out + b.astype(u.dtype)


def ssd_chunked(xs, dt, a, bm, cm, d_skip):
    bsz, s, nh, hp = xs.shape
    g, n = bm.shape[2], bm.shape[3]
    hg = nh // g
    L = SSD_CHUNK
    nc = s // L
    xs = xs.reshape(bsz, nc, L, g, hg, hp)
    dt = dt.reshape(bsz, nc, L, g, hg)
    bm = bm.reshape(bsz, nc, L, g, n)
    cm = cm.reshape(bsz, nc, L, g, n)
    a_cs = jnp.cumsum(dt * a.reshape(g, hg), axis=2)
    xdt = xs * dt[..., None]
    seg = a_cs[:, :, :, None] - a_cs[:, :, None, :]
    causal = jnp.tril(jnp.ones((L, L), dtype=bool))[None, None, :, :, None, None]
    decay = jnp.where(causal, jnp.exp(jnp.where(causal, seg, 0.0)), 0.0)
    cb = jnp.einsum("bcign,bcjgn->bcijg", cm, bm)
    y_diag = jnp.einsum("bcijgh,bcjghp->bcighp", cb[..., None] * decay, xdt)
    decay_states = jnp.exp(a_cs[:, :, -1:] - a_cs)
    states = jnp.einsum("bclgn,bclgh,bclghp->bcghpn", bm, decay_states, xdt)
    chunk_decay = jnp.exp(a_cs[:, :, -1])

    def step(h, inp):
        st, dec = inp
        return dec[..., None, None] * h + st, h

    h0 = jnp.zeros((bsz, g, hg, hp, n), dtype=states.dtype)
    _, prev = lax.scan(step, h0, (jnp.moveaxis(states, 1, 0), jnp.moveaxis(chunk_decay, 1, 0)))
    prev = jnp.moveaxis(prev, 0, 1)
    y_off = jnp.einsum("bclgn,bcghpn,bclgh->bclghp", cm, prev, jnp.exp(a_cs))
    y = y_diag + y_off + xs * d_skip.astype(jnp.float32).reshape(g, hg)[:, :, None]
    return y.reshape(bsz, s, nh * hp)


def diff_attention(q, k, v, lam):
    bsz, s, nh = q.shape[0], q.shape[1], q.shape[2]
    nb = s // ATT_BLOCK
    scale = ATT_QK_DIM ** -0.5
    kf = k.astype(jnp.float32)
    vf = v.astype(jnp.float32)
    qb = jnp.moveaxis(q.astype(jnp.float32).reshape(bsz, nb, ATT_BLOCK, nh, 2, ATT_QK_DIM), 1, 0)
    kpos = jnp.arange(s)

    def block(args):
        qi, i = args
        sc = jnp.einsum("bqhtd,bkhtd->bhtqk", qi, kf) * scale
        qpos = i * ATT_BLOCK + jnp.arange(ATT_BLOCK)
        mask = qpos[:, None] >= kpos[None, :]
        p = jax.nn.softmax(jnp.where(mask, sc, -jnp.inf), axis=-1)
        att = p[:, :, 0] - lam * p[:, :, 1]
        return jnp.einsum("bhqk,bkhd->bqhd", att, vf)

    out = lax.map(block, (qb, jnp.arange(nb)))
    return jnp.moveaxis(out, 0, 1).reshape(bsz, s, nh, ATT_V_DIM)


def setup_inputs(seed: int = 0) -> dict:
    key = jax.random.key(seed)
    ks = jax.random.split(key, 24)
    f32 = jnp.float32

    def nrm(k, shape, scale):
        return jax.random.normal(k, shape, f32) * scale

    def gain(k, shape):
        return 1.0 + 0.02 * jax.random.normal(k, shape, f32)

    dt0 = jnp.exp(jax.random.uniform(ks[9], (DEPTH, SSD_HEADS), f32, math.log(1e-3), math.log(1e-1)))
    dt_bias = dt0 + jnp.log(-jnp.expm1(-dt0))
    return {
        "x": jax.random.normal(ks[0], (BATCH, SEQ, D_MODEL), f32),
        "ffn1_norm_w": gain(ks[1], (DEPTH, D_MODEL)),
        "ffn1_w_gate": nrm(ks[2], (DEPTH, D_MODEL, D_FF), D_MODEL ** -0.5),
        "ffn1_w_up": nrm(ks[3], (DEPTH, D_MODEL, D_FF), D_MODEL ** -0.5),
        "ffn1_w_down": nrm(ks[4], (DEPTH, D_FF, D_MODEL), D_FF ** -0.5),
        "mix_norm_w": gain(ks[5], (DEPTH, D_MODEL)),
        "w_in": nrm(ks[6], (DEPTH, D_MODEL, IN_WIDTH), D_MODEL ** -0.5),
        "conv_w": nrm(ks[7], (DEPTH, SSD_CONV, CONV_CH), SSD_CONV ** -0.5),
        "conv_b": nrm(ks[8], (DEPTH, CONV_CH), 0.02),
        "dt_bias": dt_bias,
        "a_log": jnp.log(jax.random.uniform(ks[10], (DEPTH, SSD_HEADS), f32, 1.0, 16.0)),
        "d_skip": gain(ks[11], (DEPTH, SSD_HEADS)),
        "ssd_norm_w": gain(ks[12], (DEPTH, SSD_WIDTH)),
        "q_norm_w": gain(ks[13], (DEPTH, ATT_QK_DIM)),
        "k_norm_w": gain(ks[14], (DEPTH, ATT_QK_DIM)),
        "lambda_q1": nrm(ks[15], (DEPTH, ATT_QK_DIM), 0.1),
        "lambda_k1": nrm(ks[16], (DEPTH, ATT_QK_DIM), 0.1),
        "lambda_q2": nrm(ks[17], (DEPTH, ATT_QK_DIM), 0.1),
        "lambda_k2": nrm(ks[18], (DEPTH, ATT_QK_DIM), 0.1),
        "attn_subln_w": gain(ks[19], (DEPTH, ATT_V_DIM)),
        "w_out": nrm(ks[20], (DEPTH, MIX_WIDTH, D_MODEL), MIX_WIDTH ** -0.5),
        "ffn2_norm_w": gain(ks[21], (DEPTH, D_MODEL)),
        "ffn2_w_gate": nrm(ks[22], (DEPTH, D_MODEL, D_FF), D_MODEL ** -0.5),
        "ffn2_w_up": nrm(ks[23], (DEPTH, D_MODEL, D_FF), D_MODEL ** -0.5),
        "ffn2_w_down": nrm(jax.random.fold_in(key, 99), (DEPTH, D_FF, D_MODEL), D_FF ** -0.5),
    }


def reference(x, ffn1_norm_w, ffn1_w_gate, ffn1_w_up, ffn1_w_down, mix_norm_w, w_in, conv_w,
              conv_b, dt_bias, a_log, d_skip, ssd_norm_w, q_norm_w, k_norm_w, lambda_q1,
              lambda_k1, lambda_q2, lambda_k2, attn_subln_w, w_out, ffn2_norm_w, ffn2_w_gate,
              ffn2_w_up, ffn2_w_down):
    bsz, s, _ = x.shape
    f32 = jnp.float32
    split_idx = [int(v) for v in np.cumsum(IN_SIZES)[:-1]]
    for l in range(DEPTH):
        lambda_init = 0.8 - 0.6 * math.exp(-0.3 * l)
        x = x + 0.5 * swiglu(rms_norm(x, ffn1_norm_w[l]), ffn1_w_gate[l], ffn1_w_up[l], ffn1_w_down[l])
        h = rms_norm(x, mix_norm_w[l])
        proj = h @ w_in[l]
        z, xbc, dt_raw, q, k, v = jnp.split(proj, split_idx, axis=-1)
        xbc = jax.nn.silu(causal_dwconv(xbc, conv_w[l], conv_b[l]))
        xs, bm, cm = jnp.split(xbc, [SSD_WIDTH, SSD_WIDTH + SSD_GROUPS * SSD_STATE], axis=-1)
        dt = jax.nn.softplus(dt_raw.astype(f32) + dt_bias[l].astype(f32))
        a = -jnp.exp(a_log[l].astype(f32))
        y_ssd = ssd_chunked(
            xs.astype(f32).reshape(bsz, s, SSD_HEADS, SSD_HEAD_DIM), dt, a,
            bm.astype(f32).reshape(bsz, s, SSD_GROUPS, SSD_STATE),
            cm.astype(f32).reshape(bsz, s, SSD_GROUPS, SSD_STATE), d_skip[l])
        gated = (y_ssd * jax.nn.silu(z.astype(f32))).reshape(bsz, s, SSD_GROUPS, SSD_WIDTH // SSD_GROUPS)
        y_ssd = rms_norm(gated, ssd_norm_w[l].reshape(SSD_GROUPS, -1)).reshape(bsz, s, SSD_WIDTH)
        q = rms_norm(q.reshape(bsz, s, ATT_HEADS, 2, ATT_QK_DIM), q_norm_w[l])
        k = rms_norm(k.reshape(bsz, s, ATT_HEADS, 2, ATT_QK_DIM), k_norm_w[l])
        v = v.reshape(bsz, s, ATT_HEADS, ATT_V_DIM)
        lam = (jnp.exp(jnp.sum(lambda_q1[l].astype(f32) * lambda_k1[l].astype(f32)))
               - jnp.exp(jnp.sum(lambda_q2[l].astype(f32) * lambda_k2[l].astype(f32)))
               + lambda_init)
        o = diff_attention(q, k, v, lam)
        o = rms_norm(o, attn_subln_w[l]) * (1.0 - lambda_init)
        mixed = jnp.concatenate([y_ssd.astype(x.dtype), o.reshape(bsz, s, ATT_WIDTH).astype(x.dtype)], axis=-1)
        x = x + (mixed @ w_out[l]).astype(x.dtype)
        x = x + 0.5 * swiglu(rms_norm(x, ffn2_norm_w[l]), ffn2_w_gate[l], ffn2_w_up[l], ffn2_w_down[l])
    return x
```

```python
import functools
import math

import jax
import jax.numpy as jnp
from jax import lax
from jax.experimental import pallas as pl
from jax.experimental.pallas import tpu as pltpu

F32 = jnp.float32
BF16 = jnp.bfloat16

NORM_EPS = 1e-6
SSD_HEAD_DIM = 64
SSD_GROUPS = 2
SSD_STATE = 128
SSD_CONV = 4
SSD_CHUNK = 128
ATT_QK_DIM = 64
ATT_V_DIM = 128
LANES = 128
CONV_HALO = 8

FFN_ROWS = 512
FFN_COLS = 256
ATT_BLOCK_Q = 512
ATT_BLOCK_K = 512
VMEM_LIMIT = 56 * 1024 * 1024


def _resident(shape):
    nd = len(shape)
    return pl.BlockSpec(shape, lambda *_: (0,) * nd, pipeline_mode=pl.Buffered(1))


def _rms_norm(x, w):
    return x * lax.rsqrt(jnp.mean(x * x, axis=-1, keepdims=True) + NORM_EPS) * w


def _silu(x):
    return x * jax.nn.sigmoid(x)


def _swiglu(h, wg_ref, wu_ref, wd_ref):
    d_ff = wg_ref.shape[1]
    acc = None
    for c in range(d_ff // FFN_COLS):
        sl = slice(c * FFN_COLS, (c + 1) * FFN_COLS)
        g = jnp.dot(h, wg_ref[:, sl], preferred_element_type=F32)
        u = jnp.dot(h, wu_ref[:, sl], preferred_element_type=F32)
        a = (_silu(g) * u).astype(BF16)
        d = jnp.dot(a, wd_ref[sl, :], preferred_element_type=F32)
        acc = d if acc is None else acc + d
    return acc


def _ffn_kernel(x_ref, nw_ref, wg_ref, wu_ref, wd_ref, o_ref):
    x = x_ref[...]
    h = _rms_norm(x, nw_ref[...]).astype(BF16)
    o_ref[...] = x + 0.5 * _swiglu(h, wg_ref, wu_ref, wd_ref)


def _ffn(x2d, norm_w, wg, wu, wd):
    t, d = x2d.shape
    tm = min(FFN_ROWS, t)
    row = pl.BlockSpec((tm, d), lambda i: (i, 0))
    return pl.pallas_call(
        _ffn_kernel,
        grid=(t // tm,),
        in_specs=[row, _resident((1, d)), _resident(wg.shape), _resident(wu.shape), _resident(wd.shape)],
        out_specs=row,
        out_shape=jax.ShapeDtypeStruct((t, d), F32),
        compiler_params=pltpu.CompilerParams(dimension_semantics=("arbitrary",), vmem_limit_bytes=VMEM_LIMIT),
        name="ffn",
    )(x2d, norm_w, wg, wu, wd)


def _qk_norm(t, w2, scale):
    lo = lax.broadcasted_iota(jnp.int32, t.shape, 1) < ATT_QK_DIM
    x2 = t * t
    s_lo = jnp.sum(jnp.where(lo, x2, 0.0), axis=-1, keepdims=True)
    s_hi = jnp.sum(jnp.where(lo, 0.0, x2), axis=-1, keepdims=True)
    r = lax.rsqrt(jnp.where(lo, s_lo, s_hi) * (1.0 / ATT_QK_DIM) + NORM_EPS)
    return t * r * (w2 * scale)


def _proj_kernel(x_ref, nw_ref, wz_ref, wxbc_ref, wdt_ref, wq_ref, wk_ref, wv_ref, qw_ref, kw_ref,
                 z_ref, xbc_ref, dt_ref, q_ref, k_ref, v_ref):
    h = _rms_norm(x_ref[...], nw_ref[...]).astype(BF16)
    z_ref[...] = jnp.dot(h, wz_ref[...], preferred_element_type=F32).astype(BF16)
    xbc_ref[...] = jnp.dot(h, wxbc_ref[...], preferred_element_type=F32).astype(BF16)
    dt_ref[...] = jnp.dot(h, wdt_ref[...], preferred_element_type=F32)
    v_ref[...] = jnp.dot(h, wv_ref[...], preferred_element_type=F32).astype(BF16)
    q = jnp.dot(h, wq_ref[...], preferred_element_type=F32)
    k = jnp.dot(h, wk_ref[...], preferred_element_type=F32)
    n_heads = q.shape[1] // LANES
    for hd in range(n_heads):
        sl = slice(hd * LANES, (hd + 1) * LANES)
        q_ref[:, sl] = _qk_norm(q[:, sl], qw_ref[...], ATT_QK_DIM ** -0.5).astype(BF16)
        k_ref[:, sl] = _qk_norm(k[:, sl], kw_ref[...], 1.0).astype(BF16)


def _proj(x2d, norm_w, wz, wxbc, wdt, wq, wk, wv, qw2, kw2):
    t, d = x2d.shape
    tm = min(FFN_ROWS, t)
    row = lambda n: pl.BlockSpec((tm, n), lambda i: (i, 0))
    widths = (wz.shape[1], wxbc.shape[1], wdt.shape[1], wq.shape[1], wk.shape[1], wv.shape[1])
    dtypes = (BF16, BF16, F32, BF16, BF16, BF16)
    return pl.pallas_call(
        _proj_kernel,
        grid=(t // tm,),
        in_specs=[row(d), _resident((1, d))] + [_resident(w.shape) for w in (wz, wxbc, wdt, wq, wk, wv)]
                 + [_resident((1, LANES)), _resident((1, LANES))],
        out_specs=[row(n) for n in widths],
        out_shape=[jax.ShapeDtypeStruct((t, n), dt) for n, dt in zip(widths, dtypes)],
        compiler_params=pltpu.CompilerParams(dimension_semantics=("arbitrary",), vmem_limit_bytes=VMEM_LIMIT),
        name="proj",
    )(x2d, norm_w, wz, wxbc, wdt, wq, wk, wv, qw2, kw2)


def _lane_bcast(x, col):
    return jnp.broadcast_to(x[:, col:col + 1], x.shape)


def _expand_heads(x, pair):
    lo = lax.broadcasted_iota(jnp.int32, x.shape, 1) < SSD_HEAD_DIM
    return jnp.where(lo, _lane_bcast(x, 2 * pair), _lane_bcast(x, 2 * pair + 1))


def _ssd_kernel(z_ref, xbc_ref, dt_ref, cw_ref, cb_ref, dtb_ref, alog_ref, dskip_ref, nw_ref,
                y_ref, ext_ref, state_ref):
    L = SSD_CHUNK
    width = z_ref.shape[-1]
    n_pairs = width // LANES
    heads_per_group = (width // SSD_HEAD_DIM) // SSD_GROUPS
    gw = width // SSD_GROUPS

    @pl.when(pl.program_id(1) == 0)
    def _():
        ext_ref[0:CONV_HALO, :] = jnp.zeros((CONV_HALO, ext_ref.shape[1]), F32)
        state_ref[...] = jnp.zeros(state_ref.shape, F32)

    ext_ref[CONV_HALO:CONV_HALO + L, :] = xbc_ref[...].astype(F32)
    conv = cb_ref[...]
    for kk in range(SSD_CONV):
        off = CONV_HALO - (SSD_CONV - 1) + kk
        conv = conv + cw_ref[kk:kk + 1, :] * ext_ref[off:off + L, :]
    ext_ref[0:CONV_HALO, :] = ext_ref[L:L + CONV_HALO, :]
    act = _silu(conv)
    xs = act[:, :width]
    bm = act[:, width:width + SSD_GROUPS * SSD_STATE].astype(BF16)
    cm = act[:, width + SSD_GROUPS * SSD_STATE:]

    raw = dt_ref[...] + dtb_ref[...]
    dt = jnp.maximum(raw, 0.0) + jnp.log1p(jnp.exp(-jnp.abs(raw)))
    dta = dt * (-jnp.exp(alog_ref[...]))
    ri = lax.broadcasted_iota(jnp.int32, (L, L), 0)
    ci = lax.broadcasted_iota(jnp.int32, (L, L), 1)
    causal = ri >= ci
    a_cs = jnp.dot(causal.astype(F32), dta, preferred_element_type=F32, precision=lax.Precision.HIGHEST)
    a_last = a_cs[L - 1:L, :]
    ea = jnp.exp(a_cs)
    f = dt * jnp.exp(a_last - a_cs)
    a_cs_t = a_cs.T
    dt_t = dt.T

    lo = lax.broadcasted_iota(jnp.int32, (L, LANES), 1) < SSD_HEAD_DIM
    xs_bf = xs.astype(BF16)
    y_tiles = []
    for pair in range(n_pairs):
        g = (2 * pair) // heads_per_group
        cm_g = cm[:, g * SSD_STATE:(g + 1) * SSD_STATE]
        bm_g = bm[:, g * SSD_STATE:(g + 1) * SSD_STATE]
        cb_g = lax.dot_general(cm_g.astype(BF16), bm_g, (((1,), (1,)), ((), ())), preferred_element_type=F32)
        sl = slice(pair * LANES, (pair + 1) * LANES)
        x_pair = xs_bf[:, sl]
        prev_pair = state_ref[:, sl].astype(BF16)
        y_pair = None
        for half in range(2):
            hd = 2 * pair + half
            seg = _lane_bcast(a_cs, hd) - a_cs_t[hd:hd + 1, :]
            decay = jnp.where(causal, jnp.exp(jnp.where(causal, seg, 0.0)), 0.0)
            m_h = (cb_g * decay * dt_t[hd:hd + 1, :]).astype(BF16)
            c_h = (cm_g * _lane_bcast(ea, hd)).astype(BF16)
            keep = lo if half == 0 else jnp.logical_not(lo)
            zero = jnp.zeros((L, LANES), BF16)
            rhs = jnp.concatenate([jnp.where(keep, x_pair, zero), jnp.where(keep, prev_pair, zero)], axis=0)
            yh = jnp.dot(jnp.concatenate([m_h, c_h], axis=1), rhs, preferred_element_type=F32)
            y_pair = yh if y_pair is None else y_pair + yh
        y_tiles.append(y_pair)

    for pair in range(n_pairs):
        g = (2 * pair) // heads_per_group
        sl = slice(pair * LANES, (pair + 1) * LANES)
        bm_g = bm[:, g * SSD_STATE:(g + 1) * SSD_STATE]
        xw = (xs[:, sl] * _expand_heads(f, pair)).astype(BF16)
        new = lax.dot_general(bm_g, xw, (((0,), (0,)), ((), ())), preferred_element_type=F32)
        chunk_decay = _expand_heads(ea[L - CONV_HALO:L, :], pair)[CONV_HALO - 1:CONV_HALO, :]
        state_ref[:, sl] = state_ref[:, sl] * chunk_decay + new

    y = jnp.concatenate(y_tiles, axis=1) + xs * dskip_ref[...]
    gated = y * _silu(z_ref[...].astype(F32))
    for g in range(SSD_GROUPS):
        sl = slice(g * gw, (g + 1) * gw)
        y_ref[:, sl] = _rms_norm(gated[:, sl], nw_ref[:, sl]).astype(BF16)


def _ssd(z, xbc, dt, conv_w, conv_b, dt_bias, a_log, d_skip_exp, norm_w):
    bsz, s, width = z.shape
    L = SSD_CHUNK
    cw = xbc.shape[-1]
    blk = lambda n: pl.BlockSpec((None, L, n), lambda b, c: (b, c, 0))
    return pl.pallas_call(
        _ssd_kernel,
        grid=(bsz, s // L),
        in_specs=[blk(width), blk(cw), blk(LANES), _resident(conv_w.shape), _resident((1, cw)),
                  _resident((1, LANES)), _resident((1, LANES)), _resident((1, width)), _resident((1, width))],
        out_specs=blk(width),
        out_shape=jax.ShapeDtypeStruct((bsz, s, width), BF16),
        scratch_shapes=[pltpu.VMEM((CONV_HALO + L, cw), F32), pltpu.VMEM((SSD_STATE, width), F32)],
        compiler_params=pltpu.CompilerParams(dimension_semantics=("arbitrary", "arbitrary"),
                                             vmem_limit_bytes=VMEM_LIMIT),
        name="ssd",
    )(z, xbc, dt, conv_w, conv_b, dt_bias, a_log, d_skip_exp, norm_w)


def _attn_kernel(q_ref, k_ref, v_ref, lq1_ref, lk1_ref, lq2_ref, lk2_ref, sw_ref, o_ref, *, lambda_init):
    tq = q_ref.shape[0]
    tk = ATT_BLOCK_K if k_ref.shape[0] % ATT_BLOCK_K == 0 else tq
    qi = pl.program_id(2)
    q = q_ref[...]
    lo = lax.broadcasted_iota(jnp.int32, q.shape, 1) < ATT_QK_DIM
    zero = jnp.zeros_like(q)
    q_maps = (jnp.where(lo, q, zero), jnp.where(lo, zero, q))

    def step(j, carry, masked):
        start = pl.multiple_of(j * tk, tk)
        k = k_ref[pl.ds(start, tk), :]
        v_t = v_ref[pl.ds(start, tk), :].T
        out = []
        for mp in range(2):
            m, l, acc = carry[mp]
            s = lax.dot_general(k, q_maps[mp], (((1,), (1,)), ((), ())), preferred_element_type=F32)
            if masked:
                kpos = start + lax.broadcasted_iota(jnp.int32, s.shape, 0)
                qpos = qi * tq + lax.broadcasted_iota(jnp.int32, s.shape, 1)
                s = jnp.where(qpos >= kpos, s, -jnp.inf)
            m_new = jnp.maximum(m, jnp.max(s, axis=0, keepdims=True))
            alpha = jnp.exp(m - m_new)
            p = jnp.exp(s - m_new)
            l = alpha * l + jnp.sum(p, axis=0, keepdims=True)
            acc = alpha * acc + jnp.dot(v_t, p.astype(BF16), preferred_element_type=F32)
            out.append((m_new, l, acc))
        return tuple(out)

    init_one = (jnp.full((1, tq), -jnp.inf, F32), jnp.zeros((1, tq), F32), jnp.zeros((ATT_V_DIM, tq), F32))
    n_full = (qi * tq) // tk
    carry = lax.fori_loop(0, n_full, lambda j, c: step(j, c, False), (init_one, init_one))
    for d in range(tq // tk):
        carry = step(n_full + d, carry, True)

    lam = (jnp.exp(jnp.sum(lq1_ref[...] * lk1_ref[...], axis=-1, keepdims=True))
           - jnp.exp(jnp.sum(lq2_ref[...] * lk2_ref[...], axis=-1, keepdims=True)) + lambda_init)
    (_, l1, acc1), (_, l2, acc2) = carry
    o_t = acc1 / l1 - lam * (acc2 / l2)
    o = o_t.T
    o_ref[...] = (_rms_norm(o, sw_ref[...]) * (1.0 - lambda_init)).astype(BF16)


def _attn(q, k, v, lq1, lk1, lq2, lk2, subln_w, lambda_init):
    bsz, s, width = q.shape
    n_heads = width // ATT_V_DIM
    tq = min(ATT_BLOCK_Q, s)
    q_spec = pl.BlockSpec((None, tq, LANES), lambda b, h, i: (b, i, h))
    kv_spec = pl.BlockSpec((None, s, LANES), lambda b, h, i: (b, 0, h))
    small = _resident((1, ATT_QK_DIM))
    return pl.pallas_call(
        functools.partial(_attn_kernel, lambda_init=lambda_init),
        grid=(bsz, n_heads, s // tq),
        in_specs=[q_spec, kv_spec, kv_spec, small, small, small, small, _resident((1, ATT_V_DIM))],
        out_specs=q_spec,
        out_shape=jax.ShapeDtypeStruct((bsz, s, width), BF16),
        compiler_params=pltpu.CompilerParams(dimension_semantics=("arbitrary",) * 3, vmem_limit_bytes=VMEM_LIMIT),
        name="attn",
    )(q, k, v, lq1, lk1, lq2, lk2, subln_w)


def _out_kernel(x_ref, ys_ref, ya_ref, wos_ref, woa_ref, nw_ref, wg_ref, wu_ref, wd_ref, o_ref):
    x = (x_ref[...] + jnp.dot(ys_ref[...], wos_ref[...], preferred_element_type=F32)
         + jnp.dot(ya_ref[...], woa_ref[...], preferred_element_type=F32))
    h = _rms_norm(x, nw_ref[...]).astype(BF16)
    o_ref[...] = x + 0.5 * _swiglu(h, wg_ref, wu_ref, wd_ref)


def _out(x2d, y_ssd, y_att, wo_ssd, wo_att, norm_w, wg, wu, wd):
    t, d = x2d.shape
    tm = min(FFN_ROWS, t)
    row = lambda n: pl.BlockSpec((tm, n), lambda i: (i, 0))
    return pl.pallas_call(
        _out_kernel,
        grid=(t // tm,),
        in_specs=[row(d), row(y_ssd.shape[1]), row(y_att.shape[1]), _resident(wo_ssd.shape), _resident(wo_att.shape),
                  _resident((1, d)), _resident(wg.shape), _resident(wu.shape), _resident(wd.shape)],
        out_specs=row(d),
        out_shape=jax.ShapeDtypeStruct((t, d), F32),
        compiler_params=pltpu.CompilerParams(dimension_semantics=("arbitrary",), vmem_limit_bytes=VMEM_LIMIT),
        name="out_ffn",
    )(x2d, y_ssd, y_att, wo_ssd, wo_att, norm_w, wg, wu, wd)


def _pad_lanes(v):
    return jnp.zeros((1, LANES), F32).at[0, :v.shape[0]].set(v.astype(F32))


def kernel(x, ffn1_norm_w, ffn1_w_gate, ffn1_w_up, ffn1_w_down, mix_norm_w, w_in, conv_w, conv_b, dt_bias, a_log,
           d_skip, ssd_norm_w, q_norm_w, k_norm_w, lambda_q1, lambda_k1, lambda_q2, lambda_k2, attn_subln_w, w_out,
           ffn2_norm_w, ffn2_w_gate, ffn2_w_up, ffn2_w_down):
    bsz, s, d = x.shape
    depth = w_in.shape[0]
    ssd_heads = dt_bias.shape[1]
    ssd_width = ssd_heads * SSD_HEAD_DIM
    conv_ch = conv_w.shape[2]
    att_width = w_out.shape[1] - ssd_width
    qk_width = (att_width // ATT_V_DIM) * 2 * ATT_QK_DIM
    sizes = (ssd_width, conv_ch, ssd_heads, qk_width, qk_width, att_width)
    offs = [0]
    for n in sizes:
        offs.append(offs[-1] + n)
    row = lambda v: v.astype(F32).reshape(1, -1)

    x2d = x.reshape(bsz * s, d)
    for l in range(depth):
        lambda_init = 0.8 - 0.6 * math.exp(-0.3 * l)
        w = w_in[l].astype(BF16)
        wz, wxbc, wdt, wq, wk, wv = (w[:, offs[i]:offs[i + 1]] for i in range(6))
        wdt = jnp.pad(wdt, ((0, 0), (0, LANES - ssd_heads)))
        wo = w_out[l].astype(BF16)

        x1 = _ffn(x2d, row(ffn1_norm_w[l]), ffn1_w_gate[l].astype(BF16), ffn1_w_up[l].astype(BF16),
                  ffn1_w_down[l].astype(BF16))
        z, xbc, dt, q, k, v = _proj(x1, row(mix_norm_w[l]), wz, wxbc, wdt, wq, wk, wv,
                                    row(jnp.tile(q_norm_w[l], 2)), row(jnp.tile(k_norm_w[l], 2)))
        y_ssd = _ssd(z.reshape(bsz, s, -1), xbc.reshape(bsz, s, -1), dt.reshape(bsz, s, -1),
                     conv_w[l].astype(F32), row(conv_b[l]), _pad_lanes(dt_bias[l]), _pad_lanes(a_log[l]),
                     row(jnp.repeat(d_skip[l], SSD_HEAD_DIM)), row(ssd_norm_w[l]))
        y_att = _attn(q.reshape(bsz, s, -1), k.reshape(bsz, s, -1), v.reshape(bsz, s, -1),
                      row(lambda_q1[l]), row(lambda_k1[l]), row(lambda_q2[l]), row(lambda_k2[l]),
                      row(attn_subln_w[l]), lambda_init)
        x2d = _out(x1, y_ssd.reshape(bsz * s, -1), y_att.reshape(bsz * s, -1), wo[:ssd_width], wo[ssd_width:],
                   row(ffn2_norm_w[l]), ffn2_w_gate[l].astype(BF16), ffn2_w_up[l].astype(BF16),
                   ffn2_w_down[l].astype(BF16))
    return x2d.reshape(bsz, s, d)
```

```python
import functools
import math

import jax
import jax.numpy as jnp
from jax import lax
from jax.experimental import pallas as pl
from jax.experimental.pallas import tpu as pltpu

F32 = jnp.float32
BF16 = jnp.bfloat16

NORM_EPS = 1e-6
SSD_HEAD_DIM = 64
SSD_GROUPS = 2
SSD_STATE = 128
SSD_CONV = 4
SSD_CHUNK = 128
ATT_QK_DIM = 64
ATT_V_DIM = 128
LANES = 128
CONV_HALO = 8

FFN_ROWS = 512
FFN_COLS = 256
ATT_BLOCK = 512
VMEM_LIMIT = 56 * 1024 * 1024


def _resident(shape):
    nd = len(shape)
    return pl.BlockSpec(shape, lambda *_: (0,) * nd, pipeline_mode=pl.Buffered(1))


def _rms_norm(x, w):
    return x * lax.rsqrt(jnp.mean(x * x, axis=-1, keepdims=True) + NORM_EPS) * w


def _silu(x):
    return x * jax.nn.sigmoid(x)


def _swiglu(h, wg_ref, wu_ref, wd_ref):
    d_ff = wg_ref.shape[1]
    acc = None
    for c in range(d_ff // FFN_COLS):
        sl = slice(c * FFN_COLS, (c + 1) * FFN_COLS)
        g = jnp.dot(h, wg_ref[:, sl], preferred_element_type=F32)
        u = jnp.dot(h, wu_ref[:, sl], preferred_element_type=F32)
        a = (_silu(g) * u).astype(BF16)
        d = jnp.dot(a, wd_ref[sl, :], preferred_element_type=F32)
        acc = d if acc is None else acc + d
    return acc


def _ffn_kernel(x_ref, nw_ref, wg_ref, wu_ref, wd_ref, o_ref):
    x = x_ref[...]
    h = _rms_norm(x, nw_ref[...]).astype(BF16)
    o_ref[...] = x + 0.5 * _swiglu(h, wg_ref, wu_ref, wd_ref)


def _ffn(x2d, norm_w, wg, wu, wd):
    t, d = x2d.shape
    tm = min(FFN_ROWS, t)
    row = pl.BlockSpec((tm, d), lambda i: (i, 0))
    return pl.pallas_call(
        _ffn_kernel,
        grid=(t // tm,),
        in_specs=[row, _resident((1, d)), _resident(wg.shape), _resident(wu.shape), _resident(wd.shape)],
        out_specs=row,
        out_shape=jax.ShapeDtypeStruct((t, d), F32),
        compiler_params=pltpu.CompilerParams(dimension_semantics=("arbitrary",), vmem_limit_bytes=VMEM_LIMIT),
        name="ffn",
    )(x2d, norm_w, wg, wu, wd)


def _qk_norm(t, w2, scale):
    lo = lax.broadcasted_iota(jnp.int32, t.shape, 1) < ATT_QK_DIM
    x2 = t * t
    s_lo = jnp.sum(jnp.where(lo, x2, 0.0), axis=-1, keepdims=True)
    s_hi = jnp.sum(jnp.where(lo, 0.0, x2), axis=-1, keepdims=True)
    r = lax.rsqrt(jnp.where(lo, s_lo, s_hi) * (1.0 / ATT_QK_DIM) + NORM_EPS)
    return t * r * (w2 * scale)


def _proj_kernel(x_ref, nw_ref, wz_ref, wxbc_ref, wdt_ref, wq_ref, wk_ref, wv_ref, qw_ref, kw_ref,
                 z_ref, xbc_ref, dt_ref, q_ref, k_ref, v_ref):
    h = _rms_norm(x_ref[...], nw_ref[...]).astype(BF16)
    z_ref[...] = jnp.dot(h, wz_ref[...], preferred_element_type=F32).astype(BF16)
    xbc_ref[...] = jnp.dot(h, wxbc_ref[...], preferred_element_type=F32).astype(BF16)
    dt_ref[...] = jnp.dot(h, wdt_ref[...], preferred_element_type=F32)
    v_ref[...] = jnp.dot(h, wv_ref[...], preferred_element_type=F32).astype(BF16)
    q = jnp.dot(h, wq_ref[...], preferred_element_type=F32)
    k = jnp.dot(h, wk_ref[...], preferred_element_type=F32)
    n_heads = q.shape[1] // LANES
    for hd in range(n_heads):
        sl = slice(hd * LANES, (hd + 1) * LANES)
        q_ref[:, sl] = _qk_norm(q[:, sl], qw_ref[...], ATT_QK_DIM ** -0.5 * math.log2(math.e)).astype(BF16)
        k_ref[:, sl] = _qk_norm(k[:, sl], kw_ref[...], 1.0).astype(BF16)


def _proj(x2d, norm_w, wz, wxbc, wdt, wq, wk, wv, qw2, kw2):
    t, d = x2d.shape
    tm = min(FFN_ROWS, t)
    row = lambda n: pl.BlockSpec((tm, n), lambda i: (i, 0))
    widths = (wz.shape[1], wxbc.shape[1], wdt.shape[1], wq.shape[1], wk.shape[1], wv.shape[1])
    dtypes = (BF16, BF16, F32, BF16, BF16, BF16)
    return pl.pallas_call(
        _proj_kernel,
        grid=(t // tm,),
        in_specs=[row(d), _resident((1, d))] + [_resident(w.shape) for w in (wz, wxbc, wdt, wq, wk, wv)]
                 + [_resident((1, LANES)), _resident((1, LANES))],
        out_specs=[row(n) for n in widths],
        out_shape=[jax.ShapeDtypeStruct((t, n), dt) for n, dt in zip(widths, dtypes)],
        compiler_params=pltpu.CompilerParams(dimension_semantics=("arbitrary",), vmem_limit_bytes=VMEM_LIMIT),
        name="proj",
    )(x2d, norm_w, wz, wxbc, wdt, wq, wk, wv, qw2, kw2)


def _lane_bcast(x, col):
    return jnp.broadcast_to(x[:, col:col + 1], x.shape)


def _expand_heads(x, pair):
    lo = lax.broadcasted_iota(jnp.int32, x.shape, 1) < SSD_HEAD_DIM
    return jnp.where(lo, _lane_bcast(x, 2 * pair), _lane_bcast(x, 2 * pair + 1))


def _ssd_kernel(z_ref, xbc_ref, dt_ref, cw_ref, cb_ref, dtb_ref, alog_ref, dskip_ref, nw_ref,
                y_ref, ext_ref, state_ref):
    L = SSD_CHUNK
    width = z_ref.shape[-1]
    n_pairs = width // LANES
    heads_per_group = (width // SSD_HEAD_DIM) // SSD_GROUPS
    gw = width // SSD_GROUPS

    @pl.when(pl.program_id(1) == 0)
    def _():
        ext_ref[0:CONV_HALO, :] = jnp.zeros((CONV_HALO, ext_ref.shape[1]), F32)
        state_ref[...] = jnp.zeros(state_ref.shape, F32)

    ext_ref[CONV_HALO:CONV_HALO + L, :] = xbc_ref[...].astype(F32)
    conv = cb_ref[...]
    for kk in range(SSD_CONV):
        off = CONV_HALO - (SSD_CONV - 1) + kk
        conv = conv + cw_ref[kk:kk + 1, :] * ext_ref[off:off + L, :]
    ext_ref[0:CONV_HALO, :] = ext_ref[L:L + CONV_HALO, :]
    act = _silu(conv)
    xs = act[:, :width]
    bm = act[:, width:width + SSD_GROUPS * SSD_STATE].astype(BF16)
    cm = act[:, width + SSD_GROUPS * SSD_STATE:]

    raw = dt_ref[...] + dtb_ref[...]
    dt = jnp.maximum(raw, 0.0) + jnp.log1p(jnp.exp(-jnp.abs(raw)))
    dta = dt * (-jnp.exp(alog_ref[...]))
    ri = lax.broadcasted_iota(jnp.int32, (L, L), 0)
    ci = lax.broadcasted_iota(jnp.int32, (L, L), 1)
    causal = ri >= ci
    a_cs = jnp.dot(causal.astype(F32), dta, preferred_element_type=F32, precision=lax.Precision.HIGHEST)
    a_last = a_cs[L - 1:L, :]
    ea = jnp.exp(a_cs)
    f = dt * jnp.exp(a_last - a_cs)
    a_cs_t = a_cs.T
    dt_t = dt.T

    lo = lax.broadcasted_iota(jnp.int32, (L, LANES), 1) < SSD_HEAD_DIM
    xs_bf = xs.astype(BF16)
    y_tiles = []
    for pair in range(n_pairs):
        g = (2 * pair) // heads_per_group
        cm_g = cm[:, g * SSD_STATE:(g + 1) * SSD_STATE]
        bm_g = bm[:, g * SSD_STATE:(g + 1) * SSD_STATE]
        cb_g = lax.dot_general(cm_g.astype(BF16), bm_g, (((1,), (1,)), ((), ())), preferred_element_type=F32)
        sl = slice(pair * LANES, (pair + 1) * LANES)
        x_pair = xs_bf[:, sl]
        prev_pair = state_ref[:, sl].astype(BF16)
        y_pair = None
        for half in range(2):
            hd = 2 * pair + half
            seg = _lane_bcast(a_cs, hd) - a_cs_t[hd:hd + 1, :]
            decay = jnp.where(causal, jnp.exp(jnp.where(causal, seg, 0.0)), 0.0)
            m_h = (cb_g * decay * dt_t[hd:hd + 1, :]).astype(BF16)
            c_h = (cm_g * _lane_bcast(ea, hd)).astype(BF16)
            keep = lo if half == 0 else jnp.logical_not(lo)
            zero = jnp.zeros((L, LANES), BF16)
            rhs = jnp.concatenate([jnp.where(keep, x_pair, zero), jnp.where(keep, prev_pair, zero)], axis=0)
            yh = jnp.dot(jnp.concatenate([m_h, c_h], axis=1), rhs, preferred_element_type=F32)
            y_pair = yh if y_pair is None else y_pair + yh
        y_tiles.append(y_pair)

    for pair in range(n_pairs):
        g = (2 * pair) // heads_per_group
        sl = slice(pair * LANES, (pair + 1) * LANES)
        bm_g = bm[:, g * SSD_STATE:(g + 1) * SSD_STATE]
        xw = (xs[:, sl] * _expand_heads(f, pair)).astype(BF16)
        new = lax.dot_general(bm_g, xw, (((0,), (0,)), ((), ())), preferred_element_type=F32)
        chunk_decay = _expand_heads(ea[L - CONV_HALO:L, :], pair)[CONV_HALO - 1:CONV_HALO, :]
        state_ref[:, sl] = state_ref[:, sl] * chunk_decay + new

    y = jnp.concatenate(y_tiles, axis=1) + xs * dskip_ref[...]
    gated = y * _silu(z_ref[...].astype(F32))
    for g in range(SSD_GROUPS):
        sl = slice(g * gw, (g + 1) * gw)
        y_ref[:, sl] = _rms_norm(gated[:, sl], nw_ref[:, sl]).astype(BF16)


def _ssd(z, xbc, dt, conv_w, conv_b, dt_bias, a_log, d_skip_exp, norm_w):
    bsz, s, width = z.shape
    L = SSD_CHUNK
    cw = xbc.shape[-1]
    blk = lambda n: pl.BlockSpec((None, L, n), lambda b, c: (b, c, 0))
    return pl.pallas_call(
        _ssd_kernel,
        grid=(bsz, s // L),
        in_specs=[blk(width), blk(cw), blk(LANES), _resident(conv_w.shape), _resident((1, cw)),
                  _resident((1, LANES)), _resident((1, LANES)), _resident((1, width)), _resident((1, width))],
        out_specs=blk(width),
        out_shape=jax.ShapeDtypeStruct((bsz, s, width), BF16),
        scratch_shapes=[pltpu.VMEM((CONV_HALO + L, cw), F32), pltpu.VMEM((SSD_STATE, width), F32)],
        compiler_params=pltpu.CompilerParams(dimension_semantics=("arbitrary", "arbitrary"),
                                             vmem_limit_bytes=VMEM_LIMIT),
        name="ssd",
    )(z, xbc, dt, conv_w, conv_b, dt_bias, a_log, d_skip_exp, norm_w)


def _attn_kernel(q_ref, k_ref, v_ref, lq1_ref, lk1_ref, lq2_ref, lk2_ref, sw_ref, o_ref, s_ref, acc_ref, *,
                 lambda_init):
    t = q_ref.shape[0]
    qi = pl.program_id(2)
    q = q_ref[...]
    lo = lax.broadcasted_iota(jnp.int32, q.shape, 1) < ATT_QK_DIM
    zero = jnp.zeros_like(q)
    q_t = (jnp.where(lo, q, zero).T, jnp.where(lo, zero, q).T)
    below_diag = (lax.broadcasted_iota(jnp.int32, (t, t), 0) <= lax.broadcasted_iota(jnp.int32, (t, t), 1))

    def qk(j, slot):
        k = k_ref[pl.ds(pl.multiple_of(j * t, t), t), :]
        for mp in range(2):
            s_ref[slot, mp] = jnp.dot(k, q_t[mp], preferred_element_type=F32)

    def softmax_pv(j, slot, stats, first=False):
        v_t = v_ref[pl.ds(pl.multiple_of(j * t, t), t), :].T
        out = []
        for mp in range(2):
            s = s_ref[slot, mp]
            if first:
                s = jnp.where(below_diag, s, -jnp.inf)
                m_new = jnp.max(s, axis=0, keepdims=True)
                p = jnp.exp2(s - m_new)
                l_new = jnp.sum(p, axis=0, keepdims=True)
                acc_ref[mp] = jnp.dot(v_t, p.astype(BF16), preferred_element_type=F32)
            else:
                m, l = stats[mp]
                m_new = jnp.maximum(m, jnp.max(s, axis=0, keepdims=True))
                alpha = jnp.exp2(m - m_new)
                p = jnp.exp2(s - m_new)
                l_new = alpha * l + jnp.sum(p, axis=0, keepdims=True)
                acc_ref[mp] = alpha * acc_ref[mp] + jnp.dot(v_t, p.astype(BF16), preferred_element_type=F32)
            out.append((m_new, l_new))
        return tuple(out)

    qk(qi, 0)
    qk(0, 1)
    stats = softmax_pv(qi, 0, None, first=True)

    def pair(i, st):
        ja = 2 * i
        qk(ja + 1, 0)
        st = softmax_pv(ja, 1, st)
        qk(ja + 2, 1)
        return softmax_pv(ja + 1, 0, st)

    stats = lax.fori_loop(0, qi // 2, pair, stats)
    stats = lax.cond(qi % 2 == 1, lambda st: softmax_pv(qi - 1, 1, st), lambda st: st, stats)

    lam = (jnp.exp(jnp.sum(lq1_ref[...] * lk1_ref[...], axis=-1, keepdims=True))
           - jnp.exp(jnp.sum(lq2_ref[...] * lk2_ref[...], axis=-1, keepdims=True)) + lambda_init)
    (_, l1), (_, l2) = stats
    o_t = acc_ref[0] / l1 - lam * (acc_ref[1] / l2)
    o_ref[...] = (_rms_norm(o_t.T, sw_ref[...]) * (1.0 - lambda_init)).astype(BF16)


def _attn(q, k, v, lq1, lk1, lq2, lk2, subln_w, lambda_init):
    bsz, s, width = q.shape
    n_heads = width // ATT_V_DIM
    t = min(ATT_BLOCK, s)
    q_spec = pl.BlockSpec((None, t, LANES), lambda b, h, i: (b, i, h))
    kv_spec = pl.BlockSpec((None, s, LANES), lambda b, h, i: (b, 0, h))
    small = _resident((1, ATT_QK_DIM))
    return pl.pallas_call(
        functools.partial(_attn_kernel, lambda_init=lambda_init),
        grid=(bsz, n_heads, s // t),
        in_specs=[q_spec, kv_spec, kv_spec, small, small, small, small, _resident((1, ATT_V_DIM))],
        out_specs=q_spec,
        out_shape=jax.ShapeDtypeStruct((bsz, s, width), BF16),
        scratch_shapes=[pltpu.VMEM((2, 2, t, t), F32), pltpu.VMEM((2, ATT_V_DIM, t), F32)],
        compiler_params=pltpu.CompilerParams(dimension_semantics=("arbitrary",) * 3, vmem_limit_bytes=VMEM_LIMIT),
        name="attn",
    )(q, k, v, lq1, lk1, lq2, lk2, subln_w)


def _out_kernel(x_ref, ys_ref, ya_ref, wos_ref, woa_ref, nw_ref, wg_ref, wu_ref, wd_ref, o_ref):
    x = (x_ref[...] + jnp.dot(ys_ref[...], wos_ref[...], preferred_element_type=F32)
         + jnp.dot(ya_ref[...], woa_ref[...], preferred_element_type=F32))
    h = _rms_norm(x, nw_ref[...]).astype(BF16)
    o_ref[...] = x + 0.5 * _swiglu(h, wg_ref, wu_ref, wd_ref)


def _out(x2d, y_ssd, y_att, wo_ssd, wo_att, norm_w, wg, wu, wd):
    t, d = x2d.shape
    tm = min(FFN_ROWS, t)
    row = lambda n: pl.BlockSpec((tm, n), lambda i: (i, 0))
    return pl.pallas_call(
        _out_kernel,
        grid=(t // tm,),
        in_specs=[row(d), row(y_ssd.shape[1]), row(y_att.shape[1]), _resident(wo_ssd.shape), _resident(wo_att.shape),
                  _resident((1, d)), _resident(wg.shape), _resident(wu.shape), _resident(wd.shape)],
        out_specs=row(d),
        out_shape=jax.ShapeDtypeStruct((t, d), F32),
        compiler_params=pltpu.CompilerParams(dimension_semantics=("arbitrary",), vmem_limit_bytes=VMEM_LIMIT),
        name="out_ffn",
    )(x2d, y_ssd, y_att, wo_ssd, wo_att, norm_w, wg, wu, wd)


def _pad_lanes(v):
    return jnp.zeros((1, LANES), F32).at[0, :v.shape[0]].set(v.astype(F32))


def kernel(x, ffn1_norm_w, ffn1_w_gate, ffn1_w_up, ffn1_w_down, mix_norm_w, w_in, conv_w, conv_b, dt_bias, a_log,
           d_skip, ssd_norm_w, q_norm_w, k_norm_w, lambda_q1, lambda_k1, lambda_q2, lambda_k2, attn_subln_w, w_out,
           ffn2_norm_w, ffn2_w_gate, ffn2_w_up, ffn2_w_down):
    bsz, s, d = x.shape
    depth = w_in.shape[0]
    ssd_heads = dt_bias.shape[1]
    ssd_width = ssd_heads * SSD_HEAD_DIM
    conv_ch = conv_w.shape[2]
    att_width = w_out.shape[1] - ssd_width
    qk_width = (att_width // ATT_V_DIM) * 2 * ATT_QK_DIM
    sizes = (ssd_width, conv_ch, ssd_heads, qk_width, qk_width, att_width)
    offs = [0]
    for n in sizes:
        offs.append(offs[-1] + n)
    row = lambda v: v.astype(F32).reshape(1, -1)

    x2d = x.reshape(bsz * s, d)
    for l in range(depth):
        lambda_init = 0.8 - 0.6 * math.exp(-0.3 * l)
        w = w_in[l].astype(BF16)
        wz, wxbc, wdt, wq, wk, wv = (w[:, offs[i]:offs[i + 1]] for i in range(6))
        wdt = jnp.pad(wdt, ((0, 0), (0, LANES - ssd_heads)))
        wo = w_out[l].astype(BF16)

        x1 = _ffn(x2d, row(ffn1_norm_w[l]), ffn1_w_gate[l].astype(BF16), ffn1_w_up[l].astype(BF16),
                  ffn1_w_down[l].astype(BF16))
        z, xbc, dt, q, k, v = _proj(x1, row(mix_norm_w[l]), wz, wxbc, wdt, wq, wk, wv,
                                    row(jnp.tile(q_norm_w[l], 2)), row(jnp.tile(k_norm_w[l], 2)))
        y_ssd = _ssd(z.reshape(bsz, s, -1), xbc.reshape(bsz, s, -1), dt.reshape(bsz, s, -1),
                     conv_w[l].astype(F32), row(conv_b[l]), _pad_lanes(dt_bias[l]), _pad_lanes(a_log[l]),
                     row(jnp.repeat(d_skip[l], SSD_HEAD_DIM)), row(ssd_norm_w[l]))
        y_att = _attn(q.reshape(bsz, s, -1), k.reshape(bsz, s, -1), v.reshape(bsz, s, -1),
                      row(lambda_q1[l]), row(lambda_k1[l]), row(lambda_q2[l]), row(lambda_k2[l]),
                      row(attn_subln_w[l]), lambda_init)
        x2d = _out(x1, y_ssd.reshape(bsz * s, -1), y_att.reshape(bsz * s, -1), wo[:ssd_width], wo[ssd_width:],
                   row(ffn2_norm_w[l]), ffn2_w_gate[l].astype(BF16), ffn2_w_up[l].astype(BF16),
                   ffn2_w_down[l].astype(BF16))
    return x2d.reshape(bsz, s, d)
```

```python
import functools
import math

import jax
import jax.numpy as jnp
from jax import lax
from jax.experimental import pallas as pl
from jax.experimental.pallas import tpu as pltpu

F32 = jnp.float32
BF16 = jnp.bfloat16

NORM_EPS = 1e-6
SSD_HEAD_DIM = 64
SSD_GROUPS = 2
SSD_STATE = 128
SSD_CONV = 4
SSD_CHUNK = 128
ATT_QK_DIM = 64
ATT_V_DIM = 128
LANES = 128
CONV_HALO = 8

FFN_ROWS = 512
FFN_COLS = 256
ATT_BLOCK = 512
VMEM_LIMIT = 56 * 1024 * 1024


def _resident(shape):
    nd = len(shape)
    return pl.BlockSpec(shape, lambda *_: (0,) * nd, pipeline_mode=pl.Buffered(1))


def _rms_norm(x, w):
    return x * lax.rsqrt(jnp.mean(x * x, axis=-1, keepdims=True) + NORM_EPS) * w


def _silu(x):
    return x * jax.nn.sigmoid(x)


def _swiglu(h, wg_ref, wu_ref, wd_ref):
    d_ff = wg_ref.shape[1]
    acc = None
    for c in range(d_ff // FFN_COLS):
        sl = slice(c * FFN_COLS, (c + 1) * FFN_COLS)
        g = jnp.dot(h, wg_ref[:, sl], preferred_element_type=F32)
        u = jnp.dot(h, wu_ref[:, sl], preferred_element_type=F32)
        a = (_silu(g) * u).astype(BF16)
        d = jnp.dot(a, wd_ref[sl, :], preferred_element_type=F32)
        acc = d if acc is None else acc + d
    return acc


def _ffn_kernel(x_ref, nw_ref, wg_ref, wu_ref, wd_ref, o_ref):
    x = x_ref[...]
    h = _rms_norm(x, nw_ref[...]).astype(BF16)
    o_ref[...] = x + 0.5 * _swiglu(h, wg_ref, wu_ref, wd_ref)


def _ffn(x2d, norm_w, wg, wu, wd):
    t, d = x2d.shape
    tm = min(FFN_ROWS, t)
    row = pl.BlockSpec((tm, d), lambda i: (i, 0))
    return pl.pallas_call(
        _ffn_kernel,
        grid=(t // tm,),
        in_specs=[row, _resident((1, d)), _resident(wg.shape), _resident(wu.shape), _resident(wd.shape)],
        out_specs=row,
        out_shape=jax.ShapeDtypeStruct((t, d), F32),
        compiler_params=pltpu.CompilerParams(dimension_semantics=("arbitrary",), vmem_limit_bytes=VMEM_LIMIT),
        name="ffn",
    )(x2d, norm_w, wg, wu, wd)


def _qk_norm(t, w2, scale):
    lo = lax.broadcasted_iota(jnp.int32, t.shape, 1) < ATT_QK_DIM
    x2 = t * t
    s_lo = jnp.sum(jnp.where(lo, x2, 0.0), axis=-1, keepdims=True)
    s_hi = jnp.sum(jnp.where(lo, 0.0, x2), axis=-1, keepdims=True)
    r = lax.rsqrt(jnp.where(lo, s_lo, s_hi) * (1.0 / ATT_QK_DIM) + NORM_EPS)
    return t * r * (w2 * scale)


def _proj_kernel(x_ref, nw_ref, wz_ref, wxbc_ref, wdt_ref, wq_ref, wk_ref, wv_ref, qw_ref, kw_ref,
                 z_ref, xbc_ref, dt_ref, q_ref, k_ref, v_ref):
    h = _rms_norm(x_ref[...], nw_ref[...]).astype(BF16)
    z_ref[...] = jnp.dot(h, wz_ref[...], preferred_element_type=F32).astype(BF16)
    xbc_ref[...] = jnp.dot(h, wxbc_ref[...], preferred_element_type=F32).astype(BF16)
    dt_ref[...] = jnp.dot(h, wdt_ref[...], preferred_element_type=F32)
    v_ref[...] = jnp.dot(h, wv_ref[...], preferred_element_type=F32).astype(BF16)
    q = jnp.dot(h, wq_ref[...], preferred_element_type=F32)
    k = jnp.dot(h, wk_ref[...], preferred_element_type=F32)
    n_heads = q.shape[1] // LANES
    for hd in range(n_heads):
        sl = slice(hd * LANES, (hd + 1) * LANES)
        q_ref[:, sl] = _qk_norm(q[:, sl], qw_ref[...], ATT_QK_DIM ** -0.5 * math.log2(math.e)).astype(BF16)
        k_ref[:, sl] = _qk_norm(k[:, sl], kw_ref[...], 1.0).astype(BF16)


def _proj(x2d, norm_w, wz, wxbc, wdt, wq, wk, wv, qw2, kw2):
    t, d = x2d.shape
    tm = min(FFN_ROWS, t)
    row = lambda n: pl.BlockSpec((tm, n), lambda i: (i, 0))
    widths = (wz.shape[1], wxbc.shape[1], wdt.shape[1], wq.shape[1], wk.shape[1], wv.shape[1])
    dtypes = (BF16, BF16, F32, BF16, BF16, BF16)
    return pl.pallas_call(
        _proj_kernel,
        grid=(t // tm,),
        in_specs=[row(d), _resident((1, d))] + [_resident(w.shape) for w in (wz, wxbc, wdt, wq, wk, wv)]
                 + [_resident((1, LANES)), _resident((1, LANES))],
        out_specs=[row(n) for n in widths],
        out_shape=[jax.ShapeDtypeStruct((t, n), dt) for n, dt in zip(widths, dtypes)],
        compiler_params=pltpu.CompilerParams(dimension_semantics=("arbitrary",), vmem_limit_bytes=VMEM_LIMIT),
        name="proj",
    )(x2d, norm_w, wz, wxbc, wdt, wq, wk, wv, qw2, kw2)


def _lane_bcast(x, col):
    return jnp.broadcast_to(x[:, col:col + 1], x.shape)


def _expand_heads(x, pair):
    lo = lax.broadcasted_iota(jnp.int32, x.shape, 1) < SSD_HEAD_DIM
    return jnp.where(lo, _lane_bcast(x, 2 * pair), _lane_bcast(x, 2 * pair + 1))


def _ssd_kernel(z_ref, xbc_ref, dt_ref, cw_ref, cb_ref, dtb_ref, alog_ref, dskip_ref, nw_ref,
                y_ref, ext_ref, state_ref):
    L = SSD_CHUNK
    width = z_ref.shape[-1]
    n_pairs = width // LANES
    heads_per_group = (width // SSD_HEAD_DIM) // SSD_GROUPS
    gw = width // SSD_GROUPS

    @pl.when(pl.program_id(1) == 0)
    def _():
        ext_ref[0:CONV_HALO, :] = jnp.zeros((CONV_HALO, ext_ref.shape[1]), F32)
        state_ref[...] = jnp.zeros(state_ref.shape, F32)

    ext_ref[CONV_HALO:CONV_HALO + L, :] = xbc_ref[...].astype(F32)
    conv = cb_ref[...]
    for kk in range(SSD_CONV):
        off = CONV_HALO - (SSD_CONV - 1) + kk
        conv = conv + cw_ref[kk:kk + 1, :] * ext_ref[off:off + L, :]
    ext_ref[0:CONV_HALO, :] = ext_ref[L:L + CONV_HALO, :]
    act = _silu(conv)
    xs = act[:, :width]
    bm = act[:, width:width + SSD_GROUPS * SSD_STATE].astype(BF16)
    cm = act[:, width + SSD_GROUPS * SSD_STATE:]

    raw = dt_ref[...] + dtb_ref[...]
    dt = jnp.maximum(raw, 0.0) + jnp.log1p(jnp.exp(-jnp.abs(raw)))
    dta = dt * (-jnp.exp(alog_ref[...]))
    ri = lax.broadcasted_iota(jnp.int32, (L, L), 0)
    ci = lax.broadcasted_iota(jnp.int32, (L, L), 1)
    causal = ri >= ci
    a_cs = jnp.dot(causal.astype(F32), dta, preferred_element_type=F32, precision=lax.Precision.HIGHEST)
    a_last = a_cs[L - 1:L, :]
    ea = jnp.exp(a_cs)
    f = dt * jnp.exp(a_last - a_cs)
    a_cs_t = a_cs.T
    dt_t = dt.T

    lo = lax.broadcasted_iota(jnp.int32, (L, LANES), 1) < SSD_HEAD_DIM
    xs_bf = xs.astype(BF16)
    y_tiles = []
    for pair in range(n_pairs):
        g = (2 * pair) // heads_per_group
        cm_g = cm[:, g * SSD_STATE:(g + 1) * SSD_STATE]
        bm_g = bm[:, g * SSD_STATE:(g + 1) * SSD_STATE]
        cb_g = lax.dot_general(cm_g.astype(BF16), bm_g, (((1,), (1,)), ((), ())), preferred_element_type=F32)
        sl = slice(pair * LANES, (pair + 1) * LANES)
        x_pair = xs_bf[:, sl]
        prev_pair = state_ref[:, sl].astype(BF16)
        y_pair = None
        for half in range(2):
            hd = 2 * pair + half
            seg = _lane_bcast(a_cs, hd) - a_cs_t[hd:hd + 1, :]
            decay = jnp.where(causal, jnp.exp(jnp.where(causal, seg, 0.0)), 0.0)
            m_h = (cb_g * decay * dt_t[hd:hd + 1, :]).astype(BF16)
            c_h = (cm_g * _lane_bcast(ea, hd)).astype(BF16)
            keep = lo if half == 0 else jnp.logical_not(lo)
            zero = jnp.zeros((L, LANES), BF16)
            rhs = jnp.concatenate([jnp.where(keep, x_pair, zero), jnp.where(keep, prev_pair, zero)], axis=0)
            yh = jnp.dot(jnp.concatenate([m_h, c_h], axis=1), rhs, preferred_element_type=F32)
            y_pair = yh if y_pair is None else y_pair + yh
        y_tiles.append(y_pair)

    for pair in range(n_pairs):
        g = (2 * pair) // heads_per_group
        sl = slice(pair * LANES, (pair + 1) * LANES)
        bm_g = bm[:, g * SSD_STATE:(g + 1) * SSD_STATE]
        xw = (xs[:, sl] * _expand_heads(f, pair)).astype(BF16)
        new = lax.dot_general(bm_g, xw, (((0,), (0,)), ((), ())), preferred_element_type=F32)
        chunk_decay = _expand_heads(ea[L - CONV_HALO:L, :], pair)[CONV_HALO - 1:CONV_HALO, :]
        state_ref[:, sl] = state_ref[:, sl] * chunk_decay + new

    y = jnp.concatenate(y_tiles, axis=1) + xs * dskip_ref[...]
    gated = y * _silu(z_ref[...].astype(F32))
    for g in range(SSD_GROUPS):
        sl = slice(g * gw, (g + 1) * gw)
        y_ref[:, sl] = _rms_norm(gated[:, sl], nw_ref[:, sl]).astype(BF16)


def _ssd(z, xbc, dt, conv_w, conv_b, dt_bias, a_log, d_skip_exp, norm_w):
    bsz, s, width = z.shape
    L = SSD_CHUNK
    cw = xbc.shape[-1]
    blk = lambda n: pl.BlockSpec((None, L, n), lambda b, c: (b, c, 0))
    return pl.pallas_call(
        _ssd_kernel,
        grid=(bsz, s // L),
        in_specs=[blk(width), blk(cw), blk(LANES), _resident(conv_w.shape), _resident((1, cw)),
                  _resident((1, LANES)), _resident((1, LANES)), _resident((1, width)), _resident((1, width))],
        out_specs=blk(width),
        out_shape=jax.ShapeDtypeStruct((bsz, s, width), BF16),
        scratch_shapes=[pltpu.VMEM((CONV_HALO + L, cw), F32), pltpu.VMEM((SSD_STATE, width), F32)],
        compiler_params=pltpu.CompilerParams(dimension_semantics=("arbitrary", "arbitrary"),
                                             vmem_limit_bytes=VMEM_LIMIT),
        name="ssd",
    )(z, xbc, dt, conv_w, conv_b, dt_bias, a_log, d_skip_exp, norm_w)


def _attn_kernel(q_ref, k_ref, v_ref, lq1_ref, lk1_ref, lq2_ref, lk2_ref, sw_ref, o_ref, s_ref, acc_ref, *,
                 lambda_init):
    t = o_ref.shape[0]
    hk = t // 2
    n_q = q_ref.shape[0] // t
    qi = pl.program_id(2)

    def q_maps_t(blk):
        q = q_ref[pl.ds(pl.multiple_of(blk * t, t), t), :]
        lo = lax.broadcasted_iota(jnp.int32, q.shape, 1) < ATT_QK_DIM
        zero = jnp.zeros_like(q)
        return jnp.where(lo, q, zero).T, jnp.where(lo, zero, q).T

    def qk(row0, slot, q_t, col0=0):
        k = k_ref[pl.ds(pl.multiple_of(row0, hk), hk), :]
        for mp in range(2):
            s_ref[slot, mp, :, col0:] = jnp.dot(k, q_t[mp][:, col0:], preferred_element_type=F32)

    def softmax_pv(row0, slot, stats, col0=0, mask=None):
        v_t = v_ref[pl.ds(pl.multiple_of(row0, hk), hk), :].T
        out = []
        for mp in range(2):
            m, l = stats[mp]
            s = s_ref[slot, mp, :, col0:]
            if mask is not None:
                s = jnp.where(mask, s, -jnp.inf)
            m_new = jnp.maximum(m[:, col0:], jnp.max(s, axis=0, keepdims=True))
            alpha = jnp.exp2(m[:, col0:] - m_new)
            p = jnp.exp2(s - m_new)
            l_new = alpha * l[:, col0:] + jnp.sum(p, axis=0, keepdims=True)
            acc_ref[mp, :, col0:] = (alpha * acc_ref[mp, :, col0:]
                                     + jnp.dot(v_t, p.astype(BF16), preferred_element_type=F32))
            if col0:
                m_new = jnp.concatenate([m[:, :col0], m_new], axis=1)
                l_new = jnp.concatenate([l[:, :col0], l_new], axis=1)
            out.append((m_new, l_new))
        return tuple(out)

    q_t = q_maps_t(qi)

    @pl.when(qi == 0)
    def _():
        qk(0, 0, q_t)

    acc_ref[...] = jnp.zeros(acc_ref.shape, F32)
    init = (jnp.full((1, t), -jnp.inf, F32), jnp.zeros((1, t), F32))

    def block(j, st):
        row0 = j * t
        qk(row0 + hk, 1, q_t)
        st = softmax_pv(row0, 0, st)
        qk(row0 + t, 0, q_t)
        return softmax_pv(row0 + hk, 1, st)

    stats = lax.fori_loop(0, qi, block, (init, init))

    row0 = qi * t
    r = lax.broadcasted_iota(jnp.int32, (hk, t), 0)
    c = lax.broadcasted_iota(jnp.int32, (hk, t), 1)
    qk(row0 + hk, 1, q_t, col0=hk)
    stats = softmax_pv(row0, 0, stats, mask=r <= c)
    qk(0, 0, q_maps_t(jnp.minimum(qi + 1, n_q - 1)))
    stats = softmax_pv(row0 + hk, 1, stats, col0=hk, mask=(r <= c)[:, :hk])

    lam = (jnp.exp(jnp.sum(lq1_ref[...] * lk1_ref[...], axis=-1, keepdims=True))
           - jnp.exp(jnp.sum(lq2_ref[...] * lk2_ref[...], axis=-1, keepdims=True)) + lambda_init)
    (_, l1), (_, l2) = stats
    o_t = acc_ref[0] / l1 - lam * (acc_ref[1] / l2)
    o_ref[...] = (_rms_norm(o_t.T, sw_ref[...]) * (1.0 - lambda_init)).astype(BF16)


def _attn(q, k, v, lq1, lk1, lq2, lk2, subln_w, lambda_init):
    bsz, s, width = q.shape
    n_heads = width // ATT_V_DIM
    t = min(ATT_BLOCK, s)
    q_spec = pl.BlockSpec((None, t, LANES), lambda b, h, i: (b, i, h))
    kv_spec = pl.BlockSpec((None, s, LANES), lambda b, h, i: (b, 0, h))
    small = _resident((1, ATT_QK_DIM))
    return pl.pallas_call(
        functools.partial(_attn_kernel, lambda_init=lambda_init),
        grid=(bsz, n_heads, s // t),
        in_specs=[kv_spec, kv_spec, kv_spec, small, small, small, small, _resident((1, ATT_V_DIM))],
        out_specs=q_spec,
        out_shape=jax.ShapeDtypeStruct((bsz, s, width), BF16),
        scratch_shapes=[pltpu.VMEM((2, 2, t // 2, t), F32), pltpu.VMEM((2, ATT_V_DIM, t), F32)],
        compiler_params=pltpu.CompilerParams(dimension_semantics=("arbitrary",) * 3, vmem_limit_bytes=VMEM_LIMIT),
        name="attn",
    )(q, k, v, lq1, lk1, lq2, lk2, subln_w)


def _out_kernel(x_ref, ys_ref, ya_ref, wos_ref, woa_ref, nw_ref, wg_ref, wu_ref, wd_ref, o_ref):
    x = (x_ref[...] + jnp.dot(ys_ref[...], wos_ref[...], preferred_element_type=F32)
         + jnp.dot(ya_ref[...], woa_ref[...], preferred_element_type=F32))
    h = _rms_norm(x, nw_ref[...]).astype(BF16)
    o_ref[...] = x + 0.5 * _swiglu(h, wg_ref, wu_ref, wd_ref)


def _out(x2d, y_ssd, y_att, wo_ssd, wo_att, norm_w, wg, wu, wd):
    t, d = x2d.shape
    tm = min(FFN_ROWS, t)
    row = lambda n: pl.BlockSpec((tm, n), lambda i: (i, 0))
    return pl.pallas_call(
        _out_kernel,
        grid=(t // tm,),
        in_specs=[row(d), row(y_ssd.shape[1]), row(y_att.shape[1]), _resident(wo_ssd.shape), _resident(wo_att.shape),
                  _resident((1, d)), _resident(wg.shape), _resident(wu.shape), _resident(wd.shape)],
        out_specs=row(d),
        out_shape=jax.ShapeDtypeStruct((t, d), F32),
        compiler_params=pltpu.CompilerParams(dimension_semantics=("arbitrary",), vmem_limit_bytes=VMEM_LIMIT),
        name="out_ffn",
    )(x2d, y_ssd, y_att, wo_ssd, wo_att, norm_w, wg, wu, wd)


def _pad_lanes(v):
    return jnp.zeros((1, LANES), F32).at[0, :v.shape[0]].set(v.astype(F32))


def kernel(x, ffn1_norm_w, ffn1_w_gate, ffn1_w_up, ffn1_w_down, mix_norm_w, w_in, conv_w, conv_b, dt_bias, a_log,
           d_skip, ssd_norm_w, q_norm_w, k_norm_w, lambda_q1, lambda_k1, lambda_q2, lambda_k2, attn_subln_w, w_out,
           ffn2_norm_w, ffn2_w_gate, ffn2_w_up, ffn2_w_down):
    bsz, s, d = x.shape
    depth = w_in.shape[0]
    ssd_heads = dt_bias.shape[1]
    ssd_width = ssd_heads * SSD_HEAD_DIM
    conv_ch = conv_w.shape[2]
    att_width = w_out.shape[1] - ssd_width
    qk_width = (att_width // ATT_V_DIM) * 2 * ATT_QK_DIM
    sizes = (ssd_width, conv_ch, ssd_heads, qk_width, qk_width, att_width)
    offs = [0]
    for n in sizes:
        offs.append(offs[-1] + n)
    row = lambda v: v.astype(F32).reshape(1, -1)

    x2d = x.reshape(bsz * s, d)
    for l in range(depth):
        lambda_init = 0.8 - 0.6 * math.exp(-0.3 * l)
        w = w_in[l].astype(BF16)
        wz, wxbc, wdt, wq, wk, wv = (w[:, offs[i]:offs[i + 1]] for i in range(6))
        wdt = jnp.pad(wdt, ((0, 0), (0, LANES - ssd_heads)))
        wo = w_out[l].astype(BF16)

        x1 = _ffn(x2d, row(ffn1_norm_w[l]), ffn1_w_gate[l].astype(BF16), ffn1_w_up[l].astype(BF16),
                  ffn1_w_down[l].astype(BF16))
        z, xbc, dt, q, k, v = _proj(x1, row(mix_norm_w[l]), wz, wxbc, wdt, wq, wk, wv,
                                    row(jnp.tile(q_norm_w[l], 2)), row(jnp.tile(k_norm_w[l], 2)))
        y_ssd = _ssd(z.reshape(bsz, s, -1), xbc.reshape(bsz, s, -1), dt.reshape(bsz, s, -1),
                     conv_w[l].astype(F32), row(conv_b[l]), _pad_lanes(dt_bias[l]), _pad_lanes(a_log[l]),
                     row(jnp.repeat(d_skip[l], SSD_HEAD_DIM)), row(ssd_norm_w[l]))
        y_att = _attn(q.reshape(bsz, s, -1), k.reshape(bsz, s, -1), v.reshape(bsz, s, -1),
                      row(lambda_q1[l]), row(lambda_k1[l]), row(lambda_q2[l]), row(lambda_k2[l]),
                      row(attn_subln_w[l]), lambda_init)
        x2d = _out(x1, y_ssd.reshape(bsz * s, -1), y_att.reshape(bsz * s, -1), wo[:ssd_width], wo[ssd_width:],
                   row(ffn2_norm_w[l]), ffn2_w_gate[l].astype(BF16), ffn2_w_up[l].astype(BF16),
                   ffn2_w_down[l].astype(BF16))
    return x2d.reshape(bsz, s, d)
```

```python
import functools
import math

import jax
import jax.numpy as jnp
from jax import lax
from jax.experimental import pallas as pl
from jax.experimental.pallas import tpu as pltpu

F32 = jnp.float32
BF16 = jnp.bfloat16

NORM_EPS = 1e-6
SSD_HEAD_DIM = 64
SSD_GROUPS = 2
SSD_STATE = 128
SSD_CONV = 4
SSD_CHUNK = 128
ATT_QK_DIM = 64
ATT_V_DIM = 128
LANES = 128
CONV_HALO = 8

FFN_ROWS = 512
FFN_COLS = 256
ATT_BLOCK_Q = 1024
ATT_SUM_ROWS = 16
VMEM_LIMIT = 56 * 1024 * 1024


def _resident(shape):
    nd = len(shape)
    return pl.BlockSpec(shape, lambda *_: (0,) * nd, pipeline_mode=pl.Buffered(1))


def _rms_norm(x, w):
    return x * lax.rsqrt(jnp.mean(x * x, axis=-1, keepdims=True) + NORM_EPS) * w


def _silu(x):
    return x * jax.nn.sigmoid(x)


def _swiglu(h, wg_ref, wu_ref, wd_ref):
    d_ff = wg_ref.shape[1]
    acc = None
    for c in range(d_ff // FFN_COLS):
        sl = slice(c * FFN_COLS, (c + 1) * FFN_COLS)
        g = jnp.dot(h, wg_ref[:, sl], preferred_element_type=F32)
        u = jnp.dot(h, wu_ref[:, sl], preferred_element_type=F32)
        a = (_silu(g) * u).astype(BF16)
        d = jnp.dot(a, wd_ref[sl, :], preferred_element_type=F32)
        acc = d if acc is None else acc + d
    return acc


def _ffn_kernel(x_ref, nw_ref, wg_ref, wu_ref, wd_ref, o_ref):
    x = x_ref[...]
    h = _rms_norm(x, nw_ref[...]).astype(BF16)
    o_ref[...] = x + 0.5 * _swiglu(h, wg_ref, wu_ref, wd_ref)


def _ffn(x2d, norm_w, wg, wu, wd):
    t, d = x2d.shape
    tm = min(FFN_ROWS, t)
    row = pl.BlockSpec((tm, d), lambda i: (i, 0))
    return pl.pallas_call(
        _ffn_kernel,
        grid=(t // tm,),
        in_specs=[row, _resident((1, d)), _resident(wg.shape), _resident(wu.shape), _resident(wd.shape)],
        out_specs=row,
        out_shape=jax.ShapeDtypeStruct((t, d), F32),
        compiler_params=pltpu.CompilerParams(dimension_semantics=("arbitrary",), vmem_limit_bytes=VMEM_LIMIT),
        name="ffn",
    )(x2d, norm_w, wg, wu, wd)


def _qk_norm(t, w2, scale):
    lo = lax.broadcasted_iota(jnp.int32, t.shape, 1) < ATT_QK_DIM
    x2 = t * t
    s_lo = jnp.sum(jnp.where(lo, x2, 0.0), axis=-1, keepdims=True)
    s_hi = jnp.sum(jnp.where(lo, 0.0, x2), axis=-1, keepdims=True)
    r = lax.rsqrt(jnp.where(lo, s_lo, s_hi) * (1.0 / ATT_QK_DIM) + NORM_EPS)
    return t * r * (w2 * scale)


def _proj_kernel(x_ref, nw_ref, wz_ref, wxbc_ref, wdt_ref, wq_ref, wk_ref, wv_ref, qw_ref, kw_ref,
                 z_ref, xbc_ref, dt_ref, q_ref, k_ref, v_ref):
    h = _rms_norm(x_ref[...], nw_ref[...]).astype(BF16)
    z_ref[...] = jnp.dot(h, wz_ref[...], preferred_element_type=F32).astype(BF16)
    xbc_ref[...] = jnp.dot(h, wxbc_ref[...], preferred_element_type=F32).astype(BF16)
    dt_ref[...] = jnp.dot(h, wdt_ref[...], preferred_element_type=F32)
    v_ref[...] = jnp.dot(h, wv_ref[...], preferred_element_type=F32).astype(BF16)
    q = jnp.dot(h, wq_ref[...], preferred_element_type=F32)
    k = jnp.dot(h, wk_ref[...], preferred_element_type=F32)
    n_heads = q.shape[1] // LANES
    for hd in range(n_heads):
        sl = slice(hd * LANES, (hd + 1) * LANES)
        q_ref[:, sl] = _qk_norm(q[:, sl], qw_ref[...], ATT_QK_DIM ** -0.5 * math.log2(math.e)).astype(BF16)
        k_ref[:, sl] = _qk_norm(k[:, sl], kw_ref[...], 1.0).astype(BF16)


def _proj(x2d, norm_w, wz, wxbc, wdt, wq, wk, wv, qw2, kw2):
    t, d = x2d.shape
    tm = min(FFN_ROWS, t)
    row = lambda n: pl.BlockSpec((tm, n), lambda i: (i, 0))
    widths = (wz.shape[1], wxbc.shape[1], wdt.shape[1], wq.shape[1], wk.shape[1], wv.shape[1])
    dtypes = (BF16, BF16, F32, BF16, BF16, BF16)
    return pl.pallas_call(
        _proj_kernel,
        grid=(t // tm,),
        in_specs=[row(d), _resident((1, d))] + [_resident(w.shape) for w in (wz, wxbc, wdt, wq, wk, wv)]
                 + [_resident((1, LANES)), _resident((1, LANES))],
        out_specs=[row(n) for n in widths],
        out_shape=[jax.ShapeDtypeStruct((t, n), dt) for n, dt in zip(widths, dtypes)],
        compiler_params=pltpu.CompilerParams(dimension_semantics=("arbitrary",), vmem_limit_bytes=VMEM_LIMIT),
        name="proj",
    )(x2d, norm_w, wz, wxbc, wdt, wq, wk, wv, qw2, kw2)


def _lane_bcast(x, col):
    return jnp.broadcast_to(x[:, col:col + 1], x.shape)


def _expand_heads(x, pair):
    lo = lax.broadcasted_iota(jnp.int32, x.shape, 1) < SSD_HEAD_DIM
    return jnp.where(lo, _lane_bcast(x, 2 * pair), _lane_bcast(x, 2 * pair + 1))


def _ssd_kernel(z_ref, xbc_ref, dt_ref, cw_ref, cb_ref, dtb_ref, alog_ref, dskip_ref, nw_ref,
                y_ref, ext_ref, state_ref):
    L = SSD_CHUNK
    width = z_ref.shape[-1]
    n_pairs = width // LANES
    heads_per_group = (width // SSD_HEAD_DIM) // SSD_GROUPS
    gw = width // SSD_GROUPS

    @pl.when(pl.program_id(1) == 0)
    def _():
        ext_ref[0:CONV_HALO, :] = jnp.zeros((CONV_HALO, ext_ref.shape[1]), F32)
        state_ref[...] = jnp.zeros(state_ref.shape, F32)

    ext_ref[CONV_HALO:CONV_HALO + L, :] = xbc_ref[...].astype(F32)
    conv = cb_ref[...]
    for kk in range(SSD_CONV):
        off = CONV_HALO - (SSD_CONV - 1) + kk
        conv = conv + cw_ref[kk:kk + 1, :] * ext_ref[off:off + L, :]
    ext_ref[0:CONV_HALO, :] = ext_ref[L:L + CONV_HALO, :]
    act = _silu(conv)
    xs = act[:, :width]
    bm = act[:, width:width + SSD_GROUPS * SSD_STATE].astype(BF16)
    cm = act[:, width + SSD_GROUPS * SSD_STATE:]

    raw = dt_ref[...] + dtb_ref[...]
    dt = jnp.maximum(raw, 0.0) + jnp.log1p(jnp.exp(-jnp.abs(raw)))
    dta = dt * (-jnp.exp(alog_ref[...]))
    ri = lax.broadcasted_iota(jnp.int32, (L, L), 0)
    ci = lax.broadcasted_iota(jnp.int32, (L, L), 1)
    causal = ri >= ci
    a_cs = jnp.dot(causal.astype(F32), dta, preferred_element_type=F32, precision=lax.Precision.HIGHEST)
    a_last = a_cs[L - 1:L, :]
    ea = jnp.exp(a_cs)
    f = dt * jnp.exp(a_last - a_cs)
    a_cs_t = a_cs.T
    dt_t = dt.T

    lo = lax.broadcasted_iota(jnp.int32, (L, LANES), 1) < SSD_HEAD_DIM
    xs_bf = xs.astype(BF16)
    y_tiles = []
    for pair in range(n_pairs):
        g = (2 * pair) // heads_per_group
        cm_g = cm[:, g * SSD_STATE:(g + 1) * SSD_STATE]
        bm_g = bm[:, g * SSD_STATE:(g + 1) * SSD_STATE]
        cb_g = lax.dot_general(cm_g.astype(BF16), bm_g, (((1,), (1,)), ((), ())), preferred_element_type=F32)
        sl = slice(pair * LANES, (pair + 1) * LANES)
        x_pair = xs_bf[:, sl]
        prev_pair = state_ref[:, sl].astype(BF16)
        y_pair = None
        for half in range(2):
            hd = 2 * pair + half
            seg = _lane_bcast(a_cs, hd) - a_cs_t[hd:hd + 1, :]
            decay = jnp.where(causal, jnp.exp(jnp.where(causal, seg, 0.0)), 0.0)
            m_h = (cb_g * decay * dt_t[hd:hd + 1, :]).astype(BF16)
            c_h = (cm_g * _lane_bcast(ea, hd)).astype(BF16)
            keep = lo if half == 0 else jnp.logical_not(lo)
            zero = jnp.zeros((L, LANES), BF16)
            rhs = jnp.concatenate([jnp.where(keep, x_pair, zero), jnp.where(keep, prev_pair, zero)], axis=0)
            yh = jnp.dot(jnp.concatenate([m_h, c_h], axis=1), rhs, preferred_element_type=F32)
            y_pair = yh if y_pair is None else y_pair + yh
        y_tiles.append(y_pair)

    for pair in range(n_pairs):
        g = (2 * pair) // heads_per_group
        sl = slice(pair * LANES, (pair + 1) * LANES)
        bm_g = bm[:, g * SSD_STATE:(g + 1) * SSD_STATE]
        xw = (xs[:, sl] * _expand_heads(f, pair)).astype(BF16)
        new = lax.dot_general(bm_g, xw, (((0,), (0,)), ((), ())), preferred_element_type=F32)
        chunk_decay = _expand_heads(ea[L - CONV_HALO:L, :], pair)[CONV_HALO - 1:CONV_HALO, :]
        state_ref[:, sl] = state_ref[:, sl] * chunk_decay + new

    y = jnp.concatenate(y_tiles, axis=1) + xs * dskip_ref[...]
    gated = y * _silu(z_ref[...].astype(F32))
    for g in range(SSD_GROUPS):
        sl = slice(g * gw, (g + 1) * gw)
        y_ref[:, sl] = _rms_norm(gated[:, sl], nw_ref[:, sl]).astype(BF16)


def _ssd(z, xbc, dt, conv_w, conv_b, dt_bias, a_log, d_skip_exp, norm_w):
    bsz, s, width = z.shape
    L = SSD_CHUNK
    cw = xbc.shape[-1]
    blk = lambda n: pl.BlockSpec((None, L, n), lambda b, c: (b, c, 0))
    return pl.pallas_call(
        _ssd_kernel,
        grid=(bsz, s // L),
        in_specs=[blk(width), blk(cw), blk(LANES), _resident(conv_w.shape), _resident((1, cw)),
                  _resident((1, LANES)), _resident((1, LANES)), _resident((1, width)), _resident((1, width))],
        out_specs=blk(width),
        out_shape=jax.ShapeDtypeStruct((bsz, s, width), BF16),
        scratch_shapes=[pltpu.VMEM((CONV_HALO + L, cw), F32), pltpu.VMEM((SSD_STATE, width), F32)],
        compiler_params=pltpu.CompilerParams(dimension_semantics=("arbitrary", "arbitrary"),
                                             vmem_limit_bytes=VMEM_LIMIT),
        name="ssd",
    )(z, xbc, dt, conv_w, conv_b, dt_bias, a_log, d_skip_exp, norm_w)


def _attn_kernel(q_ref, k_ref, v_ref, lq1_ref, lk1_ref, lq2_ref, lk2_ref, sw_ref, o_ref, s_ref, acc_ref, *,
                 lambda_init):
    tq = o_ref.shape[0]
    tk = s_ref.shape[2]
    n_q = q_ref.shape[0] // tq
    qi = pl.program_id(2)

    def q_maps_t(blk):
        q = q_ref[pl.ds(pl.multiple_of(blk * tq, tq), tq), :]
        lo = lax.broadcasted_iota(jnp.int32, q.shape, 1) < ATT_QK_DIM
        zero = jnp.zeros_like(q)
        return jnp.where(lo, q, zero).T, jnp.where(lo, zero, q).T

    def qk(row0, slot, q_t, col0=0):
        k = k_ref[pl.ds(pl.multiple_of(row0, tk), tk), :]
        for mp in range(2):
            s_ref[slot, mp, :, col0:] = jnp.dot(k, q_t[mp][:, col0:], preferred_element_type=F32)

    ones_rows = jnp.ones((ATT_SUM_ROWS, tk), BF16)

    def softmax_pv(row0, slot, m, col0=0, mask=None):
        v_ext = jnp.concatenate([v_ref[pl.ds(pl.multiple_of(row0, tk), tk), :].T, ones_rows], axis=0)
        m_out = []
        for mp in range(2):
            s = s_ref[slot, mp, :, col0:]
            if mask is not None:
                s = jnp.where(mask, s, -jnp.inf)
            m_old = m[mp][:, col0:]
            m_new = jnp.maximum(m_old, jnp.max(s, axis=0, keepdims=True))
            p = jnp.exp2(s - m_new).astype(BF16)
            acc_ref[mp, :, col0:] = (jnp.exp2(m_old - m_new) * acc_ref[mp, :, col0:]
                                     + jnp.dot(v_ext, p, preferred_element_type=F32))
            m_out.append(m_new if col0 == 0 else jnp.concatenate([m[mp][:, :col0], m_new], axis=1))
        return tuple(m_out)

    q_t = q_maps_t(qi)

    @pl.when(qi == 0)
    def _():
        qk(0, 0, q_t)

    acc_ref[...] = jnp.zeros(acc_ref.shape, F32)
    neg_inf = jnp.full((1, tq), -jnp.inf, F32)

    def pair(i, m):
        row0 = i * tq
        qk(row0 + tk, 1, q_t)
        m = softmax_pv(row0, 0, m)
        qk(row0 + tq, 0, q_t)
        return softmax_pv(row0 + tk, 1, m)

    m = lax.fori_loop(0, qi, pair, (neg_inf, neg_inf))

    row0 = qi * tq
    causal = (lax.broadcasted_iota(jnp.int32, (tk, tq), 0) <= lax.broadcasted_iota(jnp.int32, (tk, tq), 1))
    qk(row0 + tk, 1, q_t, col0=tk)
    m = softmax_pv(row0, 0, m, mask=causal)
    qk(0, 0, q_maps_t(jnp.minimum(qi + 1, n_q - 1)))
    softmax_pv(row0 + tk, 1, m, col0=tk, mask=causal[:, :tk])

    lam = (jnp.exp(jnp.sum(lq1_ref[...] * lk1_ref[...], axis=-1, keepdims=True))
           - jnp.exp(jnp.sum(lq2_ref[...] * lk2_ref[...], axis=-1, keepdims=True)) + lambda_init)
    o_t = (acc_ref[0, :ATT_V_DIM] / acc_ref[0, ATT_V_DIM:ATT_V_DIM + 1]
           - lam * (acc_ref[1, :ATT_V_DIM] / acc_ref[1, ATT_V_DIM:ATT_V_DIM + 1]))
    o_ref[...] = (_rms_norm(o_t.T, sw_ref[...]) * (1.0 - lambda_init)).astype(BF16)


def _attn(q, k, v, lq1, lk1, lq2, lk2, subln_w, lambda_init):
    bsz, s, width = q.shape
    n_heads = width // ATT_V_DIM
    tq = min(ATT_BLOCK_Q, s)
    o_spec = pl.BlockSpec((None, tq, LANES), lambda b, h, i: (b, i, h))
    head_spec = pl.BlockSpec((None, s, LANES), lambda b, h, i: (b, 0, h))
    small = _resident((1, ATT_QK_DIM))
    return pl.pallas_call(
        functools.partial(_attn_kernel, lambda_init=lambda_init),
        grid=(bsz, n_heads, s // tq),
        in_specs=[head_spec, head_spec, head_spec, small, small, small, small, _resident((1, ATT_V_DIM))],
        out_specs=o_spec,
        out_shape=jax.ShapeDtypeStruct((bsz, s, width), BF16),
        scratch_shapes=[pltpu.VMEM((2, 2, tq // 2, tq), F32),
                        pltpu.VMEM((2, ATT_V_DIM + ATT_SUM_ROWS, tq), F32)],
        compiler_params=pltpu.CompilerParams(dimension_semantics=("arbitrary",) * 3, vmem_limit_bytes=VMEM_LIMIT),
        name="attn",
    )(q, k, v, lq1, lk1, lq2, lk2, subln_w)


def _out_kernel(x_ref, ys_ref, ya_ref, wos_ref, woa_ref, nw_ref, wg_ref, wu_ref, wd_ref, o_ref):
    x = (x_ref[...] + jnp.dot(ys_ref[...], wos_ref[...], preferred_element_type=F32)
         + jnp.dot(ya_ref[...], woa_ref[...], preferred_element_type=F32))
    h = _rms_norm(x, nw_ref[...]).astype(BF16)
    o_ref[...] = x + 0.5 * _swiglu(h, wg_ref, wu_ref, wd_ref)


def _out(x2d, y_ssd, y_att, wo_ssd, wo_att, norm_w, wg, wu, wd):
    t, d = x2d.shape
    tm = min(FFN_ROWS, t)
    row = lambda n: pl.BlockSpec((tm, n), lambda i: (i, 0))
    return pl.pallas_call(
        _out_kernel,
        grid=(t // tm,),
        in_specs=[row(d), row(y_ssd.shape[1]), row(y_att.shape[1]), _resident(wo_ssd.shape), _resident(wo_att.shape),
                  _resident((1, d)), _resident(wg.shape), _resident(wu.shape), _resident(wd.shape)],
        out_specs=row(d),
        out_shape=jax.ShapeDtypeStruct((t, d), F32),
        compiler_params=pltpu.CompilerParams(dimension_semantics=("arbitrary",), vmem_limit_bytes=VMEM_LIMIT),
        name="out_ffn",
    )(x2d, y_ssd, y_att, wo_ssd, wo_att, norm_w, wg, wu, wd)


def _pad_lanes(v):
    return jnp.zeros((1, LANES), F32).at[0, :v.shape[0]].set(v.astype(F32))


def kernel(x, ffn1_norm_w, ffn1_w_gate, ffn1_w_up, ffn1_w_down, mix_norm_w, w_in, conv_w, conv_b, dt_bias, a_log,
           d_skip, ssd_norm_w, q_norm_w, k_norm_w, lambda_q1, lambda_k1, lambda_q2, lambda_k2, attn_subln_w, w_out,
           ffn2_norm_w, ffn2_w_gate, ffn2_w_up, ffn2_w_down):
    bsz, s, d = x.shape
    depth = w_in.shape[0]
    ssd_heads = dt_bias.shape[1]
    ssd_width = ssd_heads * SSD_HEAD_DIM
    conv_ch = conv_w.shape[2]
    att_width = w_out.shape[1] - ssd_width
    qk_width = (att_width // ATT_V_DIM) * 2 * ATT_QK_DIM
    sizes = (ssd_width, conv_ch, ssd_heads, qk_width, qk_width, att_width)
    offs = [0]
    for n in sizes:
        offs.append(offs[-1] + n)
    row = lambda v: v.astype(F32).reshape(1, -1)

    x2d = x.reshape(bsz * s, d)
    for l in range(depth):
        lambda_init = 0.8 - 0.6 * math.exp(-0.3 * l)
        w = w_in[l].astype(BF16)
        wz, wxbc, wdt, wq, wk, wv = (w[:, offs[i]:offs[i + 1]] for i in range(6))
        wdt = jnp.pad(wdt, ((0, 0), (0, LANES - ssd_heads)))
        wo = w_out[l].astype(BF16)

        x1 = _ffn(x2d, row(ffn1_norm_w[l]), ffn1_w_gate[l].astype(BF16), ffn1_w_up[l].astype(BF16),
                  ffn1_w_down[l].astype(BF16))
        z, xbc, dt, q, k, v = _proj(x1, row(mix_norm_w[l]), wz, wxbc, wdt, wq, wk, wv,
                                    row(jnp.tile(q_norm_w[l], 2)), row(jnp.tile(k_norm_w[l], 2)))
        y_ssd = _ssd(z.reshape(bsz, s, -1), xbc.reshape(bsz, s, -1), dt.reshape(bsz, s, -1),
                     conv_w[l].astype(F32), row(conv_b[l]), _pad_lanes(dt_bias[l]), _pad_lanes(a_log[l]),
                     row(jnp.repeat(d_skip[l], SSD_HEAD_DIM)), row(ssd_norm_w[l]))
        y_att = _attn(q.reshape(bsz, s, -1), k.reshape(bsz, s, -1), v.reshape(bsz, s, -1),
                      row(lambda_q1[l]), row(lambda_k1[l]), row(lambda_q2[l]), row(lambda_k2[l]),
                      row(attn_subln_w[l]), lambda_init)
        x2d = _out(x1, y_ssd.reshape(bsz * s, -1), y_att.reshape(bsz * s, -1), wo[:ssd_width], wo[ssd_width:],
                   row(ffn2_norm_w[l]), ffn2_w_gate[l].astype(BF16), ffn2_w_up[l].astype(BF16),
                   ffn2_w_down[l].astype(BF16))
    return x2d.reshape(bsz, s, d)
```

```python
import functools
import math

import jax
import jax.numpy as jnp
from jax import lax
from jax.experimental import pallas as pl
from jax.experimental.pallas import tpu as pltpu

F32 = jnp.float32
BF16 = jnp.bfloat16

NORM_EPS = 1e-6
SSD_HEAD_DIM = 64
SSD_GROUPS = 2
SSD_STATE = 128
SSD_CONV = 4
SSD_CHUNK = 128
ATT_QK_DIM = 64
ATT_V_DIM = 128
LANES = 128
CONV_HALO = 128
LOG2_E = math.log2(math.e)

FFN_ROWS = 512
FFN_COLS = 256
ATT_BLOCK_Q = 1024
ATT_SUM_ROWS = 16
VMEM_LIMIT = 56 * 1024 * 1024


def _resident(shape):
    nd = len(shape)
    return pl.BlockSpec(shape, lambda *_: (0,) * nd, pipeline_mode=pl.Buffered(1))


def _rms_norm(x, w):
    return x * lax.rsqrt(jnp.mean(x * x, axis=-1, keepdims=True) + NORM_EPS) * w


def _silu(x):
    return x * jax.nn.sigmoid(x)


def _swiglu(h, wg_ref, wu_ref, wd_ref):
    d_ff = wg_ref.shape[1]
    acc = None
    for c in range(d_ff // FFN_COLS):
        sl = slice(c * FFN_COLS, (c + 1) * FFN_COLS)
        g = jnp.dot(h, wg_ref[:, sl], preferred_element_type=F32)
        u = jnp.dot(h, wu_ref[:, sl], preferred_element_type=F32)
        a = (_silu(g) * u).astype(BF16)
        d = jnp.dot(a, wd_ref[sl, :], preferred_element_type=F32)
        acc = d if acc is None else acc + d
    return acc


def _ffn_kernel(x_ref, nw_ref, wg_ref, wu_ref, wd_ref, o_ref):
    x = x_ref[...]
    h = _rms_norm(x, nw_ref[...]).astype(BF16)
    o_ref[...] = x + 0.5 * _swiglu(h, wg_ref, wu_ref, wd_ref)


def _ffn(x2d, norm_w, wg, wu, wd):
    t, d = x2d.shape
    tm = min(FFN_ROWS, t)
    row = pl.BlockSpec((tm, d), lambda i: (i, 0))
    return pl.pallas_call(
        _ffn_kernel,
        grid=(t // tm,),
        in_specs=[row, _resident((1, d)), _resident(wg.shape), _resident(wu.shape), _resident(wd.shape)],
        out_specs=row,
        out_shape=jax.ShapeDtypeStruct((t, d), F32),
        compiler_params=pltpu.CompilerParams(dimension_semantics=("arbitrary",), vmem_limit_bytes=VMEM_LIMIT),
        name="ffn",
    )(x2d, norm_w, wg, wu, wd)


def _qk_norm(t, w2, scale):
    lo = lax.broadcasted_iota(jnp.int32, t.shape, 1) < ATT_QK_DIM
    x2 = t * t
    s_lo = jnp.sum(jnp.where(lo, x2, 0.0), axis=-1, keepdims=True)
    s_hi = jnp.sum(jnp.where(lo, 0.0, x2), axis=-1, keepdims=True)
    r = lax.rsqrt(jnp.where(lo, s_lo, s_hi) * (1.0 / ATT_QK_DIM) + NORM_EPS)
    return t * r * (w2 * scale)


def _proj_kernel(x_ref, nw_ref, wz_ref, wxbc_ref, wdt_ref, wq_ref, wk_ref, wv_ref, qw_ref, kw_ref,
                 z_ref, xbc_ref, dt_ref, q_ref, k_ref, v_ref):
    h = _rms_norm(x_ref[...], nw_ref[...]).astype(BF16)
    z_ref[...] = jnp.dot(h, wz_ref[...], preferred_element_type=F32).astype(BF16)
    xbc_ref[...] = jnp.dot(h, wxbc_ref[...], preferred_element_type=F32).astype(BF16)
    dt_ref[...] = jnp.dot(h, wdt_ref[...], preferred_element_type=F32)
    v_ref[...] = jnp.dot(h, wv_ref[...], preferred_element_type=F32).astype(BF16)
    q = jnp.dot(h, wq_ref[...], preferred_element_type=F32)
    k = jnp.dot(h, wk_ref[...], preferred_element_type=F32)
    n_heads = q.shape[1] // LANES
    for hd in range(n_heads):
        sl = slice(hd * LANES, (hd + 1) * LANES)
        q_ref[:, sl] = _qk_norm(q[:, sl], qw_ref[...], ATT_QK_DIM ** -0.5 * math.log2(math.e)).astype(BF16)
        k_ref[:, sl] = _qk_norm(k[:, sl], kw_ref[...], 1.0).astype(BF16)


def _proj(x2d, norm_w, wz, wxbc, wdt, wq, wk, wv, qw2, kw2):
    t, d = x2d.shape
    tm = min(FFN_ROWS, t)
    row = lambda n: pl.BlockSpec((tm, n), lambda i: (i, 0))
    widths = (wz.shape[1], wxbc.shape[1], wdt.shape[1], wq.shape[1], wk.shape[1], wv.shape[1])
    dtypes = (BF16, BF16, F32, BF16, BF16, BF16)
    return pl.pallas_call(
        _proj_kernel,
        grid=(t // tm,),
        in_specs=[row(d), _resident((1, d))] + [_resident(w.shape) for w in (wz, wxbc, wdt, wq, wk, wv)]
                 + [_resident((1, LANES)), _resident((1, LANES))],
        out_specs=[row(n) for n in widths],
        out_shape=[jax.ShapeDtypeStruct((t, n), dt) for n, dt in zip(widths, dtypes)],
        compiler_params=pltpu.CompilerParams(dimension_semantics=("arbitrary",), vmem_limit_bytes=VMEM_LIMIT),
        name="proj",
    )(x2d, norm_w, wz, wxbc, wdt, wq, wk, wv, qw2, kw2)


def _lane_bcast(x, col):
    return jnp.broadcast_to(x[:, col:col + 1], x.shape)


def _expand_heads(x, pair):
    lo = lax.broadcasted_iota(jnp.int32, x.shape, 1) < SSD_HEAD_DIM
    return jnp.where(lo, _lane_bcast(x, 2 * pair), _lane_bcast(x, 2 * pair + 1))


def _ssd_kernel(z_ref, xbc_ref, dt_ref, cw_ref, cb_ref, dtb_ref, alog_ref, dskip_ref, nw_ref,
                y_ref, halo_ref, state_ref):
    L = SSD_CHUNK
    width = z_ref.shape[-1]
    n_pairs = width // LANES
    heads_per_group = (width // SSD_HEAD_DIM) // SSD_GROUPS
    gw = width // SSD_GROUPS

    @pl.when(pl.program_id(1) == 0)
    def _():
        halo_ref[0:8, :] = jnp.zeros((8, halo_ref.shape[1]), F32)
        state_ref[...] = jnp.zeros(state_ref.shape, F32)

    halo_ref[8:8 + L, :] = xbc_ref[...].astype(F32)
    conv = cb_ref[...]
    for kk in range(SSD_CONV):
        off = 8 - (SSD_CONV - 1) + kk
        conv = conv + cw_ref[kk:kk + 1, :] * halo_ref[off:off + L, :]
    halo_ref[0:8, :] = halo_ref[L:L + 8, :]
    act = _silu(conv)
    xs = act[:, :width]
    bm = act[:, width:width + SSD_GROUPS * SSD_STATE].astype(BF16)
    cm = act[:, width + SSD_GROUPS * SSD_STATE:]

    raw = dt_ref[...] + dtb_ref[...]
    dt = jnp.maximum(raw, 0.0) + jnp.log1p(jnp.exp(-jnp.abs(raw)))
    dta = dt * (-jnp.exp(alog_ref[...]))
    ri = lax.broadcasted_iota(jnp.int32, (L, L), 0)
    ci = lax.broadcasted_iota(jnp.int32, (L, L), 1)
    causal = ri >= ci
    a_cs = LOG2_E * jnp.dot(causal.astype(F32), dta, preferred_element_type=F32, precision=lax.Precision.HIGHEST)
    a_last = a_cs[L - 1:L, :]
    ea = jnp.exp2(a_cs)
    f = dt * jnp.exp2(a_last - a_cs)
    a_cs_t = a_cs.T
    a_src_t = a_cs_t - jnp.log2(dt).T

    cm_bf = cm.astype(BF16)
    cb = [lax.dot_general(cm_bf[:, g * SSD_STATE:(g + 1) * SSD_STATE], bm[:, g * SSD_STATE:(g + 1) * SSD_STATE],
                          (((1,), (1,)), ((), ())), preferred_element_type=F32) for g in range(SSD_GROUPS)]
    lo = lax.broadcasted_iota(jnp.int32, (L, LANES), 1) < SSD_HEAD_DIM
    zero = jnp.zeros((L, LANES), BF16)
    xs_bf = xs.astype(BF16)
    y_tiles = []
    for pair in range(n_pairs):
        g = (2 * pair) // heads_per_group
        cm_g = cm[:, g * SSD_STATE:(g + 1) * SSD_STATE]
        sl = slice(pair * LANES, (pair + 1) * LANES)
        x_pair = xs_bf[:, sl]
        prev_pair = state_ref[:, sl].astype(BF16)
        y_pair = None
        for half in range(2):
            hd = 2 * pair + half
            a_col = _lane_bcast(a_cs, hd)
            seg = a_col - a_src_t[hd:hd + 1, :]
            m_h = (cb[g] * jnp.exp2(jnp.where(causal, seg, -jnp.inf))).astype(BF16)
            c_h = (cm_g * jnp.exp2(a_col)).astype(BF16)
            keep = lo if half == 0 else jnp.logical_not(lo)
            rhs = jnp.concatenate([jnp.where(keep, x_pair, zero), jnp.where(keep, prev_pair, zero)], axis=0)
            yh = jnp.dot(jnp.concatenate([m_h, c_h], axis=1), rhs, preferred_element_type=F32)
            y_pair = yh if y_pair is None else y_pair + yh
        y_tiles.append(y_pair)

    head_of_lane = lax.broadcasted_iota(jnp.int32, (LANES, width), 1) // SSD_HEAD_DIM
    expand = (lax.broadcasted_iota(jnp.int32, (LANES, width), 0) == head_of_lane).astype(BF16)
    f_hi = f.astype(BF16)
    f_lo = (f - f_hi.astype(F32)).astype(BF16)
    f_exp = (jnp.dot(f_hi, expand, preferred_element_type=F32) + jnp.dot(f_lo, expand, preferred_element_type=F32))
    for pair in range(n_pairs):
        g = (2 * pair) // heads_per_group
        sl = slice(pair * LANES, (pair + 1) * LANES)
        bm_g = bm[:, g * SSD_STATE:(g + 1) * SSD_STATE]
        xw = (xs[:, sl] * f_exp[:, sl]).astype(BF16)
        new = lax.dot_general(bm_g, xw, (((0,), (0,)), ((), ())), preferred_element_type=F32)
        chunk_decay = _expand_heads(ea[L - 8:L, :], pair)[7:8, :]
        state_ref[:, sl] = state_ref[:, sl] * chunk_decay + new

    y = jnp.concatenate(y_tiles, axis=1) + xs * dskip_ref[...]
    gated = y * _silu(z_ref[...].astype(F32))
    for g in range(SSD_GROUPS):
        sl = slice(g * gw, (g + 1) * gw)
        y_ref[:, sl] = _rms_norm(gated[:, sl], nw_ref[:, sl]).astype(BF16)


def _ssd(z, xbc, dt, conv_w, conv_b, dt_bias, a_log, d_skip_exp, norm_w):
    bsz, s, width = z.shape
    L = SSD_CHUNK
    cw = xbc.shape[-1]
    blk = lambda n: pl.BlockSpec((None, L, n), lambda b, c: (b, c, 0))
    return pl.pallas_call(
        _ssd_kernel,
        grid=(bsz, s // L),
        in_specs=[blk(width), blk(cw), blk(LANES), _resident(conv_w.shape), _resident((1, cw)),
                  _resident((1, LANES)), _resident((1, LANES)), _resident((1, width)), _resident((1, width))],
        out_specs=blk(width),
        out_shape=jax.ShapeDtypeStruct((bsz, s, width), BF16),
        scratch_shapes=[pltpu.VMEM((8 + L, cw), F32), pltpu.VMEM((SSD_STATE, width), F32)],
        compiler_params=pltpu.CompilerParams(dimension_semantics=("arbitrary", "arbitrary"),
                                             vmem_limit_bytes=VMEM_LIMIT),
        name="ssd",
    )(z, xbc, dt, conv_w, conv_b, dt_bias, a_log, d_skip_exp, norm_w)


def _attn_kernel(q_ref, k_ref, v_ref, lq1_ref, lk1_ref, lq2_ref, lk2_ref, sw_ref, o_ref, s_ref, acc_ref, *,
                 lambda_init):
    tq = o_ref.shape[0]
    tk = s_ref.shape[2]
    n_q = q_ref.shape[0] // tq
    qi = pl.program_id(2)

    def q_maps_t(blk):
        q = q_ref[pl.ds(pl.multiple_of(blk * tq, tq), tq), :]
        lo = lax.broadcasted_iota(jnp.int32, q.shape, 1) < ATT_QK_DIM
        zero = jnp.zeros_like(q)
        return jnp.where(lo, q, zero).T, jnp.where(lo, zero, q).T

    def qk(row0, slot, q_t, col0=0):
        k = k_ref[pl.ds(pl.multiple_of(row0, tk), tk), :]
        for mp in range(2):
            s_ref[slot, mp, :, col0:] = jnp.dot(k, q_t[mp][:, col0:], preferred_element_type=F32)

    ones_rows = jnp.ones((ATT_SUM_ROWS, tk), BF16)

    def softmax_pv(row0, slot, m, col0=0, mask=None):
        v_ext = jnp.concatenate([v_ref[pl.ds(pl.multiple_of(row0, tk), tk), :].T, ones_rows], axis=0)
        m_out = []
        for mp in range(2):
            s = s_ref[slot, mp, :, col0:]
            if mask is not None:
                s = jnp.where(mask, s, -jnp.inf)
            m_old = m[mp][:, col0:]
            m_new = jnp.maximum(m_old, jnp.max(s, axis=0, keepdims=True))
            p = jnp.exp2(s - m_new).astype(BF16)
            acc_ref[mp, :, col0:] = (jnp.exp2(m_old - m_new) * acc_ref[mp, :, col0:]
                                     + jnp.dot(v_ext, p, preferred_element_type=F32))
            m_out.append(m_new if col0 == 0 else jnp.concatenate([m[mp][:, :col0], m_new], axis=1))
        return tuple(m_out)

    q_t = q_maps_t(qi)

    @pl.when(qi == 0)
    def _():
        qk(0, 0, q_t)

    acc_ref[...] = jnp.zeros(acc_ref.shape, F32)
    neg_inf = jnp.full((1, tq), -jnp.inf, F32)

    def pair(i, m):
        row0 = i * tq
        qk(row0 + tk, 1, q_t)
        m = softmax_pv(row0, 0, m)
        qk(row0 + tq, 0, q_t)
        return softmax_pv(row0 + tk, 1, m)

    m = lax.fori_loop(0, qi, pair, (neg_inf, neg_inf))

    row0 = qi * tq
    causal = (lax.broadcasted_iota(jnp.int32, (tk, tq), 0) <= lax.broadcasted_iota(jnp.int32, (tk, tq), 1))
    qk(row0 + tk, 1, q_t, col0=tk)
    m = softmax_pv(row0, 0, m, mask=causal)
    qk(0, 0, q_maps_t(jnp.minimum(qi + 1, n_q - 1)))
    softmax_pv(row0 + tk, 1, m, col0=tk, mask=causal[:, :tk])

    lam = (jnp.exp(jnp.sum(lq1_ref[...] * lk1_ref[...], axis=-1, keepdims=True))
           - jnp.exp(jnp.sum(lq2_ref[...] * lk2_ref[...], axis=-1, keepdims=True)) + lambda_init)
    o_t = (acc_ref[0, :ATT_V_DIM] / acc_ref[0, ATT_V_DIM:ATT_V_DIM + 1]
           - lam * (acc_ref[1, :ATT_V_DIM] / acc_ref[1, ATT_V_DIM:ATT_V_DIM + 1]))
    o_ref[...] = (_rms_norm(o_t.T, sw_ref[...]) * (1.0 - lambda_init)).astype(BF16)


def _attn(q, k, v, lq1, lk1, lq2, lk2, subln_w, lambda_init):
    bsz, s, width = q.shape
    n_heads = width // ATT_V_DIM
    tq = min(ATT_BLOCK_Q, s)
    o_spec = pl.BlockSpec((None, tq, LANES), lambda b, h, i: (b, i, h))
    head_spec = pl.BlockSpec((None, s, LANES), lambda b, h, i: (b, 0, h))
    small = _resident((1, ATT_QK_DIM))
    return pl.pallas_call(
        functools.partial(_attn_kernel, lambda_init=lambda_init),
        grid=(bsz, n_heads, s // tq),
        in_specs=[head_spec, head_spec, head_spec, small, small, small, small, _resident((1, ATT_V_DIM))],
        out_specs=o_spec,
        out_shape=jax.ShapeDtypeStruct((bsz, s, width), BF16),
        scratch_shapes=[pltpu.VMEM((2, 2, tq // 2, tq), F32),
                        pltpu.VMEM((2, ATT_V_DIM + ATT_SUM_ROWS, tq), F32)],
        compiler_params=pltpu.CompilerParams(dimension_semantics=("arbitrary",) * 3, vmem_limit_bytes=VMEM_LIMIT),
        name="attn",
    )(q, k, v, lq1, lk1, lq2, lk2, subln_w)


def _out_kernel(x_ref, ys_ref, ya_ref, wos_ref, woa_ref, nw_ref, wg_ref, wu_ref, wd_ref, o_ref):
    x = (x_ref[...] + jnp.dot(ys_ref[...], wos_ref[...], preferred_element_type=F32)
         + jnp.dot(ya_ref[...], woa_ref[...], preferred_element_type=F32))
    h = _rms_norm(x, nw_ref[...]).astype(BF16)
    o_ref[...] = x + 0.5 * _swiglu(h, wg_ref, wu_ref, wd_ref)


def _out(x2d, y_ssd, y_att, wo_ssd, wo_att, norm_w, wg, wu, wd):
    t, d = x2d.shape
    tm = min(FFN_ROWS, t)
    row = lambda n: pl.BlockSpec((tm, n), lambda i: (i, 0))
    return pl.pallas_call(
        _out_kernel,
        grid=(t // tm,),
        in_specs=[row(d), row(y_ssd.shape[1]), row(y_att.shape[1]), _resident(wo_ssd.shape), _resident(wo_att.shape),
                  _resident((1, d)), _resident(wg.shape), _resident(wu.shape), _resident(wd.shape)],
        out_specs=row(d),
        out_shape=jax.ShapeDtypeStruct((t, d), F32),
        compiler_params=pltpu.CompilerParams(dimension_semantics=("arbitrary",), vmem_limit_bytes=VMEM_LIMIT),
        name="out_ffn",
    )(x2d, y_ssd, y_att, wo_ssd, wo_att, norm_w, wg, wu, wd)


def _pad_lanes(v):
    return jnp.zeros((1, LANES), F32).at[0, :v.shape[0]].set(v.astype(F32))


def kernel(x, ffn1_norm_w, ffn1_w_gate, ffn1_w_up, ffn1_w_down, mix_norm_w, w_in, conv_w, conv_b, dt_bias, a_log,
           d_skip, ssd_norm_w, q_norm_w, k_norm_w, lambda_q1, lambda_k1, lambda_q2, lambda_k2, attn_subln_w, w_out,
           ffn2_norm_w, ffn2_w_gate, ffn2_w_up, ffn2_w_down):
    bsz, s, d = x.shape
    depth = w_in.shape[0]
    ssd_heads = dt_bias.shape[1]
    ssd_width = ssd_heads * SSD_HEAD_DIM
    conv_ch = conv_w.shape[2]
    att_width = w_out.shape[1] - ssd_width
    qk_width = (att_width // ATT_V_DIM) * 2 * ATT_QK_DIM
    sizes = (ssd_width, conv_ch, ssd_heads, qk_width, qk_width, att_width)
    offs = [0]
    for n in sizes:
        offs.append(offs[-1] + n)
    row = lambda v: v.astype(F32).reshape(1, -1)

    x2d = x.reshape(bsz * s, d)
    for l in range(depth):
        lambda_init = 0.8 - 0.6 * math.exp(-0.3 * l)
        w = w_in[l].astype(BF16)
        wz, wxbc, wdt, wq, wk, wv = (w[:, offs[i]:offs[i + 1]] for i in range(6))
        wdt = jnp.pad(wdt, ((0, 0), (0, LANES - ssd_heads)))
        wo = w_out[l].astype(BF16)

        x1 = _ffn(x2d, row(ffn1_norm_w[l]), ffn1_w_gate[l].astype(BF16), ffn1_w_up[l].astype(BF16),
                  ffn1_w_down[l].astype(BF16))
        z, xbc, dt, q, k, v = _proj(x1, row(mix_norm_w[l]), wz, wxbc, wdt, wq, wk, wv,
                                    row(jnp.tile(q_norm_w[l], 2)), row(jnp.tile(k_norm_w[l], 2)))
        y_ssd = _ssd(z.reshape(bsz, s, -1), xbc.reshape(bsz, s, -1), dt.reshape(bsz, s, -1),
                     conv_w[l].astype(F32), row(conv_b[l]), _pad_lanes(dt_bias[l]), _pad_lanes(a_log[l]),
                     row(jnp.repeat(d_skip[l], SSD_HEAD_DIM)), row(ssd_norm_w[l]))
        y_att = _attn(q.reshape(bsz, s, -1), k.reshape(bsz, s, -1), v.reshape(bsz, s, -1),
                      row(lambda_q1[l]), row(lambda_k1[l]), row(lambda_q2[l]), row(lambda_k2[l]),
                      row(attn_subln_w[l]), lambda_init)
        x2d = _out(x1, y_ssd.reshape(bsz * s, -1), y_att.reshape(bsz * s, -1), wo[:ssd_width], wo[ssd_width:],
                   row(ffn2_norm_w[l]), ffn2_w_gate[l].astype(BF16), ffn2_w_up[l].astype(BF16),
                   ffn2_w_down[l].astype(BF16))
    return x2d.reshape(bsz, s, d)
```

```python
import functools
import math

import jax
import jax.numpy as jnp
from jax import lax
from jax.experimental import pallas as pl
from jax.experimental.pallas import tpu as pltpu

F32 = jnp.float32
BF16 = jnp.bfloat16

NORM_EPS = 1e-6
SSD_HEAD_DIM = 64
SSD_GROUPS = 2
SSD_STATE = 128
SSD_CONV = 4
SSD_CHUNK = 128
ATT_QK_DIM = 64
ATT_V_DIM = 128
LANES = 128
CONV_HALO = 16
LOG2_E = math.log2(math.e)

FFN_ROWS = 512
FFN_COLS = 256
ATT_BLOCK_Q = 1024
ATT_SUM_ROWS = 16
VMEM_LIMIT = 56 * 1024 * 1024


def _resident(shape):
    nd = len(shape)
    return pl.BlockSpec(shape, lambda *_: (0,) * nd, pipeline_mode=pl.Buffered(1))


def _rms_norm(x, w):
    return x * lax.rsqrt(jnp.mean(x * x, axis=-1, keepdims=True) + NORM_EPS) * w


def _silu(x):
    return x * jax.nn.sigmoid(x)


def _swiglu(h, wg_ref, wu_ref, wd_ref):
    d_ff = wg_ref.shape[1]
    acc = None
    for c in range(d_ff // FFN_COLS):
        sl = slice(c * FFN_COLS, (c + 1) * FFN_COLS)
        g = jnp.dot(h, wg_ref[:, sl], preferred_element_type=F32)
        u = jnp.dot(h, wu_ref[:, sl], preferred_element_type=F32)
        a = (_silu(g) * u).astype(BF16)
        d = jnp.dot(a, wd_ref[sl, :], preferred_element_type=F32)
        acc = d if acc is None else acc + d
    return acc


def _ffn_kernel(x_ref, nw_ref, wg_ref, wu_ref, wd_ref, o_ref):
    x = x_ref[...]
    h = _rms_norm(x, nw_ref[...]).astype(BF16)
    o_ref[...] = x + 0.5 * _swiglu(h, wg_ref, wu_ref, wd_ref)


def _ffn(x2d, norm_w, wg, wu, wd):
    t, d = x2d.shape
    tm = min(FFN_ROWS, t)
    row = pl.BlockSpec((tm, d), lambda i: (i, 0))
    return pl.pallas_call(
        _ffn_kernel,
        grid=(t // tm,),
        in_specs=[row, _resident((1, d)), _resident(wg.shape), _resident(wu.shape), _resident(wd.shape)],
        out_specs=row,
        out_shape=jax.ShapeDtypeStruct((t, d), F32),
        compiler_params=pltpu.CompilerParams(dimension_semantics=("arbitrary",), vmem_limit_bytes=VMEM_LIMIT),
        name="ffn",
    )(x2d, norm_w, wg, wu, wd)


def _qk_norm(t, w2, scale):
    lo = lax.broadcasted_iota(jnp.int32, t.shape, 1) < ATT_QK_DIM
    x2 = t * t
    s_lo = jnp.sum(jnp.where(lo, x2, 0.0), axis=-1, keepdims=True)
    s_hi = jnp.sum(jnp.where(lo, 0.0, x2), axis=-1, keepdims=True)
    r = lax.rsqrt(jnp.where(lo, s_lo, s_hi) * (1.0 / ATT_QK_DIM) + NORM_EPS)
    return t * r * (w2 * scale)


def _proj_kernel(x_ref, nw_ref, wz_ref, wxbc_ref, wdt_ref, wq_ref, wk_ref, wv_ref, qw_ref, kw_ref,
                 z_ref, xbc_ref, dt_ref, q_ref, k_ref, v_ref):
    h = _rms_norm(x_ref[...], nw_ref[...]).astype(BF16)
    z_ref[...] = jnp.dot(h, wz_ref[...], preferred_element_type=F32).astype(BF16)
    xbc_ref[...] = jnp.dot(h, wxbc_ref[...], preferred_element_type=F32).astype(BF16)
    dt_ref[...] = jnp.dot(h, wdt_ref[...], preferred_element_type=F32)
    v_ref[...] = jnp.dot(h, wv_ref[...], preferred_element_type=F32).astype(BF16)
    q = jnp.dot(h, wq_ref[...], preferred_element_type=F32)
    k = jnp.dot(h, wk_ref[...], preferred_element_type=F32)
    n_heads = q.shape[1] // LANES
    for hd in range(n_heads):
        sl = slice(hd * LANES, (hd + 1) * LANES)
        q_ref[:, sl] = _qk_norm(q[:, sl], qw_ref[...], ATT_QK_DIM ** -0.5 * math.log2(math.e)).astype(BF16)
        k_ref[:, sl] = _qk_norm(k[:, sl], kw_ref[...], 1.0).astype(BF16)


def _proj(x2d, norm_w, wz, wxbc, wdt, wq, wk, wv, qw2, kw2):
    t, d = x2d.shape
    tm = min(FFN_ROWS, t)
    row = lambda n: pl.BlockSpec((tm, n), lambda i: (i, 0))
    widths = (wz.shape[1], wxbc.shape[1], wdt.shape[1], wq.shape[1], wk.shape[1], wv.shape[1])
    dtypes = (BF16, BF16, F32, BF16, BF16, BF16)
    return pl.pallas_call(
        _proj_kernel,
        grid=(t // tm,),
        in_specs=[row(d), _resident((1, d))] + [_resident(w.shape) for w in (wz, wxbc, wdt, wq, wk, wv)]
                 + [_resident((1, LANES)), _resident((1, LANES))],
        out_specs=[row(n) for n in widths],
        out_shape=[jax.ShapeDtypeStruct((t, n), dt) for n, dt in zip(widths, dtypes)],
        compiler_params=pltpu.CompilerParams(dimension_semantics=("arbitrary",), vmem_limit_bytes=VMEM_LIMIT),
        name="proj",
    )(x2d, norm_w, wz, wxbc, wdt, wq, wk, wv, qw2, kw2)


def _lane_bcast(x, col):
    return jnp.broadcast_to(x[:, col:col + 1], x.shape)


def _expand_heads(x, pair):
    lo = lax.broadcasted_iota(jnp.int32, x.shape, 1) < SSD_HEAD_DIM
    return jnp.where(lo, _lane_bcast(x, 2 * pair), _lane_bcast(x, 2 * pair + 1))


def _ssd_kernel(z_ref, xbc_ref, halo_ref, dt_ref, cw_ref, cb_ref, dtb_ref, alog_ref, dskip_ref, nw_ref,
                y_ref, state_ref):
    L = SSD_CHUNK
    width = z_ref.shape[-1]
    n_pairs = width // LANES
    heads_per_group = (width // SSD_HEAD_DIM) // SSD_GROUPS
    gw = width // SSD_GROUPS

    first = pl.program_id(1) == 0

    @pl.when(first)
    def _():
        state_ref[...] = jnp.zeros(state_ref.shape, F32)

    x_cur = xbc_ref[...]
    halo = halo_ref[...]
    x_ext = jnp.concatenate([jnp.where(first, jnp.zeros_like(halo), halo), x_cur], axis=0)
    out_t = lax.broadcasted_iota(jnp.int32, (L, CONV_HALO + L), 0)
    in_t = lax.broadcasted_iota(jnp.int32, (L, CONV_HALO + L), 1) - CONV_HALO
    conv = cb_ref[...] + cw_ref[SSD_CONV - 1:SSD_CONV, :] * x_cur.astype(F32)
    for kk in range(SSD_CONV - 1):
        shift = jnp.where(in_t == out_t - (SSD_CONV - 1 - kk), 1.0, 0.0).astype(BF16)
        conv = conv + cw_ref[kk:kk + 1, :] * jnp.dot(shift, x_ext, preferred_element_type=F32)
    act = _silu(conv)
    xs = act[:, :width]
    bm = act[:, width:width + SSD_GROUPS * SSD_STATE].astype(BF16)
    cm = act[:, width + SSD_GROUPS * SSD_STATE:]

    raw = dt_ref[...] + dtb_ref[...]
    dt = jnp.maximum(raw, 0.0) + jnp.log1p(jnp.exp(-jnp.abs(raw)))
    dta = dt * (-jnp.exp(alog_ref[...]))
    ri = lax.broadcasted_iota(jnp.int32, (L, L), 0)
    ci = lax.broadcasted_iota(jnp.int32, (L, L), 1)
    causal = ri >= ci
    a_cs = LOG2_E * jnp.dot(causal.astype(F32), dta, preferred_element_type=F32, precision=lax.Precision.HIGHEST)
    a_last = a_cs[L - 1:L, :]
    ea = jnp.exp2(a_cs)
    f = dt * jnp.exp2(a_last - a_cs)
    a_cs_t = a_cs.T
    a_src_t = a_cs_t - jnp.log2(dt).T

    cm_bf = cm.astype(BF16)
    cb = [lax.dot_general(cm_bf[:, g * SSD_STATE:(g + 1) * SSD_STATE], bm[:, g * SSD_STATE:(g + 1) * SSD_STATE],
                          (((1,), (1,)), ((), ())), preferred_element_type=F32) for g in range(SSD_GROUPS)]
    lo = lax.broadcasted_iota(jnp.int32, (L, LANES), 1) < SSD_HEAD_DIM
    zero = jnp.zeros((L, LANES), BF16)
    xs_bf = xs.astype(BF16)
    y_tiles = []
    for pair in range(n_pairs):
        g = (2 * pair) // heads_per_group
        cm_g = cm[:, g * SSD_STATE:(g + 1) * SSD_STATE]
        sl = slice(pair * LANES, (pair + 1) * LANES)
        x_pair = xs_bf[:, sl]
        prev_pair = state_ref[:, sl].astype(BF16)
        y_pair = None
        for half in range(2):
            hd = 2 * pair + half
            a_col = _lane_bcast(a_cs, hd)
            seg = a_col - a_src_t[hd:hd + 1, :]
            m_h = (cb[g] * jnp.exp2(jnp.where(causal, seg, -jnp.inf))).astype(BF16)
            c_h = (cm_g * jnp.exp2(a_col)).astype(BF16)
            keep = lo if half == 0 else jnp.logical_not(lo)
            rhs = jnp.concatenate([jnp.where(keep, x_pair, zero), jnp.where(keep, prev_pair, zero)], axis=0)
            yh = jnp.dot(jnp.concatenate([m_h, c_h], axis=1), rhs, preferred_element_type=F32)
            y_pair = yh if y_pair is None else y_pair + yh
        y_tiles.append(y_pair)

    head_of_lane = lax.broadcasted_iota(jnp.int32, (LANES, width), 1) // SSD_HEAD_DIM
    expand = (lax.broadcasted_iota(jnp.int32, (LANES, width), 0) == head_of_lane).astype(BF16)
    f_hi = f.astype(BF16)
    f_lo = (f - f_hi.astype(F32)).astype(BF16)
    f_exp = (jnp.dot(f_hi, expand, preferred_element_type=F32) + jnp.dot(f_lo, expand, preferred_element_type=F32))
    for pair in range(n_pairs):
        g = (2 * pair) // heads_per_group
        sl = slice(pair * LANES, (pair + 1) * LANES)
        bm_g = bm[:, g * SSD_STATE:(g + 1) * SSD_STATE]
        xw = (xs[:, sl] * f_exp[:, sl]).astype(BF16)
        new = lax.dot_general(bm_g, xw, (((0,), (0,)), ((), ())), preferred_element_type=F32)
        chunk_decay = _expand_heads(ea[L - 8:L, :], pair)[7:8, :]
        state_ref[:, sl] = state_ref[:, sl] * chunk_decay + new

    y = jnp.concatenate(y_tiles, axis=1) + xs * dskip_ref[...]
    gated = y * _silu(z_ref[...].astype(F32))
    for g in range(SSD_GROUPS):
        sl = slice(g * gw, (g + 1) * gw)
        y_ref[:, sl] = _rms_norm(gated[:, sl], nw_ref[:, sl]).astype(BF16)


def _ssd(z, xbc, dt, conv_w, conv_b, dt_bias, a_log, d_skip_exp, norm_w):
    bsz, s, width = z.shape
    L = SSD_CHUNK
    cw = xbc.shape[-1]
    blk = lambda n: pl.BlockSpec((None, L, n), lambda b, c: (b, c, 0))
    halo = pl.BlockSpec((None, CONV_HALO, cw), lambda b, c: (b, jnp.maximum(c * (L // CONV_HALO) - 1, 0), 0))
    return pl.pallas_call(
        _ssd_kernel,
        grid=(bsz, s // L),
        in_specs=[blk(width), blk(cw), halo, blk(LANES), _resident(conv_w.shape), _resident((1, cw)),
                  _resident((1, LANES)), _resident((1, LANES)), _resident((1, width)), _resident((1, width))],
        out_specs=blk(width),
        out_shape=jax.ShapeDtypeStruct((bsz, s, width), BF16),
        scratch_shapes=[pltpu.VMEM((SSD_STATE, width), F32)],
        compiler_params=pltpu.CompilerParams(dimension_semantics=("arbitrary", "arbitrary"),
                                             vmem_limit_bytes=VMEM_LIMIT),
        name="ssd",
    )(z, xbc, xbc, dt, conv_w, conv_b, dt_bias, a_log, d_skip_exp, norm_w)


def _attn_kernel(q_ref, k_ref, v_ref, lq1_ref, lk1_ref, lq2_ref, lk2_ref, sw_ref, o_ref, s_ref, acc_ref, *,
                 lambda_init):
    tq = o_ref.shape[0]
    tk = s_ref.shape[2]
    n_q = q_ref.shape[0] // tq
    qi = pl.program_id(2)

    def q_maps_t(blk):
        q = q_ref[pl.ds(pl.multiple_of(blk * tq, tq), tq), :]
        lo = lax.broadcasted_iota(jnp.int32, q.shape, 1) < ATT_QK_DIM
        zero = jnp.zeros_like(q)
        return jnp.where(lo, q, zero).T, jnp.where(lo, zero, q).T

    def qk(row0, slot, q_t, col0=0):
        k = k_ref[pl.ds(pl.multiple_of(row0, tk), tk), :]
        for mp in range(2):
            s_ref[slot, mp, :, col0:] = jnp.dot(k, q_t[mp][:, col0:], preferred_element_type=F32)

    ones_rows = jnp.ones((ATT_SUM_ROWS, tk), BF16)

    def softmax_pv(row0, slot, m, col0=0, mask=None):
        v_ext = jnp.concatenate([v_ref[pl.ds(pl.multiple_of(row0, tk), tk), :].T, ones_rows], axis=0)
        m_out = []
        for mp in range(2):
            s = s_ref[slot, mp, :, col0:]
            if mask is not None:
                s = jnp.where(mask, s, -jnp.inf)
            m_old = m[mp][:, col0:]
            m_new = jnp.maximum(m_old, jnp.max(s, axis=0, keepdims=True))
            p = jnp.exp2(s - m_new).astype(BF16)
            acc_ref[mp, :, col0:] = (jnp.exp2(m_old - m_new) * acc_ref[mp, :, col0:]
                                     + jnp.dot(v_ext, p, preferred_element_type=F32))
            m_out.append(m_new if col0 == 0 else jnp.concatenate([m[mp][:, :col0], m_new], axis=1))
        return tuple(m_out)

    q_t = q_maps_t(qi)

    @pl.when(qi == 0)
    def _():
        qk(0, 0, q_t)

    acc_ref[...] = jnp.zeros(acc_ref.shape, F32)
    neg_inf = jnp.full((1, tq), -jnp.inf, F32)

    def pair(i, m):
        row0 = i * tq
        qk(row0 + tk, 1, q_t)
        m = softmax_pv(row0, 0, m)
        qk(row0 + tq, 0, q_t)
        return softmax_pv(row0 + tk, 1, m)

    m = lax.fori_loop(0, qi, pair, (neg_inf, neg_inf))

    row0 = qi * tq
    causal = (lax.broadcasted_iota(jnp.int32, (tk, tq), 0) <= lax.broadcasted_iota(jnp.int32, (tk, tq), 1))
    qk(row0 + tk, 1, q_t, col0=tk)
    m = softmax_pv(row0, 0, m, mask=causal)
    qk(0, 0, q_maps_t(jnp.minimum(qi + 1, n_q - 1)))
    softmax_pv(row0 + tk, 1, m, col0=tk, mask=causal[:, :tk])

    lam = (jnp.exp(jnp.sum(lq1_ref[...] * lk1_ref[...], axis=-1, keepdims=True))
           - jnp.exp(jnp.sum(lq2_ref[...] * lk2_ref[...], axis=-1, keepdims=True)) + lambda_init)
    o_t = (acc_ref[0, :ATT_V_DIM] / acc_ref[0, ATT_V_DIM:ATT_V_DIM + 1]
           - lam * (acc_ref[1, :ATT_V_DIM] / acc_ref[1, ATT_V_DIM:ATT_V_DIM + 1]))
    o_ref[...] = (_rms_norm(o_t.T, sw_ref[...]) * (1.0 - lambda_init)).astype(BF16)


def _attn(q, k, v, lq1, lk1, lq2, lk2, subln_w, lambda_init):
    bsz, s, width = q.shape
    n_heads = width // ATT_V_DIM
    tq = min(ATT_BLOCK_Q, s)
    o_spec = pl.BlockSpec((None, tq, LANES), lambda b, h, i: (b, i, h))
    head_spec = pl.BlockSpec((None, s, LANES), lambda b, h, i: (b, 0, h))
    small = _resident((1, ATT_QK_DIM))
    return pl.pallas_call(
        functools.partial(_attn_kernel, lambda_init=lambda_init),
        grid=(bsz, n_heads, s // tq),
        in_specs=[head_spec, head_spec, head_spec, small, small, small, small, _resident((1, ATT_V_DIM))],
        out_specs=o_spec,
        out_shape=jax.ShapeDtypeStruct((bsz, s, width), BF16),
        scratch_shapes=[pltpu.VMEM((2, 2, tq // 2, tq), F32),
                        pltpu.VMEM((2, ATT_V_DIM + ATT_SUM_ROWS, tq), F32)],
        compiler_params=pltpu.CompilerParams(dimension_semantics=("arbitrary",) * 3, vmem_limit_bytes=VMEM_LIMIT),
        name="attn",
    )(q, k, v, lq1, lk1, lq2, lk2, subln_w)


def _out_kernel(x_ref, ys_ref, ya_ref, wos_ref, woa_ref, nw_ref, wg_ref, wu_ref, wd_ref, o_ref):
    x = (x_ref[...] + jnp.dot(ys_ref[...], wos_ref[...], preferred_element_type=F32)
         + jnp.dot(ya_ref[...], woa_ref[...], preferred_element_type=F32))
    h = _rms_norm(x, nw_ref[...]).astype(BF16)
    o_ref[...] = x + 0.5 * _swiglu(h, wg_ref, wu_ref, wd_ref)


def _out(x2d, y_ssd, y_att, wo_ssd, wo_att, norm_w, wg, wu, wd):
    t, d = x2d.shape
    tm = min(FFN_ROWS, t)
    row = lambda n: pl.BlockSpec((tm, n), lambda i: (i, 0))
    return pl.pallas_call(
        _out_kernel,
        grid=(t // tm,),
        in_specs=[row(d), row(y_ssd.shape[1]), row(y_att.shape[1]), _resident(wo_ssd.shape), _resident(wo_att.shape),
                  _resident((1, d)), _resident(wg.shape), _resident(wu.shape), _resident(wd.shape)],
        out_specs=row(d),
        out_shape=jax.ShapeDtypeStruct((t, d), F32),
        compiler_params=pltpu.CompilerParams(dimension_semantics=("arbitrary",), vmem_limit_bytes=VMEM_LIMIT),
        name="out_ffn",
    )(x2d, y_ssd, y_att, wo_ssd, wo_att, norm_w, wg, wu, wd)


def _pad_lanes(v):
    return jnp.zeros((1, LANES), F32).at[0, :v.shape[0]].set(v.astype(F32))


def kernel(x, ffn1_norm_w, ffn1_w_gate, ffn1_w_up, ffn1_w_down, mix_norm_w, w_in, conv_w, conv_b, dt_bias, a_log,
           d_skip, ssd_norm_w, q_norm_w, k_norm_w, lambda_q1, lambda_k1, lambda_q2, lambda_k2, attn_subln_w, w_out,
           ffn2_norm_w, ffn2_w_gate, ffn2_w_up, ffn2_w_down):
    bsz, s, d = x.shape
    depth = w_in.shape[0]
    ssd_heads = dt_bias.shape[1]
    ssd_width = ssd_heads * SSD_HEAD_DIM
    conv_ch = conv_w.shape[2]
    att_width = w_out.shape[1] - ssd_width
    qk_width = (att_width // ATT_V_DIM) * 2 * ATT_QK_DIM
    sizes = (ssd_width, conv_ch, ssd_heads, qk_width, qk_width, att_width)
    offs = [0]
    for n in sizes:
        offs.append(offs[-1] + n)
    row = lambda v: v.astype(F32).reshape(1, -1)

    x2d = x.reshape(bsz * s, d)
    for l in range(depth):
        lambda_init = 0.8 - 0.6 * math.exp(-0.3 * l)
        w = w_in[l].astype(BF16)
        wz, wxbc, wdt, wq, wk, wv = (w[:, offs[i]:offs[i + 1]] for i in range(6))
        wdt = jnp.pad(wdt, ((0, 0), (0, LANES - ssd_heads)))
        wo = w_out[l].astype(BF16)

        x1 = _ffn(x2d, row(ffn1_norm_w[l]), ffn1_w_gate[l].astype(BF16), ffn1_w_up[l].astype(BF16),
                  ffn1_w_down[l].astype(BF16))
        z, xbc, dt, q, k, v = _proj(x1, row(mix_norm_w[l]), wz, wxbc, wdt, wq, wk, wv,
                                    row(jnp.tile(q_norm_w[l], 2)), row(jnp.tile(k_norm_w[l], 2)))
        y_ssd = _ssd(z.reshape(bsz, s, -1), xbc.reshape(bsz, s, -1), dt.reshape(bsz, s, -1),
                     conv_w[l].astype(F32), row(conv_b[l]), _pad_lanes(dt_bias[l]), _pad_lanes(a_log[l]),
                     row(jnp.repeat(d_skip[l], SSD_HEAD_DIM)), row(ssd_norm_w[l]))
        y_att = _attn(q.reshape(bsz, s, -1), k.reshape(bsz, s, -1), v.reshape(bsz, s, -1),
                      row(lambda_q1[l]), row(lambda_k1[l]), row(lambda_q2[l]), row(lambda_k2[l]),
                      row(attn_subln_w[l]), lambda_init)
        x2d = _out(x1, y_ssd.reshape(bsz * s, -1), y_att.reshape(bsz * s, -1), wo[:ssd_width], wo[ssd_width:],
                   row(ffn2_norm_w[l]), ffn2_w_gate[l].astype(BF16), ffn2_w_up[l].astype(BF16),
                   ffn2_w_down[l].astype(BF16))
    return x2d.reshape(bsz, s, d)
```

```python
import functools
import math

import jax
import jax.numpy as jnp
from jax import lax
from jax.experimental import pallas as pl
from jax.experimental.pallas import tpu as pltpu

F32 = jnp.float32
BF16 = jnp.bfloat16

NORM_EPS = 1e-6
SSD_HEAD_DIM = 64
SSD_GROUPS = 2
SSD_STATE = 128
SSD_CONV = 4
SSD_CHUNK = 128
SSD_CHUNKS_PER_STEP = 2
ATT_QK_DIM = 64
ATT_V_DIM = 128
LANES = 128
CONV_HALO = 16
LOG2_E = math.log2(math.e)

FFN_ROWS = 512
FFN_COLS = 256
ATT_BLOCK_Q = 1024
ATT_SUM_ROWS = 16
VMEM_LIMIT = 56 * 1024 * 1024


def _resident(shape):
    nd = len(shape)
    return pl.BlockSpec(shape, lambda *_: (0,) * nd, pipeline_mode=pl.Buffered(1))


def _rms_norm(x, w):
    return x * lax.rsqrt(jnp.mean(x * x, axis=-1, keepdims=True) + NORM_EPS) * w


def _silu(x):
    h = 0.5 * x
    return h + h * jnp.tanh(h)


def _swiglu(h, wg_ref, wu_ref, wd_ref):
    d_ff = wg_ref.shape[1]
    acc = None
    for c in range(d_ff // FFN_COLS):
        sl = slice(c * FFN_COLS, (c + 1) * FFN_COLS)
        g = jnp.dot(h, wg_ref[:, sl], preferred_element_type=F32)
        u = jnp.dot(h, wu_ref[:, sl], preferred_element_type=F32)
        a = (_silu(g) * u).astype(BF16)
        d = jnp.dot(a, wd_ref[sl, :], preferred_element_type=F32)
        acc = d if acc is None else acc + d
    return acc


def _ffn_kernel(x_ref, nw_ref, wg_ref, wu_ref, wd_ref, o_ref):
    x = x_ref[...]
    h = _rms_norm(x, nw_ref[...]).astype(BF16)
    o_ref[...] = x + 0.5 * _swiglu(h, wg_ref, wu_ref, wd_ref)


def _ffn(x2d, norm_w, wg, wu, wd):
    t, d = x2d.shape
    tm = min(FFN_ROWS, t)
    row = pl.BlockSpec((tm, d), lambda i: (i, 0))
    return pl.pallas_call(
        _ffn_kernel,
        grid=(t // tm,),
        in_specs=[row, _resident((1, d)), _resident(wg.shape), _resident(wu.shape), _resident(wd.shape)],
        out_specs=row,
        out_shape=jax.ShapeDtypeStruct((t, d), F32),
        compiler_params=pltpu.CompilerParams(dimension_semantics=("arbitrary",), vmem_limit_bytes=VMEM_LIMIT),
        name="ffn",
    )(x2d, norm_w, wg, wu, wd)


def _qk_norm(t, w2, scale):
    lo = lax.broadcasted_iota(jnp.int32, t.shape, 1) < ATT_QK_DIM
    x2 = t * t
    s_lo = jnp.sum(jnp.where(lo, x2, 0.0), axis=-1, keepdims=True)
    s_hi = jnp.sum(jnp.where(lo, 0.0, x2), axis=-1, keepdims=True)
    r = lax.rsqrt(jnp.where(lo, s_lo, s_hi) * (1.0 / ATT_QK_DIM) + NORM_EPS)
    return t * r * (w2 * scale)


def _proj_kernel(x_ref, nw_ref, wz_ref, wxbc_ref, wdt_ref, wq_ref, wk_ref, wv_ref, qw_ref, kw_ref,
                 z_ref, xbc_ref, dt_ref, q_ref, k_ref, v_ref):
    h = _rms_norm(x_ref[...], nw_ref[...]).astype(BF16)
    z_ref[...] = jnp.dot(h, wz_ref[...], preferred_element_type=F32).astype(BF16)
    xbc_ref[...] = jnp.dot(h, wxbc_ref[...], preferred_element_type=F32).astype(BF16)
    dt_ref[...] = jnp.dot(h, wdt_ref[...], preferred_element_type=F32)
    v_ref[...] = jnp.dot(h, wv_ref[...], preferred_element_type=F32).astype(BF16)
    q = jnp.dot(h, wq_ref[...], preferred_element_type=F32)
    k = jnp.dot(h, wk_ref[...], preferred_element_type=F32)
    n_heads = q.shape[1] // LANES
    for hd in range(n_heads):
        sl = slice(hd * LANES, (hd + 1) * LANES)
        q_ref[:, sl] = _qk_norm(q[:, sl], qw_ref[...], ATT_QK_DIM ** -0.5 * math.log2(math.e)).astype(BF16)
        k_ref[:, sl] = _qk_norm(k[:, sl], kw_ref[...], 1.0).astype(BF16)


def _proj(x2d, norm_w, wz, wxbc, wdt, wq, wk, wv, qw2, kw2):
    t, d = x2d.shape
    tm = min(FFN_ROWS, t)
    row = lambda n: pl.BlockSpec((tm, n), lambda i: (i, 0))
    widths = (wz.shape[1], wxbc.shape[1], wdt.shape[1], wq.shape[1], wk.shape[1], wv.shape[1])
    dtypes = (BF16, BF16, F32, BF16, BF16, BF16)
    return pl.pallas_call(
        _proj_kernel,
        grid=(t // tm,),
        in_specs=[row(d), _resident((1, d))] + [_resident(w.shape) for w in (wz, wxbc, wdt, wq, wk, wv)]
                 + [_resident((1, LANES)), _resident((1, LANES))],
        out_specs=[row(n) for n in widths],
        out_shape=[jax.ShapeDtypeStruct((t, n), dt) for n, dt in zip(widths, dtypes)],
        compiler_params=pltpu.CompilerParams(dimension_semantics=("arbitrary",), vmem_limit_bytes=VMEM_LIMIT),
        name="proj",
    )(x2d, norm_w, wz, wxbc, wdt, wq, wk, wv, qw2, kw2)


def _lane_bcast(x, col):
    return jnp.broadcast_to(x[:, col:col + 1], x.shape)


def _expand_heads(x, pair):
    lo = lax.broadcasted_iota(jnp.int32, x.shape, 1) < SSD_HEAD_DIM
    return jnp.where(lo, _lane_bcast(x, 2 * pair), _lane_bcast(x, 2 * pair + 1))


def _ssd_kernel(z_ref, xbc_ref, halo_ref, dt_ref, cw_ref, cb_ref, dtb_ref, alog_ref, dskip_ref, nw_ref,
                y_ref, state_ref):
    L = SSD_CHUNK
    n_chunks = xbc_ref.shape[0] // L
    width = z_ref.shape[-1]
    n_pairs = width // LANES
    heads_per_group = (width // SSD_HEAD_DIM) // SSD_GROUPS
    gw = width // SSD_GROUPS
    first = pl.program_id(1) == 0

    @pl.when(first)
    def _():
        state_ref[...] = jnp.zeros(state_ref.shape, F32)

    out_t = lax.broadcasted_iota(jnp.int32, (L, CONV_HALO + L), 0)
    in_t = lax.broadcasted_iota(jnp.int32, (L, CONV_HALO + L), 1) - CONV_HALO
    shifts = [jnp.where(in_t == out_t - (SSD_CONV - 1 - kk), 1.0, 0.0).astype(BF16) for kk in range(SSD_CONV - 1)]
    causal = lax.broadcasted_iota(jnp.int32, (L, L), 0) >= lax.broadcasted_iota(jnp.int32, (L, L), 1)
    tril = causal.astype(F32)
    lo = lax.broadcasted_iota(jnp.int32, (L, LANES), 1) < SSD_HEAD_DIM
    zero = jnp.zeros((L, LANES), BF16)
    head_of_lane = lax.broadcasted_iota(jnp.int32, (LANES, width), 1) // SSD_HEAD_DIM
    expand = (lax.broadcasted_iota(jnp.int32, (LANES, width), 0) == head_of_lane).astype(BF16)
    neg_a = -jnp.exp(alog_ref[...])

    halo = halo_ref[...]
    tail = jnp.where(first, jnp.zeros_like(halo), halo)
    for ci in range(n_chunks):
        rows = slice(ci * L, (ci + 1) * L)

        x_cur = xbc_ref[rows, :]
        x_ext = jnp.concatenate([tail, x_cur], axis=0)
        tail = x_cur[L - CONV_HALO:, :]
        conv = cb_ref[...] + cw_ref[SSD_CONV - 1:SSD_CONV, :] * x_cur.astype(F32)
        for kk in range(SSD_CONV - 1):
            conv = conv + cw_ref[kk:kk + 1, :] * jnp.dot(shifts[kk], x_ext, preferred_element_type=F32)
        act = _silu(conv)
        xs = act[:, :width]
        bm = act[:, width:width + SSD_GROUPS * SSD_STATE].astype(BF16)
        cm = act[:, width + SSD_GROUPS * SSD_STATE:]

        raw = dt_ref[rows, :] + dtb_ref[...]
        dt = jnp.maximum(raw, 0.0) + jnp.log1p(jnp.exp(-jnp.abs(raw)))
        a_cs = LOG2_E * jnp.dot(tril, dt * neg_a, preferred_element_type=F32, precision=lax.Precision.HIGHEST)
        a_last = a_cs[L - 1:L, :]
        f = dt * jnp.exp2(a_last - a_cs)
        a_src_t = a_cs.T - jnp.log2(dt).T

        cm_bf = cm.astype(BF16)
        cb = [lax.dot_general(cm_bf[:, g * SSD_STATE:(g + 1) * SSD_STATE], bm[:, g * SSD_STATE:(g + 1) * SSD_STATE],
                              (((1,), (1,)), ((), ())), preferred_element_type=F32) for g in range(SSD_GROUPS)]
        xs_bf = xs.astype(BF16)
        y_tiles = []
        for pair in range(n_pairs):
            g = (2 * pair) // heads_per_group
            cm_g = cm[:, g * SSD_STATE:(g + 1) * SSD_STATE]
            sl = slice(pair * LANES, (pair + 1) * LANES)
            x_pair = xs_bf[:, sl]
            prev_pair = state_ref[:, sl].astype(BF16)
            y_pair = None
            for half in range(2):
                hd = 2 * pair + half
                a_col = _lane_bcast(a_cs, hd)
                seg = a_col - a_src_t[hd:hd + 1, :]
                m_h = (cb[g] * jnp.exp2(jnp.where(causal, seg, -jnp.inf))).astype(BF16)
                c_h = (cm_g * jnp.exp2(a_col)).astype(BF16)
                keep = lo if half == 0 else jnp.logical_not(lo)
                rhs = jnp.concatenate([jnp.where(keep, x_pair, zero), jnp.where(keep, prev_pair, zero)], axis=0)
                yh = jnp.dot(jnp.concatenate([m_h, c_h], axis=1), rhs, preferred_element_type=F32)
                y_pair = yh if y_pair is None else y_pair + yh
            y_tiles.append(y_pair)

        f_hi = f.astype(BF16)
        f_lo = (f - f_hi.astype(F32)).astype(BF16)
        f_exp = (jnp.dot(f_hi, expand, preferred_element_type=F32)
                 + jnp.dot(f_lo, expand, preferred_element_type=F32))
        ea_last = jnp.exp2(a_cs[L - 8:L, :])
        for pair in range(n_pairs):
            g = (2 * pair) // heads_per_group
            sl = slice(pair * LANES, (pair + 1) * LANES)
            bm_g = bm[:, g * SSD_STATE:(g + 1) * SSD_STATE]
            xw = (xs[:, sl] * f_exp[:, sl]).astype(BF16)
            new = lax.dot_general(bm_g, xw, (((0,), (0,)), ((), ())), preferred_element_type=F32)
            chunk_decay = _expand_heads(ea_last, pair)[7:8, :]
            state_ref[:, sl] = state_ref[:, sl] * chunk_decay + new

        y = jnp.concatenate(y_tiles, axis=1) + xs * dskip_ref[...]
        gated = y * _silu(z_ref[rows, :].astype(F32))
        for g in range(SSD_GROUPS):
            sl = slice(g * gw, (g + 1) * gw)
            y_ref[rows, sl] = _rms_norm(gated[:, sl], nw_ref[:, sl]).astype(BF16)


def _ssd(z, xbc, dt, conv_w, conv_b, dt_bias, a_log, d_skip_exp, norm_w):
    bsz, s, width = z.shape
    rows = SSD_CHUNK * SSD_CHUNKS_PER_STEP
    cw = xbc.shape[-1]
    blk = lambda n: pl.BlockSpec((None, rows, n), lambda b, c: (b, c, 0))
    halo = pl.BlockSpec((None, CONV_HALO, cw), lambda b, c: (b, jnp.maximum(c * (rows // CONV_HALO) - 1, 0), 0))
    return pl.pallas_call(
        _ssd_kernel,
        grid=(bsz, s // rows),
        in_specs=[blk(width), blk(cw), halo, blk(LANES), _resident(conv_w.shape), _resident((1, cw)),
                  _resident((1, LANES)), _resident((1, LANES)), _resident((1, width)), _resident((1, width))],
        out_specs=blk(width),
        out_shape=jax.ShapeDtypeStruct((bsz, s, width), BF16),
        scratch_shapes=[pltpu.VMEM((SSD_STATE, width), F32)],
        compiler_params=pltpu.CompilerParams(dimension_semantics=("arbitrary", "arbitrary"),
                                             vmem_limit_bytes=VMEM_LIMIT),
        name="ssd",
    )(z, xbc, xbc, dt, conv_w, conv_b, dt_bias, a_log, d_skip_exp, norm_w)


def _attn_kernel(q_ref, k_ref, v_ref, lq1_ref, lk1_ref, lq2_ref, lk2_ref, sw_ref, o_ref, s_ref, acc_ref, *,
                 lambda_init):
    tq = o_ref.shape[0]
    tk = s_ref.shape[2]
    n_q = q_ref.shape[0] // tq
    qi = pl.program_id(2)

    def q_maps_t(blk):
        q = q_ref[pl.ds(pl.multiple_of(blk * tq, tq), tq), :]
        lo = lax.broadcasted_iota(jnp.int32, q.shape, 1) < ATT_QK_DIM
        zero = jnp.zeros_like(q)
        return jnp.where(lo, q, zero).T, jnp.where(lo, zero, q).T

    def qk(row0, slot, q_t, col0=0):
        k = k_ref[pl.ds(pl.multiple_of(row0, tk), tk), :]
        for mp in range(2):
            s_ref[slot, mp, :, col0:] = jnp.dot(k, q_t[mp][:, col0:], preferred_element_type=F32)

    ones_rows = jnp.ones((ATT_SUM_ROWS, tk), BF16)

    def softmax_pv(row0, slot, m, col0=0, mask=None):
        v_ext = jnp.concatenate([v_ref[pl.ds(pl.multiple_of(row0, tk), tk), :].T, ones_rows], axis=0)
        m_out = []
        for mp in range(2):
            s = s_ref[slot, mp, :, col0:]
            if mask is not None:
                s = jnp.where(mask, s, -jnp.inf)
            m_old = m[mp][:, col0:]
            m_new = jnp.maximum(m_old, jnp.max(s, axis=0, keepdims=True))
            p = jnp.exp2(s - m_new).astype(BF16)
            acc_ref[mp, :, col0:] = (jnp.exp2(m_old - m_new) * acc_ref[mp, :, col0:]
                                     + jnp.dot(v_ext, p, preferred_element_type=F32))
            m_out.append(m_new if col0 == 0 else jnp.concatenate([m[mp][:, :col0], m_new], axis=1))
        return tuple(m_out)

    q_t = q_maps_t(qi)

    @pl.when(qi == 0)
    def _():
        qk(0, 0, q_t)

    acc_ref[...] = jnp.zeros(acc_ref.shape, F32)
    neg_inf = jnp.full((1, tq), -jnp.inf, F32)

    def pair(i, m):
        row0 = i * tq
        qk(row0 + tk, 1, q_t)
        m = softmax_pv(row0, 0, m)
        qk(row0 + tq, 0, q_t)
        return softmax_pv(row0 + tk, 1, m)

    m = lax.fori_loop(0, qi, pair, (neg_inf, neg_inf))

    row0 = qi * tq
    causal = (lax.broadcasted_iota(jnp.int32, (tk, tq), 0) <= lax.broadcasted_iota(jnp.int32, (tk, tq), 1))
    qk(row0 + tk, 1, q_t, col0=tk)
    m = softmax_pv(row0, 0, m, mask=causal)
    qk(0, 0, q_maps_t(jnp.minimum(qi + 1, n_q - 1)))
    softmax_pv(row0 + tk, 1, m, col0=tk, mask=causal[:, :tk])

    lam = (jnp.exp(jnp.sum(lq1_ref[...] * lk1_ref[...], axis=-1, keepdims=True))
           - jnp.exp(jnp.sum(lq2_ref[...] * lk2_ref[...], axis=-1, keepdims=True)) + lambda_init)
    o_t = (acc_ref[0, :ATT_V_DIM] / acc_ref[0, ATT_V_DIM:ATT_V_DIM + 1]
           - lam * (acc_ref[1, :ATT_V_DIM] / acc_ref[1, ATT_V_DIM:ATT_V_DIM + 1]))
    o_ref[...] = (_rms_norm(o_t.T, sw_ref[...]) * (1.0 - lambda_init)).astype(BF16)


def _attn(q, k, v, lq1, lk1, lq2, lk2, subln_w, lambda_init):
    bsz, s, width = q.shape
    n_heads = width // ATT_V_DIM
    tq = min(ATT_BLOCK_Q, s)
    o_spec = pl.BlockSpec((None, tq, LANES), lambda b, h, i: (b, i, h))
    head_spec = pl.BlockSpec((None, s, LANES), lambda b, h, i: (b, 0, h))
    small = _resident((1, ATT_QK_DIM))
    return pl.pallas_call(
        functools.partial(_attn_kernel, lambda_init=lambda_init),
        grid=(bsz, n_heads, s // tq),
        in_specs=[head_spec, head_spec, head_spec, small, small, small, small, _resident((1, ATT_V_DIM))],
        out_specs=o_spec,
        out_shape=jax.ShapeDtypeStruct((bsz, s, width), BF16),
        scratch_shapes=[pltpu.VMEM((2, 2, tq // 2, tq), F32),
                        pltpu.VMEM((2, ATT_V_DIM + ATT_SUM_ROWS, tq), F32)],
        compiler_params=pltpu.CompilerParams(dimension_semantics=("arbitrary",) * 3, vmem_limit_bytes=VMEM_LIMIT),
        name="attn",
    )(q, k, v, lq1, lk1, lq2, lk2, subln_w)


def _out_kernel(x_ref, ys_ref, ya_ref, wos_ref, woa_ref, nw_ref, wg_ref, wu_ref, wd_ref, o_ref):
    x = (x_ref[...] + jnp.dot(ys_ref[...], wos_ref[...], preferred_element_type=F32)
         + jnp.dot(ya_ref[...], woa_ref[...], preferred_element_type=F32))
    h = _rms_norm(x, nw_ref[...]).astype(BF16)
    o_ref[...] = x + 0.5 * _swiglu(h, wg_ref, wu_ref, wd_ref)


def _out(x2d, y_ssd, y_att, wo_ssd, wo_att, norm_w, wg, wu, wd):
    t, d = x2d.shape
    tm = min(FFN_ROWS, t)
    row = lambda n: pl.BlockSpec((tm, n), lambda i: (i, 0))
    return pl.pallas_call(
        _out_kernel,
        grid=(t // tm,),
        in_specs=[row(d), row(y_ssd.shape[1]), row(y_att.shape[1]), _resident(wo_ssd.shape), _resident(wo_att.shape),
                  _resident((1, d)), _resident(wg.shape), _resident(wu.shape), _resident(wd.shape)],
        out_specs=row(d),
        out_shape=jax.ShapeDtypeStruct((t, d), F32),
        compiler_params=pltpu.CompilerParams(dimension_semantics=("arbitrary",), vmem_limit_bytes=VMEM_LIMIT),
        name="out_ffn",
    )(x2d, y_ssd, y_att, wo_ssd, wo_att, norm_w, wg, wu, wd)


def _pad_lanes(v):
    return jnp.zeros((1, LANES), F32).at[0, :v.shape[0]].set(v.astype(F32))


def kernel(x, ffn1_norm_w, ffn1_w_gate, ffn1_w_up, ffn1_w_down, mix_norm_w, w_in, conv_w, conv_b, dt_bias, a_log,
           d_skip, ssd_norm_w, q_norm_w, k_norm_w, lambda_q1, lambda_k1, lambda_q2, lambda_k2, attn_subln_w, w_out,
           ffn2_norm_w, ffn2_w_gate, ffn2_w_up, ffn2_w_down):
    bsz, s, d = x.shape
    depth = w_in.shape[0]
    ssd_heads = dt_bias.shape[1]
    ssd_width = ssd_heads * SSD_HEAD_DIM
    conv_ch = conv_w.shape[2]
    att_width = w_out.shape[1] - ssd_width
    qk_width = (att_width // ATT_V_DIM) * 2 * ATT_QK_DIM
    sizes = (ssd_width, conv_ch, ssd_heads, qk_width, qk_width, att_width)
    offs = [0]
    for n in sizes:
        offs.append(offs[-1] + n)
    row = lambda v: v.astype(F32).reshape(1, -1)

    x2d = x.reshape(bsz * s, d)
    for l in range(depth):
        lambda_init = 0.8 - 0.6 * math.exp(-0.3 * l)
        w = w_in[l].astype(BF16)
        wz, wxbc, wdt, wq, wk, wv = (w[:, offs[i]:offs[i + 1]] for i in range(6))
        wdt = jnp.pad(wdt, ((0, 0), (0, LANES - ssd_heads)))
        wo = w_out[l].astype(BF16)

        x1 = _ffn(x2d, row(ffn1_norm_w[l]), ffn1_w_gate[l].astype(BF16), ffn1_w_up[l].astype(BF16),
                  ffn1_w_down[l].astype(BF16))
        z, xbc, dt, q, k, v = _proj(x1, row(mix_norm_w[l]), wz, wxbc, wdt, wq, wk, wv,
                                    row(jnp.tile(q_norm_w[l], 2)), row(jnp.tile(k_norm_w[l], 2)))
        y_ssd = _ssd(z.reshape(bsz, s, -1), xbc.reshape(bsz, s, -1), dt.reshape(bsz, s, -1),
                     conv_w[l].astype(F32), row(conv_b[l]), _pad_lanes(dt_bias[l]), _pad_lanes(a_log[l]),
                     row(jnp.repeat(d_skip[l], SSD_HEAD_DIM)), row(ssd_norm_w[l]))
        y_att = _attn(q.reshape(bsz, s, -1), k.reshape(bsz, s, -1), v.reshape(bsz, s, -1),
                      row(lambda_q1[l]), row(lambda_k1[l]), row(lambda_q2[l]), row(lambda_k2[l]),
                      row(attn_subln_w[l]), lambda_init)
        x2d = _out(x1, y_ssd.reshape(bsz * s, -1), y_att.reshape(bsz * s, -1), wo[:ssd_width], wo[ssd_width:],
                   row(ffn2_norm_w[l]), ffn2_w_gate[l].astype(BF16), ffn2_w_up[l].astype(BF16),
                   ffn2_w_down[l].astype(BF16))
    return x2d.reshape(bsz, s, d)
```

```python
import functools
import math

import jax
import jax.numpy as jnp
from jax import lax
from jax.experimental import pallas as pl
from jax.experimental.pallas import tpu as pltpu

F32 = jnp.float32
BF16 = jnp.bfloat16

NORM_EPS = 1e-6
SSD_HEAD_DIM = 64
SSD_GROUPS = 2
SSD_STATE = 128
SSD_CONV = 4
SSD_CHUNK = 128
SSD_CHUNKS_PER_STEP = 2
ATT_QK_DIM = 64
ATT_V_DIM = 128
LANES = 128
CONV_HALO = 16
LOG2_E = math.log2(math.e)

FFN_ROWS = 512
FFN_COLS = 256
ATT_BLOCK_Q = 1024
ATT_COL_TILE = 256
ATT_SUM_ROWS = 16
VMEM_LIMIT = 56 * 1024 * 1024


def _resident(shape):
    nd = len(shape)
    return pl.BlockSpec(shape, lambda *_: (0,) * nd, pipeline_mode=pl.Buffered(1))


def _rms_norm(x, w):
    return x * lax.rsqrt(jnp.mean(x * x, axis=-1, keepdims=True) + NORM_EPS) * w


def _silu(x):
    h = 0.5 * x
    return h + h * jnp.tanh(h)


def _swiglu(h, wg_ref, wu_ref, wd_ref):
    d_ff = wg_ref.shape[1]
    acc = None
    for c in range(d_ff // FFN_COLS):
        sl = slice(c * FFN_COLS, (c + 1) * FFN_COLS)
        g = jnp.dot(h, wg_ref[:, sl], preferred_element_type=F32)
        u = jnp.dot(h, wu_ref[:, sl], preferred_element_type=F32)
        a = (_silu(g) * u).astype(BF16)
        d = jnp.dot(a, wd_ref[sl, :], preferred_element_type=F32)
        acc = d if acc is None else acc + d
    return acc


def _ffn_kernel(x_ref, nw_ref, wg_ref, wu_ref, wd_ref, o_ref):
    x = x_ref[...]
    h = _rms_norm(x, nw_ref[...]).astype(BF16)
    o_ref[...] = x + 0.5 * _swiglu(h, wg_ref, wu_ref, wd_ref)


def _ffn(x2d, norm_w, wg, wu, wd):
    t, d = x2d.shape
    tm = min(FFN_ROWS, t)
    row = pl.BlockSpec((tm, d), lambda i: (i, 0))
    return pl.pallas_call(
        _ffn_kernel,
        grid=(t // tm,),
        in_specs=[row, _resident((1, d)), _resident(wg.shape), _resident(wu.shape), _resident(wd.shape)],
        out_specs=row,
        out_shape=jax.ShapeDtypeStruct((t, d), F32),
        compiler_params=pltpu.CompilerParams(dimension_semantics=("arbitrary",), vmem_limit_bytes=VMEM_LIMIT),
        name="ffn",
    )(x2d, norm_w, wg, wu, wd)


def _qk_norm(t, w2, scale):
    lo = lax.broadcasted_iota(jnp.int32, t.shape, 1) < ATT_QK_DIM
    x2 = t * t
    s_lo = jnp.sum(jnp.where(lo, x2, 0.0), axis=-1, keepdims=True)
    s_hi = jnp.sum(jnp.where(lo, 0.0, x2), axis=-1, keepdims=True)
    r = lax.rsqrt(jnp.where(lo, s_lo, s_hi) * (1.0 / ATT_QK_DIM) + NORM_EPS)
    return t * r * (w2 * scale)


def _proj_kernel(x_ref, nw_ref, wz_ref, wxbc_ref, wdt_ref, wq_ref, wk_ref, wv_ref, qw_ref, kw_ref,
                 z_ref, xbc_ref, dt_ref, q_ref, k_ref, v_ref):
    h = _rms_norm(x_ref[...], nw_ref[...]).astype(BF16)
    z_ref[...] = jnp.dot(h, wz_ref[...], preferred_element_type=F32).astype(BF16)
    xbc_ref[...] = jnp.dot(h, wxbc_ref[...], preferred_element_type=F32).astype(BF16)
    dt_ref[...] = jnp.dot(h, wdt_ref[...], preferred_element_type=F32)
    v_ref[...] = jnp.dot(h, wv_ref[...], preferred_element_type=F32).astype(BF16)
    q = jnp.dot(h, wq_ref[...], preferred_element_type=F32)
    k = jnp.dot(h, wk_ref[...], preferred_element_type=F32)
    n_heads = q.shape[1] // LANES
    for hd in range(n_heads):
        sl = slice(hd * LANES, (hd + 1) * LANES)
        q_ref[:, sl] = _qk_norm(q[:, sl], qw_ref[...], ATT_QK_DIM ** -0.5 * math.log2(math.e)).astype(BF16)
        k_ref[:, sl] = _qk_norm(k[:, sl], kw_ref[...], 1.0).astype(BF16)


def _proj(x2d, norm_w, wz, wxbc, wdt, wq, wk, wv, qw2, kw2):
    t, d = x2d.shape
    tm = min(FFN_ROWS, t)
    row = lambda n: pl.BlockSpec((tm, n), lambda i: (i, 0))
    widths = (wz.shape[1], wxbc.shape[1], wdt.shape[1], wq.shape[1], wk.shape[1], wv.shape[1])
    dtypes = (BF16, BF16, F32, BF16, BF16, BF16)
    return pl.pallas_call(
        _proj_kernel,
        grid=(t // tm,),
        in_specs=[row(d), _resident((1, d))] + [_resident(w.shape) for w in (wz, wxbc, wdt, wq, wk, wv)]
                 + [_resident((1, LANES)), _resident((1, LANES))],
        out_specs=[row(n) for n in widths],
        out_shape=[jax.ShapeDtypeStruct((t, n), dt) for n, dt in zip(widths, dtypes)],
        compiler_params=pltpu.CompilerParams(dimension_semantics=("arbitrary",), vmem_limit_bytes=VMEM_LIMIT),
        name="proj",
    )(x2d, norm_w, wz, wxbc, wdt, wq, wk, wv, qw2, kw2)


def _lane_bcast(x, col):
    return jnp.broadcast_to(x[:, col:col + 1], x.shape)


def _expand_heads(x, pair):
    lo = lax.broadcasted_iota(jnp.int32, x.shape, 1) < SSD_HEAD_DIM
    return jnp.where(lo, _lane_bcast(x, 2 * pair), _lane_bcast(x, 2 * pair + 1))


def _ssd_kernel(z_ref, xbc_ref, halo_ref, dt_ref, cw_ref, cb_ref, dtb_ref, alog_ref, dskip_ref, nw_ref,
                y_ref, state_ref):
    L = SSD_CHUNK
    n_chunks = xbc_ref.shape[0] // L
    width = z_ref.shape[-1]
    n_pairs = width // LANES
    heads_per_group = (width // SSD_HEAD_DIM) // SSD_GROUPS
    gw = width // SSD_GROUPS
    first = pl.program_id(1) == 0

    @pl.when(first)
    def _():
        state_ref[...] = jnp.zeros(state_ref.shape, F32)

    out_t = lax.broadcasted_iota(jnp.int32, (L, CONV_HALO + L), 0)
    in_t = lax.broadcasted_iota(jnp.int32, (L, CONV_HALO + L), 1) - CONV_HALO
    shifts = [jnp.where(in_t == out_t - (SSD_CONV - 1 - kk), 1.0, 0.0).astype(BF16) for kk in range(SSD_CONV - 1)]
    causal = lax.broadcasted_iota(jnp.int32, (L, L), 0) >= lax.broadcasted_iota(jnp.int32, (L, L), 1)
    tril = causal.astype(F32)
    lo = lax.broadcasted_iota(jnp.int32, (L, LANES), 1) < SSD_HEAD_DIM
    zero = jnp.zeros((L, LANES), BF16)
    head_of_lane = lax.broadcasted_iota(jnp.int32, (LANES, width), 1) // SSD_HEAD_DIM
    expand = (lax.broadcasted_iota(jnp.int32, (LANES, width), 0) == head_of_lane).astype(BF16)
    neg_a = -jnp.exp(alog_ref[...])

    halo = halo_ref[...]
    tail = jnp.where(first, jnp.zeros_like(halo), halo)
    for ci in range(n_chunks):
        rows = slice(ci * L, (ci + 1) * L)

        x_cur = xbc_ref[rows, :]
        x_ext = jnp.concatenate([tail, x_cur], axis=0)
        tail = x_cur[L - CONV_HALO:, :]
        conv = cb_ref[...] + cw_ref[SSD_CONV - 1:SSD_CONV, :] * x_cur.astype(F32)
        for kk in range(SSD_CONV - 1):
            conv = conv + cw_ref[kk:kk + 1, :] * jnp.dot(shifts[kk], x_ext, preferred_element_type=F32)
        act = _silu(conv)
        xs = act[:, :width]
        bm = act[:, width:width + SSD_GROUPS * SSD_STATE].astype(BF16)
        cm = act[:, width + SSD_GROUPS * SSD_STATE:]

        raw = dt_ref[rows, :] + dtb_ref[...]
        dt = jnp.maximum(raw, 0.0) + jnp.log1p(jnp.exp(-jnp.abs(raw)))
        a_cs = LOG2_E * jnp.dot(tril, dt * neg_a, preferred_element_type=F32, precision=lax.Precision.HIGHEST)
        a_last = a_cs[L - 1:L, :]
        f = dt * jnp.exp2(a_last - a_cs)
        a_src_t = a_cs.T - jnp.log2(dt).T

        cm_bf = cm.astype(BF16)
        cb = [lax.dot_general(cm_bf[:, g * SSD_STATE:(g + 1) * SSD_STATE], bm[:, g * SSD_STATE:(g + 1) * SSD_STATE],
                              (((1,), (1,)), ((), ())), preferred_element_type=F32) for g in range(SSD_GROUPS)]
        xs_bf = xs.astype(BF16)
        y_tiles = []
        for pair in range(n_pairs):
            g = (2 * pair) // heads_per_group
            cm_g = cm[:, g * SSD_STATE:(g + 1) * SSD_STATE]
            sl = slice(pair * LANES, (pair + 1) * LANES)
            x_pair = xs_bf[:, sl]
            prev_pair = state_ref[:, sl].astype(BF16)
            y_pair = None
            for half in range(2):
                hd = 2 * pair + half
                a_col = _lane_bcast(a_cs, hd)
                seg = a_col - a_src_t[hd:hd + 1, :]
                m_h = (cb[g] * jnp.exp2(jnp.where(causal, seg, -jnp.inf))).astype(BF16)
                c_h = (cm_g * jnp.exp2(a_col)).astype(BF16)
                keep = lo if half == 0 else jnp.logical_not(lo)
                rhs = jnp.concatenate([jnp.where(keep, x_pair, zero), jnp.where(keep, prev_pair, zero)], axis=0)
                yh = jnp.dot(jnp.concatenate([m_h, c_h], axis=1), rhs, preferred_element_type=F32)
                y_pair = yh if y_pair is None else y_pair + yh
            y_tiles.append(y_pair)

        f_hi = f.astype(BF16)
        f_lo = (f - f_hi.astype(F32)).astype(BF16)
        f_exp = (jnp.dot(f_hi, expand, preferred_element_type=F32)
                 + jnp.dot(f_lo, expand, preferred_element_type=F32))
        ea_last = jnp.exp2(a_cs[L - 8:L, :])
        for pair in range(n_pairs):
            g = (2 * pair) // heads_per_group
            sl = slice(pair * LANES, (pair + 1) * LANES)
            bm_g = bm[:, g * SSD_STATE:(g + 1) * SSD_STATE]
            xw = (xs[:, sl] * f_exp[:, sl]).astype(BF16)
            new = lax.dot_general(bm_g, xw, (((0,), (0,)), ((), ())), preferred_element_type=F32)
            chunk_decay = _expand_heads(ea_last, pair)[7:8, :]
            state_ref[:, sl] = state_ref[:, sl] * chunk_decay + new

        y = jnp.concatenate(y_tiles, axis=1) + xs * dskip_ref[...]
        gated = y * _silu(z_ref[rows, :].astype(F32))
        for g in range(SSD_GROUPS):
            sl = slice(g * gw, (g + 1) * gw)
            y_ref[rows, sl] = _rms_norm(gated[:, sl], nw_ref[:, sl]).astype(BF16)


def _ssd(z, xbc, dt, conv_w, conv_b, dt_bias, a_log, d_skip_exp, norm_w):
    bsz, s, width = z.shape
    rows = SSD_CHUNK * SSD_CHUNKS_PER_STEP
    cw = xbc.shape[-1]
    blk = lambda n: pl.BlockSpec((None, rows, n), lambda b, c: (b, c, 0))
    halo = pl.BlockSpec((None, CONV_HALO, cw), lambda b, c: (b, jnp.maximum(c * (rows // CONV_HALO) - 1, 0), 0))
    return pl.pallas_call(
        _ssd_kernel,
        grid=(bsz, s // rows),
        in_specs=[blk(width), blk(cw), halo, blk(LANES), _resident(conv_w.shape), _resident((1, cw)),
                  _resident((1, LANES)), _resident((1, LANES)), _resident((1, width)), _resident((1, width))],
        out_specs=blk(width),
        out_shape=jax.ShapeDtypeStruct((bsz, s, width), BF16),
        scratch_shapes=[pltpu.VMEM((SSD_STATE, width), F32)],
        compiler_params=pltpu.CompilerParams(dimension_semantics=("arbitrary", "arbitrary"),
                                             vmem_limit_bytes=VMEM_LIMIT),
        name="ssd",
    )(z, xbc, xbc, dt, conv_w, conv_b, dt_bias, a_log, d_skip_exp, norm_w)


def _attn_kernel(q_ref, k_ref, v_ref, lq1_ref, lk1_ref, lq2_ref, lk2_ref, sw_ref, o_ref, s_ref, acc_ref, *,
                 lambda_init):
    tq = o_ref.shape[0]
    tk = s_ref.shape[2]
    tc = ATT_COL_TILE
    n_ct = tq // tc
    n_q = q_ref.shape[0] // tq
    qi = pl.program_id(2)
    cols = [slice(c * tc, (c + 1) * tc) for c in range(n_ct)]

    def q_maps_t(blk):
        q = q_ref[pl.ds(pl.multiple_of(blk * tq, tq), tq), :]
        lo = lax.broadcasted_iota(jnp.int32, q.shape, 1) < ATT_QK_DIM
        zero = jnp.zeros_like(q)
        return jnp.where(lo, q, zero).T, jnp.where(lo, zero, q).T

    def keys(row0):
        return k_ref[pl.ds(pl.multiple_of(row0, tk), tk), :]

    ones_rows = jnp.ones((ATT_SUM_ROWS, tk), BF16)

    def values(row0):
        return jnp.concatenate([v_ref[pl.ds(pl.multiple_of(row0, tk), tk), :].T, ones_rows], axis=0)

    def qk(k, slot, q_t, c):
        for mp in range(2):
            s_ref[slot, mp, :, cols[c]] = jnp.dot(k, q_t[mp][:, cols[c]], preferred_element_type=F32)

    def softmax_pv(v_ext, slot, m, c, mask=None):
        m = [list(mm) for mm in m]
        for mp in range(2):
            s = s_ref[slot, mp, :, cols[c]]
            if mask is not None:
                s = jnp.where(mask, s, -jnp.inf)
            m_old = m[mp][c]
            m_new = jnp.maximum(m_old, jnp.max(s, axis=0, keepdims=True))
            p = jnp.exp2(s - m_new).astype(BF16)
            acc_ref[mp, :, cols[c]] = (jnp.exp2(m_old - m_new) * acc_ref[mp, :, cols[c]]
                                       + jnp.dot(v_ext, p, preferred_element_type=F32))
            m[mp][c] = m_new
        return tuple(tuple(mm) for mm in m)

    q_t = q_maps_t(qi)

    @pl.when(qi == 0)
    def _():
        k0 = keys(0)
        for c in range(n_ct):
            qk(k0, 0, q_t, c)

    acc_ref[...] = jnp.zeros(acc_ref.shape, F32)
    neg_inf = tuple(jnp.full((1, tc), -jnp.inf, F32) for _ in range(n_ct))

    def pair(i, m):
        row0 = i * tq
        k_next, v_cur = keys(row0 + tk), values(row0)
        for c in range(n_ct):
            qk(k_next, 1, q_t, c)
            m = softmax_pv(v_cur, 0, m, c)
        k_next, v_cur = keys(row0 + tq), values(row0 + tk)
        for c in range(n_ct):
            qk(k_next, 0, q_t, c)
            m = softmax_pv(v_cur, 1, m, c)
        return m

    m = lax.fori_loop(0, qi, pair, (neg_inf, neg_inf))

    row0 = qi * tq
    r = lax.broadcasted_iota(jnp.int32, (tk, tc), 0)
    col = lax.broadcasted_iota(jnp.int32, (tk, tc), 1)
    n_diag = tk // tc
    masks = [r <= col + c * tc for c in range(n_diag)]
    k_next, v_cur = keys(row0 + tk), values(row0)
    for c in range(n_ct):
        if c >= n_diag:
            qk(k_next, 1, q_t, c)
        m = softmax_pv(v_cur, 0, m, c, mask=masks[c] if c < n_diag else None)
    k_next, v_cur = keys(0), values(row0 + tk)
    q_next = q_maps_t(jnp.minimum(qi + 1, n_q - 1))
    for c in range(n_ct):
        qk(k_next, 0, q_next, c)
        if c >= n_diag:
            m = softmax_pv(v_cur, 1, m, c, mask=masks[c - n_diag])

    lam = (jnp.exp(jnp.sum(lq1_ref[...] * lk1_ref[...], axis=-1, keepdims=True))
           - jnp.exp(jnp.sum(lq2_ref[...] * lk2_ref[...], axis=-1, keepdims=True)) + lambda_init)
    o_t = (acc_ref[0, :ATT_V_DIM] / acc_ref[0, ATT_V_DIM:ATT_V_DIM + 1]
           - lam * (acc_ref[1, :ATT_V_DIM] / acc_ref[1, ATT_V_DIM:ATT_V_DIM + 1]))
    o_ref[...] = (_rms_norm(o_t.T, sw_ref[...]) * (1.0 - lambda_init)).astype(BF16)


def _attn(q, k, v, lq1, lk1, lq2, lk2, subln_w, lambda_init):
    bsz, s, width = q.shape
    n_heads = width // ATT_V_DIM
    tq = min(ATT_BLOCK_Q, s)
    o_spec = pl.BlockSpec((None, tq, LANES), lambda b, h, i: (b, i, h))
    head_spec = pl.BlockSpec((None, s, LANES), lambda b, h, i: (b, 0, h))
    small = _resident((1, ATT_QK_DIM))
    return pl.pallas_call(
        functools.partial(_attn_kernel, lambda_init=lambda_init),
        grid=(bsz, n_heads, s // tq),
        in_specs=[head_spec, head_spec, head_spec, small, small, small, small, _resident((1, ATT_V_DIM))],
        out_specs=o_spec,
        out_shape=jax.ShapeDtypeStruct((bsz, s, width), BF16),
        scratch_shapes=[pltpu.VMEM((2, 2, tq // 2, tq), F32),
                        pltpu.VMEM((2, ATT_V_DIM + ATT_SUM_ROWS, tq), F32)],
        compiler_params=pltpu.CompilerParams(dimension_semantics=("arbitrary",) * 3, vmem_limit_bytes=VMEM_LIMIT),
        name="attn",
    )(q, k, v, lq1, lk1, lq2, lk2, subln_w)


def _out_kernel(x_ref, ys_ref, ya_ref, wos_ref, woa_ref, nw_ref, wg_ref, wu_ref, wd_ref, o_ref):
    x = (x_ref[...] + jnp.dot(ys_ref[...], wos_ref[...], preferred_element_type=F32)
         + jnp.dot(ya_ref[...], woa_ref[...], preferred_element_type=F32))
    h = _rms_norm(x, nw_ref[...]).astype(BF16)
    o_ref[...] = x + 0.5 * _swiglu(h, wg_ref, wu_ref, wd_ref)


def _out(x2d, y_ssd, y_att, wo_ssd, wo_att, norm_w, wg, wu, wd):
    t, d = x2d.shape
    tm = min(FFN_ROWS, t)
    row = lambda n: pl.BlockSpec((tm, n), lambda i: (i, 0))
    return pl.pallas_call(
        _out_kernel,
        grid=(t // tm,),
        in_specs=[row(d), row(y_ssd.shape[1]), row(y_att.shape[1]), _resident(wo_ssd.shape), _resident(wo_att.shape),
                  _resident((1, d)), _resident(wg.shape), _resident(wu.shape), _resident(wd.shape)],
        out_specs=row(d),
        out_shape=jax.ShapeDtypeStruct((t, d), F32),
        compiler_params=pltpu.CompilerParams(dimension_semantics=("arbitrary",), vmem_limit_bytes=VMEM_LIMIT),
        name="out_ffn",
    )(x2d, y_ssd, y_att, wo_ssd, wo_att, norm_w, wg, wu, wd)


def _pad_lanes(v):
    return jnp.zeros((1, LANES), F32).at[0, :v.shape[0]].set(v.astype(F32))


def kernel(x, ffn1_norm_w, ffn1_w_gate, ffn1_w_up, ffn1_w_down, mix_norm_w, w_in, conv_w, conv_b, dt_bias, a_log,
           d_skip, ssd_norm_w, q_norm_w, k_norm_w, lambda_q1, lambda_k1, lambda_q2, lambda_k2, attn_subln_w, w_out,
           ffn2_norm_w, ffn2_w_gate, ffn2_w_up, ffn2_w_down):
    bsz, s, d = x.shape
    depth = w_in.shape[0]
    ssd_heads = dt_bias.shape[1]
    ssd_width = ssd_heads * SSD_HEAD_DIM
    conv_ch = conv_w.shape[2]
    att_width = w_out.shape[1] - ssd_width
    qk_width = (att_width // ATT_V_DIM) * 2 * ATT_QK_DIM
    sizes = (ssd_width, conv_ch, ssd_heads, qk_width, qk_width, att_width)
    offs = [0]
    for n in sizes:
        offs.append(offs[-1] + n)
    row = lambda v: v.astype(F32).reshape(1, -1)

    x2d = x.reshape(bsz * s, d)
    for l in range(depth):
        lambda_init = 0.8 - 0.6 * math.exp(-0.3 * l)
        w = w_in[l].astype(BF16)
        wz, wxbc, wdt, wq, wk, wv = (w[:, offs[i]:offs[i + 1]] for i in range(6))
        wdt = jnp.pad(wdt, ((0, 0), (0, LANES - ssd_heads)))
        wo = w_out[l].astype(BF16)

        x1 = _ffn(x2d, row(ffn1_norm_w[l]), ffn1_w_gate[l].astype(BF16), ffn1_w_up[l].astype(BF16),
                  ffn1_w_down[l].astype(BF16))
        z, xbc, dt, q, k, v = _proj(x1, row(mix_norm_w[l]), wz, wxbc, wdt, wq, wk, wv,
                                    row(jnp.tile(q_norm_w[l], 2)), row(jnp.tile(k_norm_w[l], 2)))
        y_ssd = _ssd(z.reshape(bsz, s, -1), xbc.reshape(bsz, s, -1), dt.reshape(bsz, s, -1),
                     conv_w[l].astype(F32), row(conv_b[l]), _pad_lanes(dt_bias[l]), _pad_lanes(a_log[l]),
                     row(jnp.repeat(d_skip[l], SSD_HEAD_DIM)), row(ssd_norm_w[l]))
        y_att = _attn(q.reshape(bsz, s, -1), k.reshape(bsz, s, -1), v.reshape(bsz, s, -1),
                      row(lambda_q1[l]), row(lambda_k1[l]), row(lambda_q2[l]), row(lambda_k2[l]),
                      row(attn_subln_w[l]), lambda_init)
        x2d = _out(x1, y_ssd.reshape(bsz * s, -1), y_att.reshape(bsz * s, -1), wo[:ssd_width], wo[ssd_width:],
                   row(ffn2_norm_w[l]), ffn2_w_gate[l].astype(BF16), ffn2_w_up[l].astype(BF16),
                   ffn2_w_down[l].astype(BF16))
    return x2d.reshape(bsz, s, d)
```

```python
import functools
import math

import jax
import jax.numpy as jnp
from jax import lax
from jax.experimental import pallas as pl
from jax.experimental.pallas import tpu as pltpu

F32 = jnp.float32
BF16 = jnp.bfloat16

NORM_EPS = 1e-6
SSD_HEAD_DIM = 64
SSD_GROUPS = 2
SSD_STATE = 128
SSD_CONV = 4
SSD_CHUNK = 128
SSD_CHUNKS_PER_STEP = 2
ATT_QK_DIM = 64
ATT_V_DIM = 128
LANES = 128
CONV_HALO = 16
LOG2_E = math.log2(math.e)

FFN_ROWS = 512
FFN_COLS = 256
ATT_BLOCK_Q = 1024
ATT_COL_TILE = 256
ATT_SUM_ROWS = 16
VMEM_LIMIT = 56 * 1024 * 1024


def _resident(shape):
    nd = len(shape)
    return pl.BlockSpec(shape, lambda *_: (0,) * nd, pipeline_mode=pl.Buffered(1))


def _rms_norm(x, w):
    return x * lax.rsqrt(jnp.mean(x * x, axis=-1, keepdims=True) + NORM_EPS) * w


def _silu(x):
    h = 0.5 * x
    return h + h * jnp.tanh(h)


def _swiglu(h, wg_ref, wu_ref, wd_ref):
    d_ff = wg_ref.shape[1]
    acc = None
    for c in range(d_ff // FFN_COLS):
        sl = slice(c * FFN_COLS, (c + 1) * FFN_COLS)
        g = jnp.dot(h, wg_ref[:, sl], preferred_element_type=F32)
        u = jnp.dot(h, wu_ref[:, sl], preferred_element_type=F32)
        a = (_silu(g) * u).astype(BF16)
        d = jnp.dot(a, wd_ref[sl, :], preferred_element_type=F32)
        acc = d if acc is None else acc + d
    return acc


def _ffn_kernel(x_ref, nw_ref, wg_ref, wu_ref, wd_ref, o_ref):
    x = x_ref[...]
    h = _rms_norm(x, nw_ref[...]).astype(BF16)
    o_ref[...] = x + 0.5 * _swiglu(h, wg_ref, wu_ref, wd_ref)


def _ffn(x2d, norm_w, wg, wu, wd):
    t, d = x2d.shape
    tm = min(FFN_ROWS, t)
    row = pl.BlockSpec((tm, d), lambda i: (i, 0))
    return pl.pallas_call(
        _ffn_kernel,
        grid=(t // tm,),
        in_specs=[row, _resident((1, d)), _resident(wg.shape), _resident(wu.shape), _resident(wd.shape)],
        out_specs=row,
        out_shape=jax.ShapeDtypeStruct((t, d), F32),
        compiler_params=pltpu.CompilerParams(dimension_semantics=("arbitrary",), vmem_limit_bytes=VMEM_LIMIT),
        name="ffn",
    )(x2d, norm_w, wg, wu, wd)


def _qk_norm(t, w2, scale):
    lo = lax.broadcasted_iota(jnp.int32, t.shape, 1) < ATT_QK_DIM
    x2 = t * t
    s_lo = jnp.sum(jnp.where(lo, x2, 0.0), axis=-1, keepdims=True)
    s_hi = jnp.sum(jnp.where(lo, 0.0, x2), axis=-1, keepdims=True)
    r = lax.rsqrt(jnp.where(lo, s_lo, s_hi) * (1.0 / ATT_QK_DIM) + NORM_EPS)
    return t * r * (w2 * scale)


def _proj_kernel(x_ref, nw_ref, wz_ref, wxbc_ref, wdt_ref, wq_ref, wk_ref, wv_ref, qw_ref, kw_ref,
                 z_ref, xbc_ref, dt_ref, q_ref, k_ref, v_ref):
    h = _rms_norm(x_ref[...], nw_ref[...]).astype(BF16)
    z_ref[...] = jnp.dot(h, wz_ref[...], preferred_element_type=F32).astype(BF16)
    xbc_ref[...] = jnp.dot(h, wxbc_ref[...], preferred_element_type=F32).astype(BF16)
    dt_ref[...] = jnp.dot(h, wdt_ref[...], preferred_element_type=F32)
    v_ref[...] = jnp.dot(h, wv_ref[...], preferred_element_type=F32).astype(BF16)
    q = jnp.dot(h, wq_ref[...], preferred_element_type=F32)
    k = jnp.dot(h, wk_ref[...], preferred_element_type=F32)
    n_heads = q.shape[1] // LANES
    for hd in range(n_heads):
        sl = slice(hd * LANES, (hd + 1) * LANES)
        q_ref[:, sl] = _qk_norm(q[:, sl], qw_ref[...], ATT_QK_DIM ** -0.5 * math.log2(math.e)).astype(BF16)
        k_ref[:, sl] = _qk_norm(k[:, sl], kw_ref[...], 1.0).astype(BF16)


def _proj(x2d, norm_w, wz, wxbc, wdt, wq, wk, wv, qw2, kw2):
    t, d = x2d.shape
    tm = min(FFN_ROWS, t)
    row = lambda n: pl.BlockSpec((tm, n), lambda i: (i, 0))
    widths = (wz.shape[1], wxbc.shape[1], wdt.shape[1], wq.shape[1], wk.shape[1], wv.shape[1])
    dtypes = (BF16, BF16, F32, BF16, BF16, BF16)
    return pl.pallas_call(
        _proj_kernel,
        grid=(t // tm,),
        in_specs=[row(d), _resident((1, d))] + [_resident(w.shape) for w in (wz, wxbc, wdt, wq, wk, wv)]
                 + [_resident((1, LANES)), _resident((1, LANES))],
        out_specs=[row(n) for n in widths],
        out_shape=[jax.ShapeDtypeStruct((t, n), dt) for n, dt in zip(widths, dtypes)],
        compiler_params=pltpu.CompilerParams(dimension_semantics=("arbitrary",), vmem_limit_bytes=VMEM_LIMIT),
        name="proj",
    )(x2d, norm_w, wz, wxbc, wdt, wq, wk, wv, qw2, kw2)


def _lane_bcast(x, col):
    return jnp.broadcast_to(x[:, col:col + 1], x.shape)


def _expand_heads(x, pair):
    lo = lax.broadcasted_iota(jnp.int32, x.shape, 1) < SSD_HEAD_DIM
    return jnp.where(lo, _lane_bcast(x, 2 * pair), _lane_bcast(x, 2 * pair + 1))


def _ssd_kernel(z_ref, xbc_ref, halo_ref, dt_ref, cw_ref, cb_ref, dtb_ref, alog_ref, dskip_ref, nw_ref,
                y_ref, state_ref):
    L = SSD_CHUNK
    n_chunks = xbc_ref.shape[0] // L
    width = z_ref.shape[-1]
    n_pairs = width // LANES
    heads_per_group = (width // SSD_HEAD_DIM) // SSD_GROUPS
    gw = width // SSD_GROUPS
    first = pl.program_id(1) == 0

    @pl.when(first)
    def _():
        state_ref[...] = jnp.zeros(state_ref.shape, F32)

    out_t = lax.broadcasted_iota(jnp.int32, (L, CONV_HALO + L), 0)
    in_t = lax.broadcasted_iota(jnp.int32, (L, CONV_HALO + L), 1) - CONV_HALO
    shifts = [jnp.where(in_t == out_t - (SSD_CONV - 1 - kk), 1.0, 0.0).astype(BF16) for kk in range(SSD_CONV - 1)]
    causal = lax.broadcasted_iota(jnp.int32, (L, L), 0) >= lax.broadcasted_iota(jnp.int32, (L, L), 1)
    tril = causal.astype(F32)
    lo = lax.broadcasted_iota(jnp.int32, (L, LANES), 1) < SSD_HEAD_DIM
    zero = jnp.zeros((L, LANES), BF16)
    head_of_lane = lax.broadcasted_iota(jnp.int32, (LANES, width), 1) // SSD_HEAD_DIM
    expand = (lax.broadcasted_iota(jnp.int32, (LANES, width), 0) == head_of_lane).astype(BF16)
    neg_a = -jnp.exp(alog_ref[...])

    halo = halo_ref[...]
    tail = jnp.where(first, jnp.zeros_like(halo), halo)
    for ci in range(n_chunks):
        rows = slice(ci * L, (ci + 1) * L)

        x_cur = xbc_ref[rows, :]
        x_ext = jnp.concatenate([tail, x_cur], axis=0)
        tail = x_cur[L - CONV_HALO:, :]
        conv = cb_ref[...] + cw_ref[SSD_CONV - 1:SSD_CONV, :] * x_cur.astype(F32)
        for kk in range(SSD_CONV - 1):
            conv = conv + cw_ref[kk:kk + 1, :] * jnp.dot(shifts[kk], x_ext, preferred_element_type=F32)
        act = _silu(conv)
        xs = act[:, :width]
        bm = act[:, width:width + SSD_GROUPS * SSD_STATE].astype(BF16)
        cm = act[:, width + SSD_GROUPS * SSD_STATE:]

        raw = dt_ref[rows, :] + dtb_ref[...]
        dt = jnp.maximum(raw, 0.0) + jnp.log1p(jnp.exp(-jnp.abs(raw)))
        a_cs = LOG2_E * jnp.dot(tril, dt * neg_a, preferred_element_type=F32, precision=lax.Precision.HIGHEST)
        a_last = a_cs[L - 1:L, :]
        f = dt * jnp.exp2(a_last - a_cs)
        a_src_t = a_cs.T - jnp.log2(dt).T

        cm_bf = cm.astype(BF16)
        cb = [lax.dot_general(cm_bf[:, g * SSD_STATE:(g + 1) * SSD_STATE], bm[:, g * SSD_STATE:(g + 1) * SSD_STATE],
                              (((1,), (1,)), ((), ())), preferred_element_type=F32) for g in range(SSD_GROUPS)]
        xs_bf = xs.astype(BF16)
        y_tiles = []
        for pair in range(n_pairs):
            g = (2 * pair) // heads_per_group
            cm_g = cm[:, g * SSD_STATE:(g + 1) * SSD_STATE]
            sl = slice(pair * LANES, (pair + 1) * LANES)
            x_pair = xs_bf[:, sl]
            prev_pair = state_ref[:, sl].astype(BF16)
            y_pair = None
            for half in range(2):
                hd = 2 * pair + half
                a_col = _lane_bcast(a_cs, hd)
                seg = a_col - a_src_t[hd:hd + 1, :]
                m_h = (cb[g] * jnp.exp2(jnp.where(causal, seg, -jnp.inf))).astype(BF16)
                c_h = (cm_g * jnp.exp2(a_col)).astype(BF16)
                keep = lo if half == 0 else jnp.logical_not(lo)
                rhs = jnp.concatenate([jnp.where(keep, x_pair, zero), jnp.where(keep, prev_pair, zero)], axis=0)
                yh = jnp.dot(jnp.concatenate([m_h, c_h], axis=1), rhs, preferred_element_type=F32)
                y_pair = yh if y_pair is None else y_pair + yh
            y_tiles.append(y_pair)

        f_hi = f.astype(BF16)
        f_lo = (f - f_hi.astype(F32)).astype(BF16)
        f_exp = (jnp.dot(f_hi, expand, preferred_element_type=F32)
                 + jnp.dot(f_lo, expand, preferred_element_type=F32))
        ea_last = jnp.exp2(a_cs[L - 8:L, :])
        for pair in range(n_pairs):
            g = (2 * pair) // heads_per_group
            sl = slice(pair * LANES, (pair + 1) * LANES)
            bm_g = bm[:, g * SSD_STATE:(g + 1) * SSD_STATE]
            xw = (xs[:, sl] * f_exp[:, sl]).astype(BF16)
            new = lax.dot_general(bm_g, xw, (((0,), (0,)), ((), ())), preferred_element_type=F32)
            chunk_decay = _expand_heads(ea_last, pair)[7:8, :]
            state_ref[:, sl] = state_ref[:, sl] * chunk_decay + new

        y = jnp.concatenate(y_tiles, axis=1) + xs * dskip_ref[...]
        gated = y * _silu(z_ref[rows, :].astype(F32))
        for g in range(SSD_GROUPS):
            sl = slice(g * gw, (g + 1) * gw)
            y_ref[rows, sl] = _rms_norm(gated[:, sl], nw_ref[:, sl]).astype(BF16)


def _ssd(z, xbc, dt, conv_w, conv_b, dt_bias, a_log, d_skip_exp, norm_w):
    bsz, s, width = z.shape
    rows = SSD_CHUNK * SSD_CHUNKS_PER_STEP
    cw = xbc.shape[-1]
    blk = lambda n: pl.BlockSpec((None, rows, n), lambda b, c: (b, c, 0))
    halo = pl.BlockSpec((None, CONV_HALO, cw), lambda b, c: (b, jnp.maximum(c * (rows // CONV_HALO) - 1, 0), 0))
    return pl.pallas_call(
        _ssd_kernel,
        grid=(bsz, s // rows),
        in_specs=[blk(width), blk(cw), halo, blk(LANES), _resident(conv_w.shape), _resident((1, cw)),
                  _resident((1, LANES)), _resident((1, LANES)), _resident((1, width)), _resident((1, width))],
        out_specs=blk(width),
        out_shape=jax.ShapeDtypeStruct((bsz, s, width), BF16),
        scratch_shapes=[pltpu.VMEM((SSD_STATE, width), F32)],
        compiler_params=pltpu.CompilerParams(dimension_semantics=("arbitrary", "arbitrary"),
                                             vmem_limit_bytes=VMEM_LIMIT),
        name="ssd",
    )(z, xbc, xbc, dt, conv_w, conv_b, dt_bias, a_log, d_skip_exp, norm_w)


def _attn_kernel(q_ref, k_ref, v_ref, lq1_ref, lk1_ref, lq2_ref, lk2_ref, sw_ref, o_ref, s_ref, acc_ref, qt_ref, *,
                 lambda_init):
    tq = o_ref.shape[0]
    tk = s_ref.shape[2]
    tc = ATT_COL_TILE
    n_ct = tq // tc
    n_q = q_ref.shape[0] // tq
    qi = pl.program_id(2)
    cols = [slice(c * tc, (c + 1) * tc) for c in range(n_ct)]

    def stage_q_t(blk, qslot):
        q = q_ref[pl.ds(pl.multiple_of(blk * tq, tq), tq), :]
        lo = lax.broadcasted_iota(jnp.int32, q.shape, 1) < ATT_QK_DIM
        zero = jnp.zeros_like(q)
        qt_ref[qslot, 0] = jnp.where(lo, q, zero).T
        qt_ref[qslot, 1] = jnp.where(lo, zero, q).T

    def keys(row0):
        return k_ref[pl.ds(pl.multiple_of(row0, tk), tk), :]

    ones_rows = jnp.ones((ATT_SUM_ROWS, tk), BF16)

    def values(row0):
        return jnp.concatenate([v_ref[pl.ds(pl.multiple_of(row0, tk), tk), :].T, ones_rows], axis=0)

    def qk(k, slot, qslot, c):
        for mp in range(2):
            s_ref[slot, mp, :, cols[c]] = jnp.dot(k, qt_ref[qslot, mp, :, cols[c]], preferred_element_type=F32)

    def softmax_pv(v_ext, slot, m, c, mask=None):
        m = [list(mm) for mm in m]
        for mp in range(2):
            s = s_ref[slot, mp, :, cols[c]]
            if mask is not None:
                s = jnp.where(mask, s, -jnp.inf)
            m_old = m[mp][c]
            m_new = jnp.maximum(m_old, jnp.max(s, axis=0, keepdims=True))
            p = jnp.exp2(s - m_new).astype(BF16)
            acc_ref[mp, :, cols[c]] = (jnp.exp2(m_old - m_new) * acc_ref[mp, :, cols[c]]
                                       + jnp.dot(v_ext, p, preferred_element_type=F32))
            m[mp][c] = m_new
        return tuple(tuple(mm) for mm in m)

    q_cur = qi % 2
    q_nxt = 1 - q_cur

    @pl.when(qi == 0)
    def _():
        stage_q_t(0, 0)
        k0 = keys(0)
        for c in range(n_ct):
            qk(k0, 0, 0, c)

    acc_ref[...] = jnp.zeros(acc_ref.shape, F32)
    neg_inf = tuple(jnp.full((1, tc), -jnp.inf, F32) for _ in range(n_ct))

    def pair(i, m):
        row0 = i * tq
        k_next, v_cur = keys(row0 + tk), values(row0)
        for c in range(n_ct):
            qk(k_next, 1, q_cur, c)
            m = softmax_pv(v_cur, 0, m, c)
        k_next, v_cur = keys(row0 + tq), values(row0 + tk)
        for c in range(n_ct):
            qk(k_next, 0, q_cur, c)
            m = softmax_pv(v_cur, 1, m, c)
        return m

    m = lax.fori_loop(0, qi, pair, (neg_inf, neg_inf))

    row0 = qi * tq
    r = lax.broadcasted_iota(jnp.int32, (tk, tc), 0)
    col = lax.broadcasted_iota(jnp.int32, (tk, tc), 1)
    n_diag = tk // tc
    masks = [r <= col + c * tc for c in range(n_diag)]
    stage_q_t(jnp.minimum(qi + 1, n_q - 1), q_nxt)
    k_next, v_cur = keys(row0 + tk), values(row0)
    for c in range(n_ct):
        if c >= n_diag:
            qk(k_next, 1, q_cur, c)
        m = softmax_pv(v_cur, 0, m, c, mask=masks[c] if c < n_diag else None)

    lam = (jnp.exp(jnp.sum(lq1_ref[...] * lk1_ref[...], axis=-1, keepdims=True))
           - jnp.exp(jnp.sum(lq2_ref[...] * lk2_ref[...], axis=-1, keepdims=True)) + lambda_init)

    def finish(c):
        o_t = (acc_ref[0, :ATT_V_DIM, cols[c]] / acc_ref[0, ATT_V_DIM:ATT_V_DIM + 1, cols[c]]
               - lam * (acc_ref[1, :ATT_V_DIM, cols[c]] / acc_ref[1, ATT_V_DIM:ATT_V_DIM + 1, cols[c]]))
        o_ref[cols[c], :] = (_rms_norm(o_t.T, sw_ref[...]) * (1.0 - lambda_init)).astype(BF16)

    k_next, v_cur = keys(0), values(row0 + tk)
    for c in range(n_ct):
        qk(k_next, 0, q_nxt, c)
        if c >= n_diag:
            m = softmax_pv(v_cur, 1, m, c, mask=masks[c - n_diag])
        finish(c)


def _attn(q, k, v, lq1, lk1, lq2, lk2, subln_w, lambda_init):
    bsz, s, width = q.shape
    n_heads = width // ATT_V_DIM
    tq = min(ATT_BLOCK_Q, s)
    o_spec = pl.BlockSpec((None, tq, LANES), lambda b, h, i: (b, i, h))
    head_spec = pl.BlockSpec((None, s, LANES), lambda b, h, i: (b, 0, h))
    small = _resident((1, ATT_QK_DIM))
    return pl.pallas_call(
        functools.partial(_attn_kernel, lambda_init=lambda_init),
        grid=(bsz, n_heads, s // tq),
        in_specs=[head_spec, head_spec, head_spec, small, small, small, small, _resident((1, ATT_V_DIM))],
        out_specs=o_spec,
        out_shape=jax.ShapeDtypeStruct((bsz, s, width), BF16),
        scratch_shapes=[pltpu.VMEM((2, 2, tq // 2, tq), F32),
                        pltpu.VMEM((2, ATT_V_DIM + ATT_SUM_ROWS, tq), F32),
                        pltpu.VMEM((2, 2, LANES, tq), BF16)],
        compiler_params=pltpu.CompilerParams(dimension_semantics=("arbitrary",) * 3, vmem_limit_bytes=VMEM_LIMIT),
        name="attn",
    )(q, k, v, lq1, lk1, lq2, lk2, subln_w)


def _out_kernel(x_ref, ys_ref, ya_ref, wos_ref, woa_ref, nw_ref, wg_ref, wu_ref, wd_ref, o_ref):
    x = (x_ref[...] + jnp.dot(ys_ref[...], wos_ref[...], preferred_element_type=F32)
         + jnp.dot(ya_ref[...], woa_ref[...], preferred_element_type=F32))
    h = _rms_norm(x, nw_ref[...]).astype(BF16)
    o_ref[...] = x + 0.5 * _swiglu(h, wg_ref, wu_ref, wd_ref)


def _out(x2d, y_ssd, y_att, wo_ssd, wo_att, norm_w, wg, wu, wd):
    t, d = x2d.shape
    tm = min(FFN_ROWS, t)
    row = lambda n: pl.BlockSpec((tm, n), lambda i: (i, 0))
    return pl.pallas_call(
        _out_kernel,
        grid=(t // tm,),
        in_specs=[row(d), row(y_ssd.shape[1]), row(y_att.shape[1]), _resident(wo_ssd.shape), _resident(wo_att.shape),
                  _resident((1, d)), _resident(wg.shape), _resident(wu.shape), _resident(wd.shape)],
        out_specs=row(d),
        out_shape=jax.ShapeDtypeStruct((t, d), F32),
        compiler_params=pltpu.CompilerParams(dimension_semantics=("arbitrary",), vmem_limit_bytes=VMEM_LIMIT),
        name="out_ffn",
    )(x2d, y_ssd, y_att, wo_ssd, wo_att, norm_w, wg, wu, wd)


def _pad_lanes(v):
    return jnp.zeros((1, LANES), F32).at[0, :v.shape[0]].set(v.astype(F32))


def kernel(x, ffn1_norm_w, ffn1_w_gate, ffn1_w_up, ffn1_w_down, mix_norm_w, w_in, conv_w, conv_b, dt_bias, a_log,
           d_skip, ssd_norm_w, q_norm_w, k_norm_w, lambda_q1, lambda_k1, lambda_q2, lambda_k2, attn_subln_w, w_out,
           ffn2_norm_w, ffn2_w_gate, ffn2_w_up, ffn2_w_down):
    bsz, s, d = x.shape
    depth = w_in.shape[0]
    ssd_heads = dt_bias.shape[1]
    ssd_width = ssd_heads * SSD_HEAD_DIM
    conv_ch = conv_w.shape[2]
    att_width = w_out.shape[1] - ssd_width
    qk_width = (att_width // ATT_V_DIM) * 2 * ATT_QK_DIM
    sizes = (ssd_width, conv_ch, ssd_heads, qk_width, qk_width, att_width)
    offs = [0]
    for n in sizes:
        offs.append(offs[-1] + n)
    row = lambda v: v.astype(F32).reshape(1, -1)

    x2d = x.reshape(bsz * s, d)
    for l in range(depth):
        lambda_init = 0.8 - 0.6 * math.exp(-0.3 * l)
        w = w_in[l].astype(BF16)
        wz, wxbc, wdt, wq, wk, wv = (w[:, offs[i]:offs[i + 1]] for i in range(6))
        wdt = jnp.pad(wdt, ((0, 0), (0, LANES - ssd_heads)))
        wo = w_out[l].astype(BF16)

        x1 = _ffn(x2d, row(ffn1_norm_w[l]), ffn1_w_gate[l].astype(BF16), ffn1_w_up[l].astype(BF16),
                  ffn1_w_down[l].astype(BF16))
        z, xbc, dt, q, k, v = _proj(x1, row(mix_norm_w[l]), wz, wxbc, wdt, wq, wk, wv,
                                    row(jnp.tile(q_norm_w[l], 2)), row(jnp.tile(k_norm_w[l], 2)))
        y_ssd = _ssd(z.reshape(bsz, s, -1), xbc.reshape(bsz, s, -1), dt.reshape(bsz, s, -1),
                     conv_w[l].astype(F32), row(conv_b[l]), _pad_lanes(dt_bias[l]), _pad_lanes(a_log[l]),
                     row(jnp.repeat(d_skip[l], SSD_HEAD_DIM)), row(ssd_norm_w[l]))
        y_att = _attn(q.reshape(bsz, s, -1), k.reshape(bsz, s, -1), v.reshape(bsz, s, -1),
                      row(lambda_q1[l]), row(lambda_k1[l]), row(lambda_q2[l]), row(lambda_k2[l]),
                      row(attn_subln_w[l]), lambda_init)
        x2d = _out(x1, y_ssd.reshape(bsz * s, -1), y_att.reshape(bsz * s, -1), wo[:ssd_width], wo[ssd_width:],
                   row(ffn2_norm_w[l]), ffn2_w_gate[l].astype(BF16), ffn2_w_up[l].astype(BF16),
                   ffn2_w_down[l].astype(BF16))
    return x2d.reshape(bsz, s, d)
```

```python
import functools
import math

import jax
import jax.numpy as jnp
from jax import lax
from jax.experimental import pallas as pl
from jax.experimental.pallas import tpu as pltpu

F32 = jnp.float32
BF16 = jnp.bfloat16

NORM_EPS = 1e-6
SSD_HEAD_DIM = 64
SSD_GROUPS = 2
SSD_STATE = 128
SSD_CONV = 4
SSD_CHUNK = 128
SSD_CHUNKS_PER_STEP = 2
ATT_QK_DIM = 64
ATT_V_DIM = 128
LANES = 128
CONV_HALO = 16
LOG2_E = math.log2(math.e)
Q_SCALE = ATT_QK_DIM ** -0.5 * LOG2_E
ATT_BOUND_MARGIN = 1.01
ATT_MAX_BOUND = 60.0

FFN_ROWS = 512
FFN_COLS = 256
ATT_BLOCK_Q = 1024
ATT_COL_TILE = 256
ATT_SUM_ROWS = 16
VMEM_LIMIT = 56 * 1024 * 1024


def _resident(shape):
    nd = len(shape)
    return pl.BlockSpec(shape, lambda *_: (0,) * nd, pipeline_mode=pl.Buffered(1))


def _rms_norm(x, w):
    return x * lax.rsqrt(jnp.mean(x * x, axis=-1, keepdims=True) + NORM_EPS) * w


def _silu(x):
    h = 0.5 * x
    return h + h * jnp.tanh(h)


def _swiglu(h, wg_ref, wu_ref, wd_ref):
    d_ff = wg_ref.shape[1]
    acc = None
    for c in range(d_ff // FFN_COLS):
        sl = slice(c * FFN_COLS, (c + 1) * FFN_COLS)
        g = jnp.dot(h, wg_ref[:, sl], preferred_element_type=F32)
        u = jnp.dot(h, wu_ref[:, sl], preferred_element_type=F32)
        a = (_silu(g) * u).astype(BF16)
        d = jnp.dot(a, wd_ref[sl, :], preferred_element_type=F32)
        acc = d if acc is None else acc + d
    return acc


def _ffn_kernel(x_ref, nw_ref, wg_ref, wu_ref, wd_ref, o_ref):
    x = x_ref[...]
    h = _rms_norm(x, nw_ref[...]).astype(BF16)
    o_ref[...] = x + 0.5 * _swiglu(h, wg_ref, wu_ref, wd_ref)


def _ffn(x2d, norm_w, wg, wu, wd):
    t, d = x2d.shape
    tm = min(FFN_ROWS, t)
    row = pl.BlockSpec((tm, d), lambda i: (i, 0))
    return pl.pallas_call(
        _ffn_kernel,
        grid=(t // tm,),
        in_specs=[row, _resident((1, d)), _resident(wg.shape), _resident(wu.shape), _resident(wd.shape)],
        out_specs=row,
        out_shape=jax.ShapeDtypeStruct((t, d), F32),
        compiler_params=pltpu.CompilerParams(dimension_semantics=("arbitrary",), vmem_limit_bytes=VMEM_LIMIT),
        name="ffn",
    )(x2d, norm_w, wg, wu, wd)


def _qk_norm(t, w2, scale):
    lo = lax.broadcasted_iota(jnp.int32, t.shape, 1) < ATT_QK_DIM
    x2 = t * t
    s_lo = jnp.sum(jnp.where(lo, x2, 0.0), axis=-1, keepdims=True)
    s_hi = jnp.sum(jnp.where(lo, 0.0, x2), axis=-1, keepdims=True)
    r = lax.rsqrt(jnp.where(lo, s_lo, s_hi) * (1.0 / ATT_QK_DIM) + NORM_EPS)
    return t * r * (w2 * scale)


def _proj_kernel(x_ref, nw_ref, wz_ref, wxbc_ref, wdt_ref, wq_ref, wk_ref, wv_ref, qw_ref, kw_ref,
                 z_ref, xbc_ref, dt_ref, q_ref, k_ref, v_ref):
    h = _rms_norm(x_ref[...], nw_ref[...]).astype(BF16)
    z_ref[...] = jnp.dot(h, wz_ref[...], preferred_element_type=F32).astype(BF16)
    xbc_ref[...] = jnp.dot(h, wxbc_ref[...], preferred_element_type=F32).astype(BF16)
    dt_ref[...] = jnp.dot(h, wdt_ref[...], preferred_element_type=F32)
    v_ref[...] = jnp.dot(h, wv_ref[...], preferred_element_type=F32).astype(BF16)
    q = jnp.dot(h, wq_ref[...], preferred_element_type=F32)
    k = jnp.dot(h, wk_ref[...], preferred_element_type=F32)
    n_heads = q.shape[1] // LANES
    for hd in range(n_heads):
        sl = slice(hd * LANES, (hd + 1) * LANES)
        q_ref[:, sl] = _qk_norm(q[:, sl], qw_ref[...], Q_SCALE).astype(BF16)
        k_ref[:, sl] = _qk_norm(k[:, sl], kw_ref[...], 1.0).astype(BF16)


def _proj(x2d, norm_w, wz, wxbc, wdt, wq, wk, wv, qw2, kw2):
    t, d = x2d.shape
    tm = min(FFN_ROWS, t)
    row = lambda n: pl.BlockSpec((tm, n), lambda i: (i, 0))
    widths = (wz.shape[1], wxbc.shape[1], wdt.shape[1], wq.shape[1], wk.shape[1], wv.shape[1])
    dtypes = (BF16, BF16, F32, BF16, BF16, BF16)
    return pl.pallas_call(
        _proj_kernel,
        grid=(t // tm,),
        in_specs=[row(d), _resident((1, d))] + [_resident(w.shape) for w in (wz, wxbc, wdt, wq, wk, wv)]
                 + [_resident((1, LANES)), _resident((1, LANES))],
        out_specs=[row(n) for n in widths],
        out_shape=[jax.ShapeDtypeStruct((t, n), dt) for n, dt in zip(widths, dtypes)],
        compiler_params=pltpu.CompilerParams(dimension_semantics=("arbitrary",), vmem_limit_bytes=VMEM_LIMIT),
        name="proj",
    )(x2d, norm_w, wz, wxbc, wdt, wq, wk, wv, qw2, kw2)


def _lane_bcast(x, col):
    return jnp.broadcast_to(x[:, col:col + 1], x.shape)


def _expand_heads(x, pair):
    lo = lax.broadcasted_iota(jnp.int32, x.shape, 1) < SSD_HEAD_DIM
    return jnp.where(lo, _lane_bcast(x, 2 * pair), _lane_bcast(x, 2 * pair + 1))


def _ssd_kernel(z_ref, xbc_ref, halo_ref, dt_ref, cw_ref, cb_ref, dtb_ref, alog_ref, dskip_ref, nw_ref,
                y_ref, state_ref):
    L = SSD_CHUNK
    n_chunks = xbc_ref.shape[0] // L
    width = z_ref.shape[-1]
    n_pairs = width // LANES
    heads_per_group = (width // SSD_HEAD_DIM) // SSD_GROUPS
    gw = width // SSD_GROUPS
    first = pl.program_id(1) == 0

    @pl.when(first)
    def _():
        state_ref[...] = jnp.zeros(state_ref.shape, F32)

    out_t = lax.broadcasted_iota(jnp.int32, (L, CONV_HALO + L), 0)
    in_t = lax.broadcasted_iota(jnp.int32, (L, CONV_HALO + L), 1) - CONV_HALO
    shifts = [jnp.where(in_t == out_t - (SSD_CONV - 1 - kk), 1.0, 0.0).astype(BF16) for kk in range(SSD_CONV - 1)]
    causal = lax.broadcasted_iota(jnp.int32, (L, L), 0) >= lax.broadcasted_iota(jnp.int32, (L, L), 1)
    tril = causal.astype(F32)
    lo = lax.broadcasted_iota(jnp.int32, (L, LANES), 1) < SSD_HEAD_DIM
    zero = jnp.zeros((L, LANES), BF16)
    head_of_lane = lax.broadcasted_iota(jnp.int32, (LANES, width), 1) // SSD_HEAD_DIM
    expand = (lax.broadcasted_iota(jnp.int32, (LANES, width), 0) == head_of_lane).astype(BF16)
    neg_a = -jnp.exp(alog_ref[...])

    halo = halo_ref[...]
    tail = jnp.where(first, jnp.zeros_like(halo), halo)
    for ci in range(n_chunks):
        rows = slice(ci * L, (ci + 1) * L)

        x_cur = xbc_ref[rows, :]
        x_ext = jnp.concatenate([tail, x_cur], axis=0)
        tail = x_cur[L - CONV_HALO:, :]
        conv = cb_ref[...] + cw_ref[SSD_CONV - 1:SSD_CONV, :] * x_cur.astype(F32)
        for kk in range(SSD_CONV - 1):
            conv = conv + cw_ref[kk:kk + 1, :] * jnp.dot(shifts[kk], x_ext, preferred_element_type=F32)
        act = _silu(conv)
        xs = act[:, :width]
        bm = act[:, width:width + SSD_GROUPS * SSD_STATE].astype(BF16)
        cm = act[:, width + SSD_GROUPS * SSD_STATE:]

        raw = dt_ref[rows, :] + dtb_ref[...]
        dt = jnp.maximum(raw, 0.0) + jnp.log1p(jnp.exp(-jnp.abs(raw)))
        a_cs = LOG2_E * jnp.dot(tril, dt * neg_a, preferred_element_type=F32, precision=lax.Precision.HIGHEST)
        a_last = a_cs[L - 1:L, :]
        f = dt * jnp.exp2(a_last - a_cs)
        a_src_t = a_cs.T - jnp.log2(dt).T

        cm_bf = cm.astype(BF16)
        cb = [lax.dot_general(cm_bf[:, g * SSD_STATE:(g + 1) * SSD_STATE], bm[:, g * SSD_STATE:(g + 1) * SSD_STATE],
                              (((1,), (1,)), ((), ())), preferred_element_type=F32) for g in range(SSD_GROUPS)]
        xs_bf = xs.astype(BF16)
        y_tiles = []
        for pair in range(n_pairs):
            g = (2 * pair) // heads_per_group
            cm_g = cm[:, g * SSD_STATE:(g + 1) * SSD_STATE]
            sl = slice(pair * LANES, (pair + 1) * LANES)
            x_pair = xs_bf[:, sl]
            prev_pair = state_ref[:, sl].astype(BF16)
            y_pair = None
            for half in range(2):
                hd = 2 * pair + half
                a_col = _lane_bcast(a_cs, hd)
                seg = a_col - a_src_t[hd:hd + 1, :]
                m_h = (cb[g] * jnp.exp2(jnp.where(causal, seg, -jnp.inf))).astype(BF16)
                c_h = (cm_g * jnp.exp2(a_col)).astype(BF16)
                keep = lo if half == 0 else jnp.logical_not(lo)
                rhs = jnp.concatenate([jnp.where(keep, x_pair, zero), jnp.where(keep, prev_pair, zero)], axis=0)
                yh = jnp.dot(jnp.concatenate([m_h, c_h], axis=1), rhs, preferred_element_type=F32)
                y_pair = yh if y_pair is None else y_pair + yh
            y_tiles.append(y_pair)

        f_hi = f.astype(BF16)
        f_lo = (f - f_hi.astype(F32)).astype(BF16)
        f_exp = (jnp.dot(f_hi, expand, preferred_element_type=F32)
                 + jnp.dot(f_lo, expand, preferred_element_type=F32))
        ea_last = jnp.exp2(a_cs[L - 8:L, :])
        for pair in range(n_pairs):
            g = (2 * pair) // heads_per_group
            sl = slice(pair * LANES, (pair + 1) * LANES)
            bm_g = bm[:, g * SSD_STATE:(g + 1) * SSD_STATE]
            xw = (xs[:, sl] * f_exp[:, sl]).astype(BF16)
            new = lax.dot_general(bm_g, xw, (((0,), (0,)), ((), ())), preferred_element_type=F32)
            chunk_decay = _expand_heads(ea_last, pair)[7:8, :]
            state_ref[:, sl] = state_ref[:, sl] * chunk_decay + new

        y = jnp.concatenate(y_tiles, axis=1) + xs * dskip_ref[...]
        gated = y * _silu(z_ref[rows, :].astype(F32))
        for g in range(SSD_GROUPS):
            sl = slice(g * gw, (g + 1) * gw)
            y_ref[rows, sl] = _rms_norm(gated[:, sl], nw_ref[:, sl]).astype(BF16)


def _ssd(z, xbc, dt, conv_w, conv_b, dt_bias, a_log, d_skip_exp, norm_w):
    bsz, s, width = z.shape
    rows = SSD_CHUNK * SSD_CHUNKS_PER_STEP
    cw = xbc.shape[-1]
    blk = lambda n: pl.BlockSpec((None, rows, n), lambda b, c: (b, c, 0))
    halo = pl.BlockSpec((None, CONV_HALO, cw), lambda b, c: (b, jnp.maximum(c * (rows // CONV_HALO) - 1, 0), 0))
    return pl.pallas_call(
        _ssd_kernel,
        grid=(bsz, s // rows),
        in_specs=[blk(width), blk(cw), halo, blk(LANES), _resident(conv_w.shape), _resident((1, cw)),
                  _resident((1, LANES)), _resident((1, LANES)), _resident((1, width)), _resident((1, width))],
        out_specs=blk(width),
        out_shape=jax.ShapeDtypeStruct((bsz, s, width), BF16),
        scratch_shapes=[pltpu.VMEM((SSD_STATE, width), F32)],
        compiler_params=pltpu.CompilerParams(dimension_semantics=("arbitrary", "arbitrary"),
                                             vmem_limit_bytes=VMEM_LIMIT),
        name="ssd",
    )(z, xbc, xbc, dt, conv_w, conv_b, dt_bias, a_log, d_skip_exp, norm_w)


def _attn_kernel(q_ref, k_ref, v_ref, bound_ref, lq1_ref, lk1_ref, lq2_ref, lk2_ref, sw_ref, o_ref, s_ref, acc_ref,
                 qt_ref, *, lambda_init, bounded):
    tq = o_ref.shape[0]
    tk = s_ref.shape[2]
    tc = ATT_COL_TILE
    n_ct = tq // tc
    n_q = q_ref.shape[0] // tq
    qi = pl.program_id(2)
    cols = [slice(c * tc, (c + 1) * tc) for c in range(n_ct)]

    def stage_q_t(blk, qslot):
        q = q_ref[pl.ds(pl.multiple_of(blk * tq, tq), tq), :]
        lo = lax.broadcasted_iota(jnp.int32, q.shape, 1) < ATT_QK_DIM
        zero = jnp.zeros_like(q)
        qt_ref[qslot, 0] = jnp.where(lo, q, zero).T
        qt_ref[qslot, 1] = jnp.where(lo, zero, q).T

    def keys(row0):
        return k_ref[pl.ds(pl.multiple_of(row0, tk), tk), :]

    ones_rows = jnp.ones((ATT_SUM_ROWS, tk), BF16)

    def values(row0):
        return jnp.concatenate([v_ref[pl.ds(pl.multiple_of(row0, tk), tk), :].T, ones_rows], axis=0)

    def qk(k, slot, qslot, c):
        for mp in range(2):
            s_ref[slot, mp, :, cols[c]] = jnp.dot(k, qt_ref[qslot, mp, :, cols[c]], preferred_element_type=F32)

    def softmax_pv(v_ext, slot, m, c, mask=None):
        m = [list(mm) for mm in m]
        for mp in range(2):
            s = s_ref[slot, mp, :, cols[c]]
            if mask is not None:
                s = jnp.where(mask, s, -jnp.inf)
            if bounded:
                p = jnp.exp2(s - bound_ref[...]).astype(BF16)
                acc_ref[mp, :, cols[c]] = acc_ref[mp, :, cols[c]] + jnp.dot(v_ext, p, preferred_element_type=F32)
                continue
            m_old = m[mp][c]
            m_new = jnp.maximum(m_old, jnp.max(s, axis=0, keepdims=True))
            p = jnp.exp2(s - m_new).astype(BF16)
            acc_ref[mp, :, cols[c]] = (jnp.exp2(m_old - m_new) * acc_ref[mp, :, cols[c]]
                                       + jnp.dot(v_ext, p, preferred_element_type=F32))
            m[mp][c] = m_new
        return tuple(tuple(mm) for mm in m)

    q_cur = qi % 2
    q_nxt = 1 - q_cur

    @pl.when(qi == 0)
    def _():
        stage_q_t(0, 0)
        k0 = keys(0)
        for c in range(n_ct):
            qk(k0, 0, 0, c)

    acc_ref[...] = jnp.zeros(acc_ref.shape, F32)
    neg_inf = () if bounded else tuple(jnp.full((1, tc), -jnp.inf, F32) for _ in range(n_ct))

    def pair(i, m):
        row0 = i * tq
        k_next, v_cur = keys(row0 + tk), values(row0)
        for c in range(n_ct):
            qk(k_next, 1, q_cur, c)
            m = softmax_pv(v_cur, 0, m, c)
        k_next, v_cur = keys(row0 + tq), values(row0 + tk)
        for c in range(n_ct):
            qk(k_next, 0, q_cur, c)
            m = softmax_pv(v_cur, 1, m, c)
        return m

    m = lax.fori_loop(0, qi, pair, (neg_inf, neg_inf))

    row0 = qi * tq
    r = lax.broadcasted_iota(jnp.int32, (tk, tc), 0)
    col = lax.broadcasted_iota(jnp.int32, (tk, tc), 1)
    n_diag = tk // tc
    masks = [r <= col + c * tc for c in range(n_diag)]
    stage_q_t(jnp.minimum(qi + 1, n_q - 1), q_nxt)
    k_next, v_cur = keys(row0 + tk), values(row0)
    for c in range(n_ct):
        if c >= n_diag:
            qk(k_next, 1, q_cur, c)
        m = softmax_pv(v_cur, 0, m, c, mask=masks[c] if c < n_diag else None)

    lam = (jnp.exp(jnp.sum(lq1_ref[...] * lk1_ref[...], axis=-1, keepdims=True))
           - jnp.exp(jnp.sum(lq2_ref[...] * lk2_ref[...], axis=-1, keepdims=True)) + lambda_init)

    def finish(c):
        o_t = (acc_ref[0, :ATT_V_DIM, cols[c]] / acc_ref[0, ATT_V_DIM:ATT_V_DIM + 1, cols[c]]
               - lam * (acc_ref[1, :ATT_V_DIM, cols[c]] / acc_ref[1, ATT_V_DIM:ATT_V_DIM + 1, cols[c]]))
        o_ref[cols[c], :] = (_rms_norm(o_t.T, sw_ref[...]) * (1.0 - lambda_init)).astype(BF16)

    k_next, v_cur = keys(0), values(row0 + tk)
    for c in range(n_ct):
        qk(k_next, 0, q_nxt, c)
        if c >= n_diag:
            m = softmax_pv(v_cur, 1, m, c, mask=masks[c - n_diag])
        finish(c)


def _attn(q, k, v, bound, lq1, lk1, lq2, lk2, subln_w, lambda_init, bounded):
    bsz, s, width = q.shape
    n_heads = width // ATT_V_DIM
    tq = min(ATT_BLOCK_Q, s)
    o_spec = pl.BlockSpec((None, tq, LANES), lambda b, h, i: (b, i, h))
    head_spec = pl.BlockSpec((None, s, LANES), lambda b, h, i: (b, 0, h))
    small = _resident((1, ATT_QK_DIM))
    return pl.pallas_call(
        functools.partial(_attn_kernel, lambda_init=lambda_init, bounded=bounded),
        grid=(bsz, n_heads, s // tq),
        in_specs=[head_spec, head_spec, head_spec, _resident((1, 1)), small, small, small, small,
                  _resident((1, ATT_V_DIM))],
        out_specs=o_spec,
        out_shape=jax.ShapeDtypeStruct((bsz, s, width), BF16),
        scratch_shapes=[pltpu.VMEM((2, 2, tq // 2, tq), F32),
                        pltpu.VMEM((2, ATT_V_DIM + ATT_SUM_ROWS, tq), F32),
                        pltpu.VMEM((2, 2, LANES, tq), BF16)],
        compiler_params=pltpu.CompilerParams(dimension_semantics=("arbitrary",) * 3, vmem_limit_bytes=VMEM_LIMIT),
        name="attn",
    )(q, k, v, bound, lq1, lk1, lq2, lk2, subln_w)


def _out_kernel(x_ref, ys_ref, ya_ref, wos_ref, woa_ref, nw_ref, wg_ref, wu_ref, wd_ref, o_ref):
    x = (x_ref[...] + jnp.dot(ys_ref[...], wos_ref[...], preferred_element_type=F32)
         + jnp.dot(ya_ref[...], woa_ref[...], preferred_element_type=F32))
    h = _rms_norm(x, nw_ref[...]).astype(BF16)
    o_ref[...] = x + 0.5 * _swiglu(h, wg_ref, wu_ref, wd_ref)


def _out(x2d, y_ssd, y_att, wo_ssd, wo_att, norm_w, wg, wu, wd):
    t, d = x2d.shape
    tm = min(FFN_ROWS, t)
    row = lambda n: pl.BlockSpec((tm, n), lambda i: (i, 0))
    return pl.pallas_call(
        _out_kernel,
        grid=(t // tm,),
        in_specs=[row(d), row(y_ssd.shape[1]), row(y_att.shape[1]), _resident(wo_ssd.shape), _resident(wo_att.shape),
                  _resident((1, d)), _resident(wg.shape), _resident(wu.shape), _resident(wd.shape)],
        out_specs=row(d),
        out_shape=jax.ShapeDtypeStruct((t, d), F32),
        compiler_params=pltpu.CompilerParams(dimension_semantics=("arbitrary",), vmem_limit_bytes=VMEM_LIMIT),
        name="out_ffn",
    )(x2d, y_ssd, y_att, wo_ssd, wo_att, norm_w, wg, wu, wd)


def _pad_lanes(v):
    return jnp.zeros((1, LANES), F32).at[0, :v.shape[0]].set(v.astype(F32))


def kernel(x, ffn1_norm_w, ffn1_w_gate, ffn1_w_up, ffn1_w_down, mix_norm_w, w_in, conv_w, conv_b, dt_bias, a_log,
           d_skip, ssd_norm_w, q_norm_w, k_norm_w, lambda_q1, lambda_k1, lambda_q2, lambda_k2, attn_subln_w, w_out,
           ffn2_norm_w, ffn2_w_gate, ffn2_w_up, ffn2_w_down):
    bsz, s, d = x.shape
    depth = w_in.shape[0]
    ssd_heads = dt_bias.shape[1]
    ssd_width = ssd_heads * SSD_HEAD_DIM
    conv_ch = conv_w.shape[2]
    att_width = w_out.shape[1] - ssd_width
    qk_width = (att_width // ATT_V_DIM) * 2 * ATT_QK_DIM
    sizes = (ssd_width, conv_ch, ssd_heads, qk_width, qk_width, att_width)
    offs = [0]
    for n in sizes:
        offs.append(offs[-1] + n)
    row = lambda v: v.astype(F32).reshape(1, -1)

    x2d = x.reshape(bsz * s, d)
    for l in range(depth):
        lambda_init = 0.8 - 0.6 * math.exp(-0.3 * l)
        w = w_in[l].astype(BF16)
        wz, wxbc, wdt, wq, wk, wv = (w[:, offs[i]:offs[i + 1]] for i in range(6))
        wdt = jnp.pad(wdt, ((0, 0), (0, LANES - ssd_heads)))
        wo = w_out[l].astype(BF16)

        x1 = _ffn(x2d, row(ffn1_norm_w[l]), ffn1_w_gate[l].astype(BF16), ffn1_w_up[l].astype(BF16),
                  ffn1_w_down[l].astype(BF16))
        z, xbc, dt, q, k, v = _proj(x1, row(mix_norm_w[l]), wz, wxbc, wdt, wq, wk, wv,
                                    row(jnp.tile(q_norm_w[l], 2)), row(jnp.tile(k_norm_w[l], 2)))
        y_ssd = _ssd(z.reshape(bsz, s, -1), xbc.reshape(bsz, s, -1), dt.reshape(bsz, s, -1),
                     conv_w[l].astype(F32), row(conv_b[l]), _pad_lanes(dt_bias[l]), _pad_lanes(a_log[l]),
                     row(jnp.repeat(d_skip[l], SSD_HEAD_DIM)), row(ssd_norm_w[l]))
        score_bound = (ATT_QK_DIM * Q_SCALE * ATT_BOUND_MARGIN * jnp.max(jnp.abs(q_norm_w[l]))
                       * jnp.max(jnp.abs(k_norm_w[l]))).astype(F32)
        attn_args = (q.reshape(bsz, s, -1), k.reshape(bsz, s, -1), v.reshape(bsz, s, -1), score_bound.reshape(1, 1),
                     row(lambda_q1[l]), row(lambda_k1[l]), row(lambda_q2[l]), row(lambda_k2[l]),
                     row(attn_subln_w[l]))
        y_att = lax.cond(score_bound <= ATT_MAX_BOUND,
                         lambda *a: _attn(*a, lambda_init, True), lambda *a: _attn(*a, lambda_init, False),
                         *attn_args)
        x2d = _out(x1, y_ssd.reshape(bsz * s, -1), y_att.reshape(bsz * s, -1), wo[:ssd_width], wo[ssd_width:],
                   row(ffn2_norm_w[l]), ffn2_w_gate[l].astype(BF16), ffn2_w_up[l].astype(BF16),
                   ffn2_w_down[l].astype(BF16))
    return x2d.reshape(bsz, s, d)
```

```python
import functools
import math

import jax
import jax.numpy as jnp
from jax import lax
from jax.experimental import pallas as pl
from jax.experimental.pallas import tpu as pltpu

F32 = jnp.float32
BF16 = jnp.bfloat16

NORM_EPS = 1e-6
SSD_HEAD_DIM = 64
SSD_GROUPS = 2
SSD_STATE = 128
SSD_CONV = 4
SSD_CHUNK = 128
SSD_CHUNKS_PER_STEP = 2
ATT_QK_DIM = 64
ATT_V_DIM = 128
LANES = 128
CONV_HALO = 16
LOG2_E = math.log2(math.e)
Q_SCALE = ATT_QK_DIM ** -0.5 * LOG2_E
ATT_BOUND_MARGIN = 1.01
ATT_MAX_BOUND = 60.0

FFN_ROWS = 512
FFN_COLS = 256
ATT_BLOCK_Q = 1024
ATT_COL_TILE = 256
ATT_SUM_ROWS = 16
VMEM_LIMIT = 56 * 1024 * 1024


def _resident(shape):
    nd = len(shape)
    return pl.BlockSpec(shape, lambda *_: (0,) * nd, pipeline_mode=pl.Buffered(1))


def _rms_norm(x, w):
    return x * lax.rsqrt(jnp.mean(x * x, axis=-1, keepdims=True) + NORM_EPS) * w


def _silu(x):
    h = 0.5 * x
    return h + h * jnp.tanh(h)


def _swiglu(h, wg_ref, wu_ref, wd_ref):
    d_ff = wg_ref.shape[1]
    acc = None
    for c in range(d_ff // FFN_COLS):
        sl = slice(c * FFN_COLS, (c + 1) * FFN_COLS)
        g = jnp.dot(h, wg_ref[:, sl], preferred_element_type=F32)
        u = jnp.dot(h, wu_ref[:, sl], preferred_element_type=F32)
        a = (_silu(g) * u).astype(BF16)
        d = jnp.dot(a, wd_ref[sl, :], preferred_element_type=F32)
        acc = d if acc is None else acc + d
    return acc


def _ffn_kernel(x_ref, nw_ref, wg_ref, wu_ref, wd_ref, o_ref):
    x = x_ref[...]
    h = _rms_norm(x, nw_ref[...]).astype(BF16)
    o_ref[...] = x + 0.5 * _swiglu(h, wg_ref, wu_ref, wd_ref)


def _ffn(x2d, norm_w, wg, wu, wd):
    t, d = x2d.shape
    tm = min(FFN_ROWS, t)
    row = pl.BlockSpec((tm, d), lambda i: (i, 0))
    return pl.pallas_call(
        _ffn_kernel,
        grid=(t // tm,),
        in_specs=[row, _resident((1, d)), _resident(wg.shape), _resident(wu.shape), _resident(wd.shape)],
        out_specs=row,
        out_shape=jax.ShapeDtypeStruct((t, d), F32),
        compiler_params=pltpu.CompilerParams(dimension_semantics=("arbitrary",), vmem_limit_bytes=VMEM_LIMIT),
        name="ffn",
    )(x2d, norm_w, wg, wu, wd)


def _qk_norm(t, w2, scale):
    lo = lax.broadcasted_iota(jnp.int32, t.shape, 1) < ATT_QK_DIM
    x2 = t * t
    s_lo = jnp.sum(jnp.where(lo, x2, 0.0), axis=-1, keepdims=True)
    s_hi = jnp.sum(jnp.where(lo, 0.0, x2), axis=-1, keepdims=True)
    r = lax.rsqrt(jnp.where(lo, s_lo, s_hi) * (1.0 / ATT_QK_DIM) + NORM_EPS)
    return t * r * (w2 * scale)


def _proj_kernel(x_ref, nw_ref, wz_ref, wxbc_ref, wdt_ref, wq_ref, wk_ref, wv_ref, qw_ref, kw_ref,
                 z_ref, xbc_ref, dt_ref, q_ref, k_ref, v_ref):
    h = _rms_norm(x_ref[...], nw_ref[...]).astype(BF16)
    n_heads = wq_ref.shape[1] // LANES
    for w_ref, g_ref, o_ref, scale in ((wq_ref, qw_ref, q_ref, Q_SCALE), (wk_ref, kw_ref, k_ref, 1.0)):
        t = jnp.dot(h, w_ref[...], preferred_element_type=F32)
        for hd in range(n_heads):
            sl = slice(hd * LANES, (hd + 1) * LANES)
            o_ref[:, sl] = _qk_norm(t[:, sl], g_ref[...], scale).astype(BF16)
    dt_ref[...] = jnp.dot(h, wdt_ref[...], preferred_element_type=F32)
    v_ref[...] = jnp.dot(h, wv_ref[...], preferred_element_type=F32).astype(BF16)
    z_ref[...] = jnp.dot(h, wz_ref[...], preferred_element_type=F32).astype(BF16)
    xbc_ref[...] = jnp.dot(h, wxbc_ref[...], preferred_element_type=F32).astype(BF16)


def _proj(x2d, norm_w, wz, wxbc, wdt, wq, wk, wv, qw2, kw2):
    t, d = x2d.shape
    tm = min(FFN_ROWS, t)
    row = lambda n: pl.BlockSpec((tm, n), lambda i: (i, 0))
    widths = (wz.shape[1], wxbc.shape[1], wdt.shape[1], wq.shape[1], wk.shape[1], wv.shape[1])
    dtypes = (BF16, BF16, F32, BF16, BF16, BF16)
    return pl.pallas_call(
        _proj_kernel,
        grid=(t // tm,),
        in_specs=[row(d), _resident((1, d))] + [_resident(w.shape) for w in (wz, wxbc, wdt, wq, wk, wv)]
                 + [_resident((1, LANES)), _resident((1, LANES))],
        out_specs=[row(n) for n in widths],
        out_shape=[jax.ShapeDtypeStruct((t, n), dt) for n, dt in zip(widths, dtypes)],
        compiler_params=pltpu.CompilerParams(dimension_semantics=("arbitrary",), vmem_limit_bytes=VMEM_LIMIT),
        name="proj",
    )(x2d, norm_w, wz, wxbc, wdt, wq, wk, wv, qw2, kw2)


def _lane_bcast(x, col):
    return jnp.broadcast_to(x[:, col:col + 1], x.shape)


def _expand_heads(x, pair):
    lo = lax.broadcasted_iota(jnp.int32, x.shape, 1) < SSD_HEAD_DIM
    return jnp.where(lo, _lane_bcast(x, 2 * pair), _lane_bcast(x, 2 * pair + 1))


def _ssd_kernel(z_ref, xbc_ref, halo_ref, dt_ref, cw_ref, cb_ref, dtb_ref, alog_ref, dskip_ref, nw_ref,
                y_ref, state_ref):
    L = SSD_CHUNK
    n_chunks = xbc_ref.shape[0] // L
    width = z_ref.shape[-1]
    n_pairs = width // LANES
    heads_per_group = (width // SSD_HEAD_DIM) // SSD_GROUPS
    gw = width // SSD_GROUPS
    first = pl.program_id(1) == 0

    @pl.when(first)
    def _():
        state_ref[...] = jnp.zeros(state_ref.shape, F32)

    out_t = lax.broadcasted_iota(jnp.int32, (L, CONV_HALO + L), 0)
    in_t = lax.broadcasted_iota(jnp.int32, (L, CONV_HALO + L), 1) - CONV_HALO
    shifts = [jnp.where(in_t == out_t - (SSD_CONV - 1 - kk), 1.0, 0.0).astype(BF16) for kk in range(SSD_CONV - 1)]
    causal = lax.broadcasted_iota(jnp.int32, (L, L), 0) >= lax.broadcasted_iota(jnp.int32, (L, L), 1)
    tril = causal.astype(F32)
    lo = lax.broadcasted_iota(jnp.int32, (L, LANES), 1) < SSD_HEAD_DIM
    zero = jnp.zeros((L, LANES), BF16)
    head_of_lane = lax.broadcasted_iota(jnp.int32, (LANES, width), 1) // SSD_HEAD_DIM
    expand = (lax.broadcasted_iota(jnp.int32, (LANES, width), 0) == head_of_lane).astype(BF16)
    neg_a = -jnp.exp(alog_ref[...])

    halo = halo_ref[...]
    tail = jnp.where(first, jnp.zeros_like(halo), halo)
    for ci in range(n_chunks):
        rows = slice(ci * L, (ci + 1) * L)

        x_cur = xbc_ref[rows, :]
        x_ext = jnp.concatenate([tail, x_cur], axis=0)
        tail = x_cur[L - CONV_HALO:, :]
        conv = cb_ref[...] + cw_ref[SSD_CONV - 1:SSD_CONV, :] * x_cur.astype(F32)
        for kk in range(SSD_CONV - 1):
            conv = conv + cw_ref[kk:kk + 1, :] * jnp.dot(shifts[kk], x_ext, preferred_element_type=F32)
        act = _silu(conv)
        xs = act[:, :width]
        bm = act[:, width:width + SSD_GROUPS * SSD_STATE].astype(BF16)
        cm = act[:, width + SSD_GROUPS * SSD_STATE:]

        raw = dt_ref[rows, :] + dtb_ref[...]
        dt = jnp.maximum(raw, 0.0) + jnp.log1p(jnp.exp(-jnp.abs(raw)))
        a_cs = LOG2_E * jnp.dot(tril, dt * neg_a, preferred_element_type=F32, precision=lax.Precision.HIGHEST)
        a_last = a_cs[L - 1:L, :]
        f = dt * jnp.exp2(a_last - a_cs)
        a_src_t = a_cs.T - jnp.log2(dt).T

        cm_bf = cm.astype(BF16)
        cb = [lax.dot_general(cm_bf[:, g * SSD_STATE:(g + 1) * SSD_STATE], bm[:, g * SSD_STATE:(g + 1) * SSD_STATE],
                              (((1,), (1,)), ((), ())), preferred_element_type=F32) for g in range(SSD_GROUPS)]
        xs_bf = xs.astype(BF16)
        y_tiles = []
        for pair in range(n_pairs):
            g = (2 * pair) // heads_per_group
            cm_g = cm[:, g * SSD_STATE:(g + 1) * SSD_STATE]
            sl = slice(pair * LANES, (pair + 1) * LANES)
            x_pair = xs_bf[:, sl]
            prev_pair = state_ref[:, sl].astype(BF16)
            y_pair = None
            for half in range(2):
                hd = 2 * pair + half
                a_col = _lane_bcast(a_cs, hd)
                seg = a_col - a_src_t[hd:hd + 1, :]
                m_h = (cb[g] * jnp.exp2(jnp.where(causal, seg, -jnp.inf))).astype(BF16)
                c_h = (cm_g * jnp.exp2(a_col)).astype(BF16)
                keep = lo if half == 0 else jnp.logical_not(lo)
                rhs = jnp.concatenate([jnp.where(keep, x_pair, zero), jnp.where(keep, prev_pair, zero)], axis=0)
                yh = jnp.dot(jnp.concatenate([m_h, c_h], axis=1), rhs, preferred_element_type=F32)
                y_pair = yh if y_pair is None else y_pair + yh
            y_tiles.append(y_pair)

        f_hi = f.astype(BF16)
        f_lo = (f - f_hi.astype(F32)).astype(BF16)
        f_exp = (jnp.dot(f_hi, expand, preferred_element_type=F32)
                 + jnp.dot(f_lo, expand, preferred_element_type=F32))
        ea_last = jnp.exp2(a_cs[L - 8:L, :])
        for pair in range(n_pairs):
            g = (2 * pair) // heads_per_group
            sl = slice(pair * LANES, (pair + 1) * LANES)
            bm_g = bm[:, g * SSD_STATE:(g + 1) * SSD_STATE]
            xw = (xs[:, sl] * f_exp[:, sl]).astype(BF16)
            new = lax.dot_general(bm_g, xw, (((0,), (0,)), ((), ())), preferred_element_type=F32)
            chunk_decay = _expand_heads(ea_last, pair)[7:8, :]
            state_ref[:, sl] = state_ref[:, sl] * chunk_decay + new

        y = jnp.concatenate(y_tiles, axis=1) + xs * dskip_ref[...]
        gated = y * _silu(z_ref[rows, :].astype(F32))
        for g in range(SSD_GROUPS):
            sl = slice(g * gw, (g + 1) * gw)
            y_ref[rows, sl] = _rms_norm(gated[:, sl], nw_ref[:, sl]).astype(BF16)


def _ssd(z, xbc, dt, conv_w, conv_b, dt_bias, a_log, d_skip_exp, norm_w):
    bsz, s, width = z.shape
    rows = SSD_CHUNK * SSD_CHUNKS_PER_STEP
    cw = xbc.shape[-1]
    blk = lambda n: pl.BlockSpec((None, rows, n), lambda b, c: (b, c, 0))
    halo = pl.BlockSpec((None, CONV_HALO, cw), lambda b, c: (b, jnp.maximum(c * (rows // CONV_HALO) - 1, 0), 0))
    return pl.pallas_call(
        _ssd_kernel,
        grid=(bsz, s // rows),
        in_specs=[blk(width), blk(cw), halo, blk(LANES), _resident(conv_w.shape), _resident((1, cw)),
                  _resident((1, LANES)), _resident((1, LANES)), _resident((1, width)), _resident((1, width))],
        out_specs=blk(width),
        out_shape=jax.ShapeDtypeStruct((bsz, s, width), BF16),
        scratch_shapes=[pltpu.VMEM((SSD_STATE, width), F32)],
        compiler_params=pltpu.CompilerParams(dimension_semantics=("arbitrary", "arbitrary"),
                                             vmem_limit_bytes=VMEM_LIMIT),
        name="ssd",
    )(z, xbc, xbc, dt, conv_w, conv_b, dt_bias, a_log, d_skip_exp, norm_w)


def _attn_kernel(q_ref, k_ref, v_ref, bound_ref, lq1_ref, lk1_ref, lq2_ref, lk2_ref, sw_ref, o_ref, s_ref, acc_ref,
                 qt_ref, *, lambda_init, bounded):
    tq = o_ref.shape[0]
    tk = s_ref.shape[2]
    tc = ATT_COL_TILE
    n_ct = tq // tc
    n_q = q_ref.shape[0] // tq
    qi = pl.program_id(2)
    cols = [slice(c * tc, (c + 1) * tc) for c in range(n_ct)]

    def stage_q_t(blk, qslot):
        q = q_ref[pl.ds(pl.multiple_of(blk * tq, tq), tq), :]
        lo = lax.broadcasted_iota(jnp.int32, q.shape, 1) < ATT_QK_DIM
        zero = jnp.zeros_like(q)
        qt_ref[qslot, 0] = jnp.where(lo, q, zero).T
        qt_ref[qslot, 1] = jnp.where(lo, zero, q).T

    def keys(row0):
        return k_ref[pl.ds(pl.multiple_of(row0, tk), tk), :]

    ones_rows = jnp.ones((ATT_SUM_ROWS, tk), BF16)

    def values(row0):
        return jnp.concatenate([v_ref[pl.ds(pl.multiple_of(row0, tk), tk), :].T, ones_rows], axis=0)

    def qk(k, slot, qslot, c):
        for mp in range(2):
            s_ref[slot, mp, :, cols[c]] = jnp.dot(k, qt_ref[qslot, mp, :, cols[c]], preferred_element_type=F32)

    def softmax_pv(v_ext, slot, m, c, mask=None):
        m = [list(mm) for mm in m]
        for mp in range(2):
            s = s_ref[slot, mp, :, cols[c]]
            if mask is not None:
                s = jnp.where(mask, s, -jnp.inf)
            if bounded:
                p = jnp.exp2(s - bound_ref[...]).astype(BF16)
                acc_ref[mp, :, cols[c]] = acc_ref[mp, :, cols[c]] + jnp.dot(v_ext, p, preferred_element_type=F32)
                continue
            m_old = m[mp][c]
            m_new = jnp.maximum(m_old, jnp.max(s, axis=0, keepdims=True))
            p = jnp.exp2(s - m_new).astype(BF16)
            acc_ref[mp, :, cols[c]] = (jnp.exp2(m_old - m_new) * acc_ref[mp, :, cols[c]]
                                       + jnp.dot(v_ext, p, preferred_element_type=F32))
            m[mp][c] = m_new
        return tuple(tuple(mm) for mm in m)

    q_cur = qi % 2
    q_nxt = 1 - q_cur

    @pl.when(qi == 0)
    def _():
        stage_q_t(0, 0)
        k0 = keys(0)
        for c in range(n_ct):
            qk(k0, 0, 0, c)

    acc_ref[...] = jnp.zeros(acc_ref.shape, F32)
    neg_inf = () if bounded else tuple(jnp.full((1, tc), -jnp.inf, F32) for _ in range(n_ct))

    def pair(i, m):
        row0 = i * tq
        k_next, v_cur = keys(row0 + tk), values(row0)
        for c in range(n_ct):
            qk(k_next, 1, q_cur, c)
            m = softmax_pv(v_cur, 0, m, c)
        k_next, v_cur = keys(row0 + tq), values(row0 + tk)
        for c in range(n_ct):
            qk(k_next, 0, q_cur, c)
            m = softmax_pv(v_cur, 1, m, c)
        return m

    m = lax.fori_loop(0, qi, pair, (neg_inf, neg_inf))

    row0 = qi * tq
    r = lax.broadcasted_iota(jnp.int32, (tk, tc), 0)
    col = lax.broadcasted_iota(jnp.int32, (tk, tc), 1)
    n_diag = tk // tc
    masks = [r <= col + c * tc for c in range(n_diag)]
    stage_q_t(jnp.minimum(qi + 1, n_q - 1), q_nxt)
    k_next, v_cur = keys(row0 + tk), values(row0)
    for c in range(n_ct):
        if c >= n_diag:
            qk(k_next, 1, q_cur, c)
        m = softmax_pv(v_cur, 0, m, c, mask=masks[c] if c < n_diag else None)

    lam = (jnp.exp(jnp.sum(lq1_ref[...] * lk1_ref[...], axis=-1, keepdims=True))
           - jnp.exp(jnp.sum(lq2_ref[...] * lk2_ref[...], axis=-1, keepdims=True)) + lambda_init)

    def finish(c):
        o_t = (acc_ref[0, :ATT_V_DIM, cols[c]] / acc_ref[0, ATT_V_DIM:ATT_V_DIM + 1, cols[c]]
               - lam * (acc_ref[1, :ATT_V_DIM, cols[c]] / acc_ref[1, ATT_V_DIM:ATT_V_DIM + 1, cols[c]]))
        o_ref[cols[c], :] = (_rms_norm(o_t.T, sw_ref[...]) * (1.0 - lambda_init)).astype(BF16)

    k_next, v_cur = keys(0), values(row0 + tk)
    for c in range(n_ct):
        qk(k_next, 0, q_nxt, c)
        if c >= n_diag:
            m = softmax_pv(v_cur, 1, m, c, mask=masks[c - n_diag])
        finish(c)


def _attn(q, k, v, bound, lq1, lk1, lq2, lk2, subln_w, lambda_init, bounded):
    bsz, s, width = q.shape
    n_heads = width // ATT_V_DIM
    tq = min(ATT_BLOCK_Q, s)
    o_spec = pl.BlockSpec((None, tq, LANES), lambda b, h, i: (b, i, h))
    head_spec = pl.BlockSpec((None, s, LANES), lambda b, h, i: (b, 0, h))
    small = _resident((1, ATT_QK_DIM))
    return pl.pallas_call(
        functools.partial(_attn_kernel, lambda_init=lambda_init, bounded=bounded),
        grid=(bsz, n_heads, s // tq),
        in_specs=[head_spec, head_spec, head_spec, _resident((1, 1)), small, small, small, small,
                  _resident((1, ATT_V_DIM))],
        out_specs=o_spec,
        out_shape=jax.ShapeDtypeStruct((bsz, s, width), BF16),
        scratch_shapes=[pltpu.VMEM((2, 2, tq // 2, tq), F32),
                        pltpu.VMEM((2, ATT_V_DIM + ATT_SUM_ROWS, tq), F32),
                        pltpu.VMEM((2, 2, LANES, tq), BF16)],
        compiler_params=pltpu.CompilerParams(dimension_semantics=("arbitrary",) * 3, vmem_limit_bytes=VMEM_LIMIT),
        name="attn",
    )(q, k, v, bound, lq1, lk1, lq2, lk2, subln_w)


def _out_kernel(x_ref, ys_ref, ya_ref, wos_ref, woa_ref, nw_ref, wg_ref, wu_ref, wd_ref, o_ref):
    x = (x_ref[...] + jnp.dot(ys_ref[...], wos_ref[...], preferred_element_type=F32)
         + jnp.dot(ya_ref[...], woa_ref[...], preferred_element_type=F32))
    h = _rms_norm(x, nw_ref[...]).astype(BF16)
    o_ref[...] = x + 0.5 * _swiglu(h, wg_ref, wu_ref, wd_ref)


def _out(x2d, y_ssd, y_att, wo_ssd, wo_att, norm_w, wg, wu, wd):
    t, d = x2d.shape
    tm = min(FFN_ROWS, t)
    row = lambda n: pl.BlockSpec((tm, n), lambda i: (i, 0))
    return pl.pallas_call(
        _out_kernel,
        grid=(t // tm,),
        in_specs=[row(d), row(y_ssd.shape[1]), row(y_att.shape[1]), _resident(wo_ssd.shape), _resident(wo_att.shape),
                  _resident((1, d)), _resident(wg.shape), _resident(wu.shape), _resident(wd.shape)],
        out_specs=row(d),
        out_shape=jax.ShapeDtypeStruct((t, d), F32),
        compiler_params=pltpu.CompilerParams(dimension_semantics=("arbitrary",), vmem_limit_bytes=VMEM_LIMIT),
        name="out_ffn",
    )(x2d, y_ssd, y_att, wo_ssd, wo_att, norm_w, wg, wu, wd)


def _pad_lanes(v):
    return jnp.zeros((1, LANES), F32).at[0, :v.shape[0]].set(v.astype(F32))


def kernel(x, ffn1_norm_w, ffn1_w_gate, ffn1_w_up, ffn1_w_down, mix_norm_w, w_in, conv_w, conv_b, dt_bias, a_log,
           d_skip, ssd_norm_w, q_norm_w, k_norm_w, lambda_q1, lambda_k1, lambda_q2, lambda_k2, attn_subln_w, w_out,
           ffn2_norm_w, ffn2_w_gate, ffn2_w_up, ffn2_w_down):
    bsz, s, d = x.shape
    depth = w_in.shape[0]
    ssd_heads = dt_bias.shape[1]
    ssd_width = ssd_heads * SSD_HEAD_DIM
    conv_ch = conv_w.shape[2]
    att_width = w_out.shape[1] - ssd_width
    qk_width = (att_width // ATT_V_DIM) * 2 * ATT_QK_DIM
    sizes = (ssd_width, conv_ch, ssd_heads, qk_width, qk_width, att_width)
    offs = [0]
    for n in sizes:
        offs.append(offs[-1] + n)
    row = lambda v: v.astype(F32).reshape(1, -1)

    x2d = x.reshape(bsz * s, d)
    for l in range(depth):
        lambda_init = 0.8 - 0.6 * math.exp(-0.3 * l)
        wz, wxbc, wdt, wq, wk, wv = (w_in[l][:, offs[i]:offs[i + 1]].astype(BF16) for i in range(6))
        wdt = jnp.pad(wdt, ((0, 0), (0, LANES - ssd_heads)))
        wo_ssd, wo_att = w_out[l][:ssd_width].astype(BF16), w_out[l][ssd_width:].astype(BF16)

        x1 = _ffn(x2d, row(ffn1_norm_w[l]), ffn1_w_gate[l].astype(BF16), ffn1_w_up[l].astype(BF16),
                  ffn1_w_down[l].astype(BF16))
        z, xbc, dt, q, k, v = _proj(x1, row(mix_norm_w[l]), wz, wxbc, wdt, wq, wk, wv,
                                    row(jnp.tile(q_norm_w[l], 2)), row(jnp.tile(k_norm_w[l], 2)))
        y_ssd = _ssd(z.reshape(bsz, s, -1), xbc.reshape(bsz, s, -1), dt.reshape(bsz, s, -1),
                     conv_w[l].astype(F32), row(conv_b[l]), _pad_lanes(dt_bias[l]), _pad_lanes(a_log[l]),
                     row(jnp.repeat(d_skip[l], SSD_HEAD_DIM)), row(ssd_norm_w[l]))
        score_bound = (ATT_QK_DIM * Q_SCALE * ATT_BOUND_MARGIN * jnp.max(jnp.abs(q_norm_w[l]))
                       * jnp.max(jnp.abs(k_norm_w[l]))).astype(F32)
        attn_args = (q.reshape(bsz, s, -1), k.reshape(bsz, s, -1), v.reshape(bsz, s, -1), score_bound.reshape(1, 1),
                     row(lambda_q1[l]), row(lambda_k1[l]), row(lambda_q2[l]), row(lambda_k2[l]),
                     row(attn_subln_w[l]))
        y_att = lax.cond(score_bound <= ATT_MAX_BOUND,
                         lambda *a: _attn(*a, lambda_init, True), lambda *a: _attn(*a, lambda_init, False),
                         *attn_args)
        x2d = _out(x1, y_ssd.reshape(bsz * s, -1), y_att.reshape(bsz * s, -1), wo_ssd, wo_att,
                   row(ffn2_norm_w[l]), ffn2_w_gate[l].astype(BF16), ffn2_w_up[l].astype(BF16),
                   ffn2_w_down[l].astype(BF16))
    return x2d.reshape(bsz, s, d)
```

```python
import functools
import math

import jax
import jax.numpy as jnp
from jax import lax
from jax.experimental import pallas as pl
from jax.experimental.pallas import tpu as pltpu

F32 = jnp.float32
BF16 = jnp.bfloat16

NORM_EPS = 1e-6
SSD_HEAD_DIM = 64
SSD_GROUPS = 2
SSD_STATE = 128
SSD_CONV = 4
SSD_CHUNK = 128
SSD_CHUNKS_PER_STEP = 2
ATT_QK_DIM = 64
ATT_V_DIM = 128
LANES = 128
CONV_HALO = 16
LOG2_E = math.log2(math.e)
Q_SCALE = ATT_QK_DIM ** -0.5 * LOG2_E
ATT_BOUND_MARGIN = 1.01
ATT_MAX_BOUND = 60.0

FFN_ROWS = 512
FFN_COLS = 256
ATT_BLOCK_Q = 1024
ATT_COL_TILE = 256
ATT_SUM_ROWS = 16
VMEM_LIMIT = 56 * 1024 * 1024


def _resident(shape):
    nd = len(shape)
    return pl.BlockSpec(shape, lambda *_: (0,) * nd, pipeline_mode=pl.Buffered(1))


def _rms_norm(x, w):
    return x * lax.rsqrt(jnp.mean(x * x, axis=-1, keepdims=True) + NORM_EPS) * w


def _silu(x):
    h = 0.5 * x
    return h + h * jnp.tanh(h)


def _swiglu(h, wg_ref, wu_ref, wd_ref):
    d_ff = wg_ref.shape[1]
    acc = None
    for c in range(d_ff // FFN_COLS):
        sl = slice(c * FFN_COLS, (c + 1) * FFN_COLS)
        g = jnp.dot(h, wg_ref[:, sl].astype(BF16), preferred_element_type=F32)
        u = jnp.dot(h, wu_ref[:, sl].astype(BF16), preferred_element_type=F32)
        a = (_silu(g) * u).astype(BF16)
        d = jnp.dot(a, wd_ref[sl, :].astype(BF16), preferred_element_type=F32)
        acc = d if acc is None else acc + d
    return acc


def _ffn_kernel(x_ref, nw_ref, wg_ref, wu_ref, wd_ref, o_ref):
    x = x_ref[...]
    h = _rms_norm(x, nw_ref[...]).astype(BF16)
    o_ref[...] = x + 0.5 * _swiglu(h, wg_ref, wu_ref, wd_ref)


def _ffn(x2d, norm_w, wg, wu, wd):
    t, d = x2d.shape
    tm = min(FFN_ROWS, t)
    row = pl.BlockSpec((tm, d), lambda i: (i, 0))
    return pl.pallas_call(
        _ffn_kernel,
        grid=(t // tm,),
        in_specs=[row, _resident((1, d)), _resident(wg.shape), _resident(wu.shape), _resident(wd.shape)],
        out_specs=row,
        out_shape=jax.ShapeDtypeStruct((t, d), F32),
        compiler_params=pltpu.CompilerParams(dimension_semantics=("arbitrary",), vmem_limit_bytes=VMEM_LIMIT),
        name="ffn",
    )(x2d, norm_w, wg, wu, wd)


def _qk_norm(t, w2, scale):
    lo = lax.broadcasted_iota(jnp.int32, t.shape, 1) < ATT_QK_DIM
    x2 = t * t
    s_lo = jnp.sum(jnp.where(lo, x2, 0.0), axis=-1, keepdims=True)
    s_hi = jnp.sum(jnp.where(lo, 0.0, x2), axis=-1, keepdims=True)
    r = lax.rsqrt(jnp.where(lo, s_lo, s_hi) * (1.0 / ATT_QK_DIM) + NORM_EPS)
    return t * r * (w2 * scale)


def _proj_kernel(x_ref, nw_ref, wz_ref, wxbc_ref, wdt_ref, wq_ref, wk_ref, wv_ref, qw_ref, kw_ref,
                 z_ref, xbc_ref, dt_ref, q_ref, k_ref, v_ref):
    h = _rms_norm(x_ref[...], nw_ref[...]).astype(BF16)
    n_heads = wq_ref.shape[1] // LANES
    for w_ref, g_ref, o_ref, scale in ((wq_ref, qw_ref, q_ref, Q_SCALE), (wk_ref, kw_ref, k_ref, 1.0)):
        t = jnp.dot(h, w_ref[...].astype(BF16), preferred_element_type=F32)
        for hd in range(n_heads):
            sl = slice(hd * LANES, (hd + 1) * LANES)
            o_ref[:, sl] = _qk_norm(t[:, sl], g_ref[...], scale).astype(BF16)
    dt_ref[...] = jnp.dot(h, wdt_ref[...].astype(BF16), preferred_element_type=F32)
    v_ref[...] = jnp.dot(h, wv_ref[...].astype(BF16), preferred_element_type=F32).astype(BF16)
    z_ref[...] = jnp.dot(h, wz_ref[...].astype(BF16), preferred_element_type=F32).astype(BF16)
    xbc_ref[...] = jnp.dot(h, wxbc_ref[...].astype(BF16), preferred_element_type=F32).astype(BF16)


def _proj(x2d, norm_w, wz, wxbc, wdt, wq, wk, wv, qw2, kw2):
    t, d = x2d.shape
    tm = min(FFN_ROWS, t)
    row = lambda n: pl.BlockSpec((tm, n), lambda i: (i, 0))
    widths = (wz.shape[1], wxbc.shape[1], wdt.shape[1], wq.shape[1], wk.shape[1], wv.shape[1])
    dtypes = (BF16, BF16, F32, BF16, BF16, BF16)
    return pl.pallas_call(
        _proj_kernel,
        grid=(t // tm,),
        in_specs=[row(d), _resident((1, d))] + [_resident(w.shape) for w in (wz, wxbc, wdt, wq, wk, wv)]
                 + [_resident((1, LANES)), _resident((1, LANES))],
        out_specs=[row(n) for n in widths],
        out_shape=[jax.ShapeDtypeStruct((t, n), dt) for n, dt in zip(widths, dtypes)],
        compiler_params=pltpu.CompilerParams(dimension_semantics=("arbitrary",), vmem_limit_bytes=VMEM_LIMIT),
        name="proj",
    )(x2d, norm_w, wz, wxbc, wdt, wq, wk, wv, qw2, kw2)


def _lane_bcast(x, col):
    return jnp.broadcast_to(x[:, col:col + 1], x.shape)


def _expand_heads(x, pair):
    lo = lax.broadcasted_iota(jnp.int32, x.shape, 1) < SSD_HEAD_DIM
    return jnp.where(lo, _lane_bcast(x, 2 * pair), _lane_bcast(x, 2 * pair + 1))


def _ssd_kernel(z_ref, xbc_ref, halo_ref, dt_ref, cw_ref, cb_ref, dtb_ref, alog_ref, dskip_ref, nw_ref,
                y_ref, state_ref):
    L = SSD_CHUNK
    n_chunks = xbc_ref.shape[0] // L
    width = z_ref.shape[-1]
    n_pairs = width // LANES
    heads_per_group = (width // SSD_HEAD_DIM) // SSD_GROUPS
    gw = width // SSD_GROUPS
    first = pl.program_id(1) == 0

    @pl.when(first)
    def _():
        state_ref[...] = jnp.zeros(state_ref.shape, F32)

    out_t = lax.broadcasted_iota(jnp.int32, (L, CONV_HALO + L), 0)
    in_t = lax.broadcasted_iota(jnp.int32, (L, CONV_HALO + L), 1) - CONV_HALO
    shifts = [jnp.where(in_t == out_t - (SSD_CONV - 1 - kk), 1.0, 0.0).astype(BF16) for kk in range(SSD_CONV - 1)]
    causal = lax.broadcasted_iota(jnp.int32, (L, L), 0) >= lax.broadcasted_iota(jnp.int32, (L, L), 1)
    tril = causal.astype(F32)
    lo = lax.broadcasted_iota(jnp.int32, (L, LANES), 1) < SSD_HEAD_DIM
    zero = jnp.zeros((L, LANES), BF16)
    head_of_lane = lax.broadcasted_iota(jnp.int32, (LANES, width), 1) // SSD_HEAD_DIM
    expand = (lax.broadcasted_iota(jnp.int32, (LANES, width), 0) == head_of_lane).astype(BF16)
    neg_a = -jnp.exp(alog_ref[...])

    halo = halo_ref[...]
    tail = jnp.where(first, jnp.zeros_like(halo), halo)
    for ci in range(n_chunks):
        rows = slice(ci * L, (ci + 1) * L)

        x_cur = xbc_ref[rows, :]
        x_ext = jnp.concatenate([tail, x_cur], axis=0)
        tail = x_cur[L - CONV_HALO:, :]
        conv = cb_ref[...] + cw_ref[SSD_CONV - 1:SSD_CONV, :] * x_cur.astype(F32)
        for kk in range(SSD_CONV - 1):
            conv = conv + cw_ref[kk:kk + 1, :] * jnp.dot(shifts[kk], x_ext, preferred_element_type=F32)
        act = _silu(conv)
        xs = act[:, :width]
        bm = act[:, width:width + SSD_GROUPS * SSD_STATE].astype(BF16)
        cm = act[:, width + SSD_GROUPS * SSD_STATE:]

        raw = dt_ref[rows, :] + dtb_ref[...]
        dt = jnp.maximum(raw, 0.0) + jnp.log1p(jnp.exp(-jnp.abs(raw)))
        a_cs = LOG2_E * jnp.dot(tril, dt * neg_a, preferred_element_type=F32, precision=lax.Precision.HIGHEST)
        a_last = a_cs[L - 1:L, :]
        f = dt * jnp.exp2(a_last - a_cs)
        a_src_t = a_cs.T - jnp.log2(dt).T

        cm_bf = cm.astype(BF16)
        cb = [lax.dot_general(cm_bf[:, g * SSD_STATE:(g + 1) * SSD_STATE], bm[:, g * SSD_STATE:(g + 1) * SSD_STATE],
                              (((1,), (1,)), ((), ())), preferred_element_type=F32) for g in range(SSD_GROUPS)]
        xs_bf = xs.astype(BF16)
        y_tiles = []
        for pair in range(n_pairs):
            g = (2 * pair) // heads_per_group
            cm_g = cm[:, g * SSD_STATE:(g + 1) * SSD_STATE]
            sl = slice(pair * LANES, (pair + 1) * LANES)
            x_pair = xs_bf[:, sl]
            prev_pair = state_ref[:, sl].astype(BF16)
            y_pair = None
            for half in range(2):
                hd = 2 * pair + half
                a_col = _lane_bcast(a_cs, hd)
                seg = a_col - a_src_t[hd:hd + 1, :]
                m_h = (cb[g] * jnp.exp2(jnp.where(causal, seg, -jnp.inf))).astype(BF16)
                c_h = (cm_g * jnp.exp2(a_col)).astype(BF16)
                keep = lo if half == 0 else jnp.logical_not(lo)
                rhs = jnp.concatenate([jnp.where(keep, x_pair, zero), jnp.where(keep, prev_pair, zero)], axis=0)
                yh = jnp.dot(jnp.concatenate([m_h, c_h], axis=1), rhs, preferred_element_type=F32)
                y_pair = yh if y_pair is None else y_pair + yh
            y_tiles.append(y_pair)

        f_hi = f.astype(BF16)
        f_lo = (f - f_hi.astype(F32)).astype(BF16)
        f_exp = (jnp.dot(f_hi, expand, preferred_element_type=F32)
                 + jnp.dot(f_lo, expand, preferred_element_type=F32))
        ea_last = jnp.exp2(a_cs[L - 8:L, :])
        for pair in range(n_pairs):
            g = (2 * pair) // heads_per_group
            sl = slice(pair * LANES, (pair + 1) * LANES)
            bm_g = bm[:, g * SSD_STATE:(g + 1) * SSD_STATE]
            xw = (xs[:, sl] * f_exp[:, sl]).astype(BF16)
            new = lax.dot_general(bm_g, xw, (((0,), (0,)), ((), ())), preferred_element_type=F32)
            chunk_decay = _expand_heads(ea_last, pair)[7:8, :]
            state_ref[:, sl] = state_ref[:, sl] * chunk_decay + new

        y = jnp.concatenate(y_tiles, axis=1) + xs * dskip_ref[...]
        gated = y * _silu(z_ref[rows, :].astype(F32))
        for g in range(SSD_GROUPS):
            sl = slice(g * gw, (g + 1) * gw)
            y_ref[rows, sl] = _rms_norm(gated[:, sl], nw_ref[:, sl]).astype(BF16)


def _ssd(z, xbc, dt, conv_w, conv_b, dt_bias, a_log, d_skip_exp, norm_w):
    bsz, s, width = z.shape
    rows = SSD_CHUNK * SSD_CHUNKS_PER_STEP
    cw = xbc.shape[-1]
    blk = lambda n: pl.BlockSpec((None, rows, n), lambda b, c: (b, c, 0))
    halo = pl.BlockSpec((None, CONV_HALO, cw), lambda b, c: (b, jnp.maximum(c * (rows // CONV_HALO) - 1, 0), 0))
    return pl.pallas_call(
        _ssd_kernel,
        grid=(bsz, s // rows),
        in_specs=[blk(width), blk(cw), halo, blk(LANES), _resident(conv_w.shape), _resident((1, cw)),
                  _resident((1, LANES)), _resident((1, LANES)), _resident((1, width)), _resident((1, width))],
        out_specs=blk(width),
        out_shape=jax.ShapeDtypeStruct((bsz, s, width), BF16),
        scratch_shapes=[pltpu.VMEM((SSD_STATE, width), F32)],
        compiler_params=pltpu.CompilerParams(dimension_semantics=("arbitrary", "arbitrary"),
                                             vmem_limit_bytes=VMEM_LIMIT),
        name="ssd",
    )(z, xbc, xbc, dt, conv_w, conv_b, dt_bias, a_log, d_skip_exp, norm_w)


def _attn_kernel(q_ref, k_ref, v_ref, bound_ref, lq1_ref, lk1_ref, lq2_ref, lk2_ref, sw_ref, o_ref, s_ref, acc_ref,
                 qt_ref, *, lambda_init, bounded):
    tq = o_ref.shape[0]
    tk = s_ref.shape[2]
    tc = ATT_COL_TILE
    n_ct = tq // tc
    n_q = q_ref.shape[0] // tq
    qi = pl.program_id(2)
    cols = [slice(c * tc, (c + 1) * tc) for c in range(n_ct)]

    def stage_q_t(blk, qslot):
        q = q_ref[pl.ds(pl.multiple_of(blk * tq, tq), tq), :]
        lo = lax.broadcasted_iota(jnp.int32, q.shape, 1) < ATT_QK_DIM
        zero = jnp.zeros_like(q)
        qt_ref[qslot, 0] = jnp.where(lo, q, zero).T
        qt_ref[qslot, 1] = jnp.where(lo, zero, q).T

    def keys(row0):
        return k_ref[pl.ds(pl.multiple_of(row0, tk), tk), :]

    ones_rows = jnp.ones((ATT_SUM_ROWS, tk), BF16)

    def values(row0):
        return jnp.concatenate([v_ref[pl.ds(pl.multiple_of(row0, tk), tk), :].T, ones_rows], axis=0)

    def qk(k, slot, qslot, c):
        for mp in range(2):
            s_ref[slot, mp, :, cols[c]] = jnp.dot(k, qt_ref[qslot, mp, :, cols[c]], preferred_element_type=F32)

    def softmax_pv(v_ext, slot, m, c, mask=None):
        m = [list(mm) for mm in m]
        for mp in range(2):
            s = s_ref[slot, mp, :, cols[c]]
            if mask is not None:
                s = jnp.where(mask, s, -jnp.inf)
            if bounded:
                p = jnp.exp2(s - bound_ref[...]).astype(BF16)
                acc_ref[mp, :, cols[c]] = acc_ref[mp, :, cols[c]] + jnp.dot(v_ext, p, preferred_element_type=F32)
                continue
            m_old = m[mp][c]
            m_new = jnp.maximum(m_old, jnp.max(s, axis=0, keepdims=True))
            p = jnp.exp2(s - m_new).astype(BF16)
            acc_ref[mp, :, cols[c]] = (jnp.exp2(m_old - m_new) * acc_ref[mp, :, cols[c]]
                                       + jnp.dot(v_ext, p, preferred_element_type=F32))
            m[mp][c] = m_new
        return tuple(tuple(mm) for mm in m)

    q_cur = qi % 2
    q_nxt = 1 - q_cur

    @pl.when(qi == 0)
    def _():
        stage_q_t(0, 0)
        k0 = keys(0)
        for c in range(n_ct):
            qk(k0, 0, 0, c)

    acc_ref[...] = jnp.zeros(acc_ref.shape, F32)
    neg_inf = () if bounded else tuple(jnp.full((1, tc), -jnp.inf, F32) for _ in range(n_ct))

    def pair(i, m):
        row0 = i * tq
        k_next, v_cur = keys(row0 + tk), values(row0)
        for c in range(n_ct):
            qk(k_next, 1, q_cur, c)
            m = softmax_pv(v_cur, 0, m, c)
        k_next, v_cur = keys(row0 + tq), values(row0 + tk)
        for c in range(n_ct):
            qk(k_next, 0, q_cur, c)
            m = softmax_pv(v_cur, 1, m, c)
        return m

    m = lax.fori_loop(0, qi, pair, (neg_inf, neg_inf))

    row0 = qi * tq
    r = lax.broadcasted_iota(jnp.int32, (tk, tc), 0)
    col = lax.broadcasted_iota(jnp.int32, (tk, tc), 1)
    n_diag = tk // tc
    masks = [r <= col + c * tc for c in range(n_diag)]
    stage_q_t(jnp.minimum(qi + 1, n_q - 1), q_nxt)
    k_next, v_cur = keys(row0 + tk), values(row0)
    for c in range(n_ct):
        if c >= n_diag:
            qk(k_next, 1, q_cur, c)
        m = softmax_pv(v_cur, 0, m, c, mask=masks[c] if c < n_diag else None)

    lam = (jnp.exp(jnp.sum(lq1_ref[...] * lk1_ref[...], axis=-1, keepdims=True))
           - jnp.exp(jnp.sum(lq2_ref[...] * lk2_ref[...], axis=-1, keepdims=True)) + lambda_init)

    def finish(c):
        o_t = (acc_ref[0, :ATT_V_DIM, cols[c]] / acc_ref[0, ATT_V_DIM:ATT_V_DIM + 1, cols[c]]
               - lam * (acc_ref[1, :ATT_V_DIM, cols[c]] / acc_ref[1, ATT_V_DIM:ATT_V_DIM + 1, cols[c]]))
        o_ref[cols[c], :] = (_rms_norm(o_t.T, sw_ref[...]) * (1.0 - lambda_init)).astype(BF16)

    k_next, v_cur = keys(0), values(row0 + tk)
    for c in range(n_ct):
        qk(k_next, 0, q_nxt, c)
        if c >= n_diag:
            m = softmax_pv(v_cur, 1, m, c, mask=masks[c - n_diag])
        finish(c)


def _attn(q, k, v, bound, lq1, lk1, lq2, lk2, subln_w, lambda_init, bounded):
    bsz, s, width = q.shape
    n_heads = width // ATT_V_DIM
    tq = min(ATT_BLOCK_Q, s)
    o_spec = pl.BlockSpec((None, tq, LANES), lambda b, h, i: (b, i, h))
    head_spec = pl.BlockSpec((None, s, LANES), lambda b, h, i: (b, 0, h))
    small = _resident((1, ATT_QK_DIM))
    return pl.pallas_call(
        functools.partial(_attn_kernel, lambda_init=lambda_init, bounded=bounded),
        grid=(bsz, n_heads, s // tq),
        in_specs=[head_spec, head_spec, head_spec, _resident((1, 1)), small, small, small, small,
                  _resident((1, ATT_V_DIM))],
        out_specs=o_spec,
        out_shape=jax.ShapeDtypeStruct((bsz, s, width), BF16),
        scratch_shapes=[pltpu.VMEM((2, 2, tq // 2, tq), F32),
                        pltpu.VMEM((2, ATT_V_DIM + ATT_SUM_ROWS, tq), F32),
                        pltpu.VMEM((2, 2, LANES, tq), BF16)],
        compiler_params=pltpu.CompilerParams(dimension_semantics=("arbitrary",) * 3, vmem_limit_bytes=VMEM_LIMIT),
        name="attn",
    )(q, k, v, bound, lq1, lk1, lq2, lk2, subln_w)


def _out_kernel(x_ref, ys_ref, ya_ref, wos_ref, woa_ref, nw_ref, wg_ref, wu_ref, wd_ref, o_ref):
    x = (x_ref[...] + jnp.dot(ys_ref[...], wos_ref[...], preferred_element_type=F32)
         + jnp.dot(ya_ref[...], woa_ref[...], preferred_element_type=F32))
    h = _rms_norm(x, nw_ref[...]).astype(BF16)
    o_ref[...] = x + 0.5 * _swiglu(h, wg_ref, wu_ref, wd_ref)


def _out(x2d, y_ssd, y_att, wo_ssd, wo_att, norm_w, wg, wu, wd):
    t, d = x2d.shape
    tm = min(FFN_ROWS, t)
    row = lambda n: pl.BlockSpec((tm, n), lambda i: (i, 0))
    return pl.pallas_call(
        _out_kernel,
        grid=(t // tm,),
        in_specs=[row(d), row(y_ssd.shape[1]), row(y_att.shape[1]), _resident(wo_ssd.shape), _resident(wo_att.shape),
                  _resident((1, d)), _resident(wg.shape), _resident(wu.shape), _resident(wd.shape)],
        out_specs=row(d),
        out_shape=jax.ShapeDtypeStruct((t, d), F32),
        compiler_params=pltpu.CompilerParams(dimension_semantics=("arbitrary",), vmem_limit_bytes=VMEM_LIMIT),
        name="out_ffn",
    )(x2d, y_ssd, y_att, wo_ssd, wo_att, norm_w, wg, wu, wd)


def _pad_lanes(v):
    return jnp.zeros((1, LANES), F32).at[0, :v.shape[0]].set(v.astype(F32))


def kernel(x, ffn1_norm_w, ffn1_w_gate, ffn1_w_up, ffn1_w_down, mix_norm_w, w_in, conv_w, conv_b, dt_bias, a_log,
           d_skip, ssd_norm_w, q_norm_w, k_norm_w, lambda_q1, lambda_k1, lambda_q2, lambda_k2, attn_subln_w, w_out,
           ffn2_norm_w, ffn2_w_gate, ffn2_w_up, ffn2_w_down):
    bsz, s, d = x.shape
    depth = w_in.shape[0]
    ssd_heads = dt_bias.shape[1]
    ssd_width = ssd_heads * SSD_HEAD_DIM
    conv_ch = conv_w.shape[2]
    att_width = w_out.shape[1] - ssd_width
    qk_width = (att_width // ATT_V_DIM) * 2 * ATT_QK_DIM
    sizes = (ssd_width, conv_ch, ssd_heads, qk_width, qk_width, att_width)
    offs = [0]
    for n in sizes:
        offs.append(offs[-1] + n)
    row = lambda v: v.astype(F32).reshape(1, -1)

    x2d = x.reshape(bsz * s, d)
    for l in range(depth):
        lambda_init = 0.8 - 0.6 * math.exp(-0.3 * l)
        wz, wxbc, wdt, wq, wk, wv = (w_in[l][:, offs[i]:offs[i + 1]] for i in range(6))
        wdt = jnp.pad(wdt, ((0, 0), (0, LANES - ssd_heads)))
        wo_ssd, wo_att = w_out[l][:ssd_width].astype(BF16), w_out[l][ssd_width:].astype(BF16)

        x1 = _ffn(x2d, row(ffn1_norm_w[l]), ffn1_w_gate[l], ffn1_w_up[l], ffn1_w_down[l])
        z, xbc, dt, q, k, v = _proj(x1, row(mix_norm_w[l]), wz, wxbc, wdt, wq, wk, wv,
                                    row(jnp.tile(q_norm_w[l], 2)), row(jnp.tile(k_norm_w[l], 2)))
        y_ssd = _ssd(z.reshape(bsz, s, -1), xbc.reshape(bsz, s, -1), dt.reshape(bsz, s, -1),
                     conv_w[l].astype(F32), row(conv_b[l]), _pad_lanes(dt_bias[l]), _pad_lanes(a_log[l]),
                     row(jnp.repeat(d_skip[l], SSD_HEAD_DIM)), row(ssd_norm_w[l]))
        score_bound = (ATT_QK_DIM * Q_SCALE * ATT_BOUND_MARGIN * jnp.max(jnp.abs(q_norm_w[l]))
                       * jnp.max(jnp.abs(k_norm_w[l]))).astype(F32)
        attn_args = (q.reshape(bsz, s, -1), k.reshape(bsz, s, -1), v.reshape(bsz, s, -1), score_bound.reshape(1, 1),
                     row(lambda_q1[l]), row(lambda_k1[l]), row(lambda_q2[l]), row(lambda_k2[l]),
                     row(attn_subln_w[l]))
        y_att = lax.cond(score_bound <= ATT_MAX_BOUND,
                         lambda *a: _attn(*a, lambda_init, True), lambda *a: _attn(*a, lambda_init, False),
                         *attn_args)
        x2d = _out(x1, y_ssd.reshape(bsz * s, -1), y_att.reshape(bsz * s, -1), wo_ssd, wo_att,
                   row(ffn2_norm_w[l]), ffn2_w_gate[l], ffn2_w_up[l], ffn2_w_down[l])
    return x2d.reshape(bsz, s, d)
```

```python
import functools
import math

import jax
import jax.numpy as jnp
from jax import lax
from jax.experimental import pallas as pl
from jax.experimental.pallas import tpu as pltpu

F32 = jnp.float32
BF16 = jnp.bfloat16

NORM_EPS = 1e-6
SSD_HEAD_DIM = 64
SSD_GROUPS = 2
SSD_STATE = 128
SSD_CONV = 4
SSD_CHUNK = 128
SSD_CHUNKS_PER_STEP = 2
ATT_QK_DIM = 64
ATT_V_DIM = 128
LANES = 128
CONV_HALO = 16
LOG2_E = math.log2(math.e)
Q_SCALE = ATT_QK_DIM ** -0.5 * LOG2_E
ATT_BOUND_MARGIN = 1.01
ATT_MAX_BOUND = 60.0

FFN_ROWS = 512
FFN_COLS = 256
ATT_BLOCK_Q = 2048
ATT_BLOCK_K = 512
ATT_COL_TILE = 256
ATT_SUM_ROWS = 16
VMEM_LIMIT = 56 * 1024 * 1024


def _resident(shape):
    nd = len(shape)
    return pl.BlockSpec(shape, lambda *_: (0,) * nd, pipeline_mode=pl.Buffered(1))


def _rms_norm(x, w):
    return x * lax.rsqrt(jnp.mean(x * x, axis=-1, keepdims=True) + NORM_EPS) * w


def _silu(x):
    h = 0.5 * x
    return h + h * jnp.tanh(h)


def _swiglu(h, wg_ref, wu_ref, wd_ref):
    d_ff = wg_ref.shape[1]
    acc = None
    for c in range(d_ff // FFN_COLS):
        sl = slice(c * FFN_COLS, (c + 1) * FFN_COLS)
        g = jnp.dot(h, wg_ref[:, sl].astype(BF16), preferred_element_type=F32)
        u = jnp.dot(h, wu_ref[:, sl].astype(BF16), preferred_element_type=F32)
        a = (_silu(g) * u).astype(BF16)
        d = jnp.dot(a, wd_ref[sl, :].astype(BF16), preferred_element_type=F32)
        acc = d if acc is None else acc + d
    return acc


def _ffn_kernel(x_ref, nw_ref, wg_ref, wu_ref, wd_ref, o_ref):
    x = x_ref[...]
    h = _rms_norm(x, nw_ref[...]).astype(BF16)
    o_ref[...] = x + 0.5 * _swiglu(h, wg_ref, wu_ref, wd_ref)


def _ffn(x2d, norm_w, wg, wu, wd):
    t, d = x2d.shape
    tm = min(FFN_ROWS, t)
    row = pl.BlockSpec((tm, d), lambda i: (i, 0))
    return pl.pallas_call(
        _ffn_kernel,
        grid=(t // tm,),
        in_specs=[row, _resident((1, d)), _resident(wg.shape), _resident(wu.shape), _resident(wd.shape)],
        out_specs=row,
        out_shape=jax.ShapeDtypeStruct((t, d), F32),
        compiler_params=pltpu.CompilerParams(dimension_semantics=("arbitrary",), vmem_limit_bytes=VMEM_LIMIT),
        name="ffn",
    )(x2d, norm_w, wg, wu, wd)


def _qk_norm(t, w2, scale):
    lo = lax.broadcasted_iota(jnp.int32, t.shape, 1) < ATT_QK_DIM
    x2 = t * t
    s_lo = jnp.sum(jnp.where(lo, x2, 0.0), axis=-1, keepdims=True)
    s_hi = jnp.sum(jnp.where(lo, 0.0, x2), axis=-1, keepdims=True)
    r = lax.rsqrt(jnp.where(lo, s_lo, s_hi) * (1.0 / ATT_QK_DIM) + NORM_EPS)
    return t * r * (w2 * scale)


def _proj_kernel(x_ref, nw_ref, wz_ref, wxbc_ref, wdt_ref, wq_ref, wk_ref, wv_ref, qw_ref, kw_ref,
                 z_ref, xbc_ref, dt_ref, q_ref, k_ref, v_ref):
    h = _rms_norm(x_ref[...], nw_ref[...]).astype(BF16)
    n_heads = wq_ref.shape[1] // LANES
    for w_ref, g_ref, o_ref, scale in ((wq_ref, qw_ref, q_ref, Q_SCALE), (wk_ref, kw_ref, k_ref, 1.0)):
        t = jnp.dot(h, w_ref[...].astype(BF16), preferred_element_type=F32)
        for hd in range(n_heads):
            sl = slice(hd * LANES, (hd + 1) * LANES)
            o_ref[:, sl] = _qk_norm(t[:, sl], g_ref[...], scale).astype(BF16)
    dt_ref[...] = jnp.dot(h, wdt_ref[...].astype(BF16), preferred_element_type=F32)
    v_ref[...] = jnp.dot(h, wv_ref[...].astype(BF16), preferred_element_type=F32).astype(BF16)
    z_ref[...] = jnp.dot(h, wz_ref[...].astype(BF16), preferred_element_type=F32).astype(BF16)
    xbc_ref[...] = jnp.dot(h, wxbc_ref[...].astype(BF16), preferred_element_type=F32).astype(BF16)


def _proj(x2d, norm_w, wz, wxbc, wdt, wq, wk, wv, qw2, kw2):
    t, d = x2d.shape
    tm = min(FFN_ROWS, t)
    row = lambda n: pl.BlockSpec((tm, n), lambda i: (i, 0))
    widths = (wz.shape[1], wxbc.shape[1], wdt.shape[1], wq.shape[1], wk.shape[1], wv.shape[1])
    dtypes = (BF16, BF16, F32, BF16, BF16, BF16)
    return pl.pallas_call(
        _proj_kernel,
        grid=(t // tm,),
        in_specs=[row(d), _resident((1, d))] + [_resident(w.shape) for w in (wz, wxbc, wdt, wq, wk, wv)]
                 + [_resident((1, LANES)), _resident((1, LANES))],
        out_specs=[row(n) for n in widths],
        out_shape=[jax.ShapeDtypeStruct((t, n), dt) for n, dt in zip(widths, dtypes)],
        compiler_params=pltpu.CompilerParams(dimension_semantics=("arbitrary",), vmem_limit_bytes=VMEM_LIMIT),
        name="proj",
    )(x2d, norm_w, wz, wxbc, wdt, wq, wk, wv, qw2, kw2)


def _lane_bcast(x, col):
    return jnp.broadcast_to(x[:, col:col + 1], x.shape)


def _expand_heads(x, pair):
    lo = lax.broadcasted_iota(jnp.int32, x.shape, 1) < SSD_HEAD_DIM
    return jnp.where(lo, _lane_bcast(x, 2 * pair), _lane_bcast(x, 2 * pair + 1))


def _ssd_kernel(z_ref, xbc_ref, halo_ref, dt_ref, cw_ref, cb_ref, dtb_ref, alog_ref, dskip_ref, nw_ref,
                y_ref, state_ref):
    L = SSD_CHUNK
    n_chunks = xbc_ref.shape[0] // L
    width = z_ref.shape[-1]
    n_pairs = width // LANES
    heads_per_group = (width // SSD_HEAD_DIM) // SSD_GROUPS
    gw = width // SSD_GROUPS
    first = pl.program_id(1) == 0

    @pl.when(first)
    def _():
        state_ref[...] = jnp.zeros(state_ref.shape, F32)

    out_t = lax.broadcasted_iota(jnp.int32, (L, CONV_HALO + L), 0)
    in_t = lax.broadcasted_iota(jnp.int32, (L, CONV_HALO + L), 1) - CONV_HALO
    shifts = [jnp.where(in_t == out_t - (SSD_CONV - 1 - kk), 1.0, 0.0).astype(BF16) for kk in range(SSD_CONV - 1)]
    causal = lax.broadcasted_iota(jnp.int32, (L, L), 0) >= lax.broadcasted_iota(jnp.int32, (L, L), 1)
    tril = causal.astype(F32)
    lo = lax.broadcasted_iota(jnp.int32, (L, LANES), 1) < SSD_HEAD_DIM
    zero = jnp.zeros((L, LANES), BF16)
    head_of_lane = lax.broadcasted_iota(jnp.int32, (LANES, width), 1) // SSD_HEAD_DIM
    expand = (lax.broadcasted_iota(jnp.int32, (LANES, width), 0) == head_of_lane).astype(BF16)
    neg_a = -jnp.exp(alog_ref[...])

    halo = halo_ref[...]
    tail = jnp.where(first, jnp.zeros_like(halo), halo)
    for ci in range(n_chunks):
        rows = slice(ci * L, (ci + 1) * L)

        x_cur = xbc_ref[rows, :]
        x_ext = jnp.concatenate([tail, x_cur], axis=0)
        tail = x_cur[L - CONV_HALO:, :]
        conv = cb_ref[...] + cw_ref[SSD_CONV - 1:SSD_CONV, :] * x_cur.astype(F32)
        for kk in range(SSD_CONV - 1):
            conv = conv + cw_ref[kk:kk + 1, :] * jnp.dot(shifts[kk], x_ext, preferred_element_type=F32)
        act = _silu(conv)
        xs = act[:, :width]
        bm = act[:, width:width + SSD_GROUPS * SSD_STATE].astype(BF16)
        cm = act[:, width + SSD_GROUPS * SSD_STATE:]

        raw = dt_ref[rows, :] + dtb_ref[...]
        dt = jnp.maximum(raw, 0.0) + jnp.log1p(jnp.exp(-jnp.abs(raw)))
        a_cs = LOG2_E * jnp.dot(tril, dt * neg_a, preferred_element_type=F32, precision=lax.Precision.HIGHEST)
        a_last = a_cs[L - 1:L, :]
        f = dt * jnp.exp2(a_last - a_cs)
        a_src_t = a_cs.T - jnp.log2(dt).T

        cm_bf = cm.astype(BF16)
        cb = [lax.dot_general(cm_bf[:, g * SSD_STATE:(g + 1) * SSD_STATE], bm[:, g * SSD_STATE:(g + 1) * SSD_STATE],
                              (((1,), (1,)), ((), ())), preferred_element_type=F32) for g in range(SSD_GROUPS)]
        xs_bf = xs.astype(BF16)
        y_tiles = []
        for pair in range(n_pairs):
            g = (2 * pair) // heads_per_group
            cm_g = cm[:, g * SSD_STATE:(g + 1) * SSD_STATE]
            sl = slice(pair * LANES, (pair + 1) * LANES)
            x_pair = xs_bf[:, sl]
            prev_pair = state_ref[:, sl].astype(BF16)
            y_pair = None
            for half in range(2):
                hd = 2 * pair + half
                a_col = _lane_bcast(a_cs, hd)
                seg = a_col - a_src_t[hd:hd + 1, :]
                m_h = (cb[g] * jnp.exp2(jnp.where(causal, seg, -jnp.inf))).astype(BF16)
                c_h = (cm_g * jnp.exp2(a_col)).astype(BF16)
                keep = lo if half == 0 else jnp.logical_not(lo)
                rhs = jnp.concatenate([jnp.where(keep, x_pair, zero), jnp.where(keep, prev_pair, zero)], axis=0)
                yh = jnp.dot(jnp.concatenate([m_h, c_h], axis=1), rhs, preferred_element_type=F32)
                y_pair = yh if y_pair is None else y_pair + yh
            y_tiles.append(y_pair)

        f_hi = f.astype(BF16)
        f_lo = (f - f_hi.astype(F32)).astype(BF16)
        f_exp = (jnp.dot(f_hi, expand, preferred_element_type=F32)
                 + jnp.dot(f_lo, expand, preferred_element_type=F32))
        ea_last = jnp.exp2(a_cs[L - 8:L, :])
        for pair in range(n_pairs):
            g = (2 * pair) // heads_per_group
            sl = slice(pair * LANES, (pair + 1) * LANES)
            bm_g = bm[:, g * SSD_STATE:(g + 1) * SSD_STATE]
            xw = (xs[:, sl] * f_exp[:, sl]).astype(BF16)
            new = lax.dot_general(bm_g, xw, (((0,), (0,)), ((), ())), preferred_element_type=F32)
            chunk_decay = _expand_heads(ea_last, pair)[7:8, :]
            state_ref[:, sl] = state_ref[:, sl] * chunk_decay + new

        y = jnp.concatenate(y_tiles, axis=1) + xs * dskip_ref[...]
        gated = y * _silu(z_ref[rows, :].astype(F32))
        for g in range(SSD_GROUPS):
            sl = slice(g * gw, (g + 1) * gw)
            y_ref[rows, sl] = _rms_norm(gated[:, sl], nw_ref[:, sl]).astype(BF16)


def _ssd(z, xbc, dt, conv_w, conv_b, dt_bias, a_log, d_skip_exp, norm_w):
    bsz, s, width = z.shape
    rows = SSD_CHUNK * SSD_CHUNKS_PER_STEP
    cw = xbc.shape[-1]
    blk = lambda n: pl.BlockSpec((None, rows, n), lambda b, c: (b, c, 0))
    halo = pl.BlockSpec((None, CONV_HALO, cw), lambda b, c: (b, jnp.maximum(c * (rows // CONV_HALO) - 1, 0), 0))
    return pl.pallas_call(
        _ssd_kernel,
        grid=(bsz, s // rows),
        in_specs=[blk(width), blk(cw), halo, blk(LANES), _resident(conv_w.shape), _resident((1, cw)),
                  _resident((1, LANES)), _resident((1, LANES)), _resident((1, width)), _resident((1, width))],
        out_specs=blk(width),
        out_shape=jax.ShapeDtypeStruct((bsz, s, width), BF16),
        scratch_shapes=[pltpu.VMEM((SSD_STATE, width), F32)],
        compiler_params=pltpu.CompilerParams(dimension_semantics=("arbitrary", "arbitrary"),
                                             vmem_limit_bytes=VMEM_LIMIT),
        name="ssd",
    )(z, xbc, xbc, dt, conv_w, conv_b, dt_bias, a_log, d_skip_exp, norm_w)


def _attn_kernel(q_ref, k_ref, v_ref, bound_ref, lq1_ref, lk1_ref, lq2_ref, lk2_ref, sw_ref, o_ref, s_ref, acc_ref,
                 qt_ref, *, lambda_init, bounded):
    tq = o_ref.shape[0]
    tk = s_ref.shape[2]
    n_kb = tq // tk
    tc = ATT_COL_TILE
    n_ct = tq // tc
    n_q = q_ref.shape[0] // tq
    qi = pl.program_id(2)
    cols = [slice(c * tc, (c + 1) * tc) for c in range(n_ct)]

    def stage_q_t(blk, qslot):
        q = q_ref[pl.ds(pl.multiple_of(blk * tq, tq), tq), :]
        lo = lax.broadcasted_iota(jnp.int32, q.shape, 1) < ATT_QK_DIM
        zero = jnp.zeros_like(q)
        qt_ref[qslot, 0] = jnp.where(lo, q, zero).T
        qt_ref[qslot, 1] = jnp.where(lo, zero, q).T

    def keys(row0):
        return k_ref[pl.ds(pl.multiple_of(row0, tk), tk), :]

    ones_rows = jnp.ones((ATT_SUM_ROWS, tk), BF16)

    def values(row0):
        return jnp.concatenate([v_ref[pl.ds(pl.multiple_of(row0, tk), tk), :].T, ones_rows], axis=0)

    def qk(k, slot, qslot, c):
        for mp in range(2):
            s_ref[slot, mp, :, cols[c]] = jnp.dot(k, qt_ref[qslot, mp, :, cols[c]], preferred_element_type=F32)

    def softmax_pv(v_ext, slot, m, c, mask=None):
        m = [list(mm) for mm in m]
        for mp in range(2):
            s = s_ref[slot, mp, :, cols[c]]
            if mask is not None:
                s = jnp.where(mask, s, -jnp.inf)
            if bounded:
                p = jnp.exp2(s - bound_ref[...]).astype(BF16)
                acc_ref[mp, :, cols[c]] = acc_ref[mp, :, cols[c]] + jnp.dot(v_ext, p, preferred_element_type=F32)
                continue
            m_old = m[mp][c]
            m_new = jnp.maximum(m_old, jnp.max(s, axis=0, keepdims=True))
            p = jnp.exp2(s - m_new).astype(BF16)
            acc_ref[mp, :, cols[c]] = (jnp.exp2(m_old - m_new) * acc_ref[mp, :, cols[c]]
                                       + jnp.dot(v_ext, p, preferred_element_type=F32))
            m[mp][c] = m_new
        return tuple(tuple(mm) for mm in m)

    q_cur = qi % 2
    q_nxt = 1 - q_cur

    @pl.when(qi == 0)
    def _():
        stage_q_t(0, 0)
        k0 = keys(0)
        for c in range(n_ct):
            qk(k0, 0, 0, c)

    acc_ref[...] = jnp.zeros(acc_ref.shape, F32)
    neg_inf = () if bounded else tuple(jnp.full((1, tc), -jnp.inf, F32) for _ in range(n_ct))

    def pair(i, m):
        row0 = i * 2 * tk
        k_next, v_cur = keys(row0 + tk), values(row0)
        for c in range(n_ct):
            qk(k_next, 1, q_cur, c)
            m = softmax_pv(v_cur, 0, m, c)
        k_next, v_cur = keys(row0 + 2 * tk), values(row0 + tk)
        for c in range(n_ct):
            qk(k_next, 0, q_cur, c)
            m = softmax_pv(v_cur, 1, m, c)
        return m

    m = lax.fori_loop(0, qi * (n_kb // 2), pair, (neg_inf, neg_inf))

    row0 = qi * tq
    r = lax.broadcasted_iota(jnp.int32, (tk, tc), 0)
    col = lax.broadcasted_iota(jnp.int32, (tk, tc), 1)
    n_diag = tk // tc
    masks = [r <= col + j * tc for j in range(n_diag)]
    stage_q_t(jnp.minimum(qi + 1, n_q - 1), q_nxt)

    lam = (jnp.exp(jnp.sum(lq1_ref[...] * lk1_ref[...], axis=-1, keepdims=True))
           - jnp.exp(jnp.sum(lq2_ref[...] * lk2_ref[...], axis=-1, keepdims=True)) + lambda_init)

    def finish(c):
        o_t = (acc_ref[0, :ATT_V_DIM, cols[c]] / acc_ref[0, ATT_V_DIM:ATT_V_DIM + 1, cols[c]]
               - lam * (acc_ref[1, :ATT_V_DIM, cols[c]] / acc_ref[1, ATT_V_DIM:ATT_V_DIM + 1, cols[c]]))
        o_ref[cols[c], :] = (_rms_norm(o_t.T, sw_ref[...]) * (1.0 - lambda_init)).astype(BF16)

    for d in range(n_kb):
        slot = d % 2
        last = d == n_kb - 1
        v_cur = values(row0 + d * tk)
        k_next = keys(0) if last else keys(row0 + (d + 1) * tk)
        for c in range(n_ct):
            if last:
                qk(k_next, 0, q_nxt, c)
            elif c >= (d + 1) * n_diag:
                qk(k_next, 1 - slot, q_cur, c)
            if c >= d * n_diag:
                m = softmax_pv(v_cur, slot, m, c, mask=masks[c - d * n_diag] if c < (d + 1) * n_diag else None)
            if c // n_diag == d:
                finish(c)


def _attn(q, k, v, bound, lq1, lk1, lq2, lk2, subln_w, lambda_init, bounded):
    bsz, s, width = q.shape
    n_heads = width // ATT_V_DIM
    tq = min(ATT_BLOCK_Q, s)
    o_spec = pl.BlockSpec((None, tq, LANES), lambda b, h, i: (b, i, h))
    head_spec = pl.BlockSpec((None, s, LANES), lambda b, h, i: (b, 0, h))
    small = _resident((1, ATT_QK_DIM))
    return pl.pallas_call(
        functools.partial(_attn_kernel, lambda_init=lambda_init, bounded=bounded),
        grid=(bsz, n_heads, s // tq),
        in_specs=[head_spec, head_spec, head_spec, _resident((1, 1)), small, small, small, small,
                  _resident((1, ATT_V_DIM))],
        out_specs=o_spec,
        out_shape=jax.ShapeDtypeStruct((bsz, s, width), BF16),
        scratch_shapes=[pltpu.VMEM((2, 2, min(ATT_BLOCK_K, tq // 2), tq), F32),
                        pltpu.VMEM((2, ATT_V_DIM + ATT_SUM_ROWS, tq), F32),
                        pltpu.VMEM((2, 2, LANES, tq), BF16)],
        compiler_params=pltpu.CompilerParams(dimension_semantics=("arbitrary",) * 3, vmem_limit_bytes=VMEM_LIMIT),
        name="attn",
    )(q, k, v, bound, lq1, lk1, lq2, lk2, subln_w)


def _out_kernel(x_ref, ys_ref, ya_ref, wos_ref, woa_ref, nw_ref, wg_ref, wu_ref, wd_ref, o_ref):
    x = (x_ref[...] + jnp.dot(ys_ref[...], wos_ref[...], preferred_element_type=F32)
         + jnp.dot(ya_ref[...], woa_ref[...], preferred_element_type=F32))
    h = _rms_norm(x, nw_ref[...]).astype(BF16)
    o_ref[...] = x + 0.5 * _swiglu(h, wg_ref, wu_ref, wd_ref)


def _out(x2d, y_ssd, y_att, wo_ssd, wo_att, norm_w, wg, wu, wd):
    t, d = x2d.shape
    tm = min(FFN_ROWS, t)
    row = lambda n: pl.BlockSpec((tm, n), lambda i: (i, 0))
    return pl.pallas_call(
        _out_kernel,
        grid=(t // tm,),
        in_specs=[row(d), row(y_ssd.shape[1]), row(y_att.shape[1]), _resident(wo_ssd.shape), _resident(wo_att.shape),
                  _resident((1, d)), _resident(wg.shape), _resident(wu.shape), _resident(wd.shape)],
        out_specs=row(d),
        out_shape=jax.ShapeDtypeStruct((t, d), F32),
        compiler_params=pltpu.CompilerParams(dimension_semantics=("arbitrary",), vmem_limit_bytes=VMEM_LIMIT),
        name="out_ffn",
    )(x2d, y_ssd, y_att, wo_ssd, wo_att, norm_w, wg, wu, wd)


def _pad_lanes(v):
    return jnp.zeros((1, LANES), F32).at[0, :v.shape[0]].set(v.astype(F32))


def kernel(x, ffn1_norm_w, ffn1_w_gate, ffn1_w_up, ffn1_w_down, mix_norm_w, w_in, conv_w, conv_b, dt_bias, a_log,
           d_skip, ssd_norm_w, q_norm_w, k_norm_w, lambda_q1, lambda_k1, lambda_q2, lambda_k2, attn_subln_w, w_out,
           ffn2_norm_w, ffn2_w_gate, ffn2_w_up, ffn2_w_down):
    bsz, s, d = x.shape
    depth = w_in.shape[0]
    ssd_heads = dt_bias.shape[1]
    ssd_width = ssd_heads * SSD_HEAD_DIM
    conv_ch = conv_w.shape[2]
    att_width = w_out.shape[1] - ssd_width
    qk_width = (att_width // ATT_V_DIM) * 2 * ATT_QK_DIM
    sizes = (ssd_width, conv_ch, ssd_heads, qk_width, qk_width, att_width)
    offs = [0]
    for n in sizes:
        offs.append(offs[-1] + n)
    row = lambda v: v.astype(F32).reshape(1, -1)

    x2d = x.reshape(bsz * s, d)
    for l in range(depth):
        lambda_init = 0.8 - 0.6 * math.exp(-0.3 * l)
        wz, wxbc, wdt, wq, wk, wv = (w_in[l][:, offs[i]:offs[i + 1]] for i in range(6))
        wdt = jnp.pad(wdt, ((0, 0), (0, LANES - ssd_heads)))
        wo_ssd, wo_att = w_out[l][:ssd_width].astype(BF16), w_out[l][ssd_width:].astype(BF16)

        x1 = _ffn(x2d, row(ffn1_norm_w[l]), ffn1_w_gate[l], ffn1_w_up[l], ffn1_w_down[l])
        z, xbc, dt, q, k, v = _proj(x1, row(mix_norm_w[l]), wz, wxbc, wdt, wq, wk, wv,
                                    row(jnp.tile(q_norm_w[l], 2)), row(jnp.tile(k_norm_w[l], 2)))
        y_ssd = _ssd(z.reshape(bsz, s, -1), xbc.reshape(bsz, s, -1), dt.reshape(bsz, s, -1),
                     conv_w[l].astype(F32), row(conv_b[l]), _pad_lanes(dt_bias[l]), _pad_lanes(a_log[l]),
                     row(jnp.repeat(d_skip[l], SSD_HEAD_DIM)), row(ssd_norm_w[l]))
        score_bound = (ATT_QK_DIM * Q_SCALE * ATT_BOUND_MARGIN * jnp.max(jnp.abs(q_norm_w[l]))
                       * jnp.max(jnp.abs(k_norm_w[l]))).astype(F32)
        attn_args = (q.reshape(bsz, s, -1), k.reshape(bsz, s, -1), v.reshape(bsz, s, -1), score_bound.reshape(1, 1),
                     row(lambda_q1[l]), row(lambda_k1[l]), row(lambda_q2[l]), row(lambda_k2[l]),
                     row(attn_subln_w[l]))
        y_att = lax.cond(score_bound <= ATT_MAX_BOUND,
                         lambda *a: _attn(*a, lambda_init, True), lambda *a: _attn(*a, lambda_init, False),
                         *attn_args)
        x2d = _out(x1, y_ssd.reshape(bsz * s, -1), y_att.reshape(bsz * s, -1), wo_ssd, wo_att,
                   row(ffn2_norm_w[l]), ffn2_w_gate[l], ffn2_w_up[l], ffn2_w_down[l])
    return x2d.reshape(bsz, s, d)
```

```python
import functools
import math

import jax
import jax.numpy as jnp
from jax import lax
from jax.experimental import pallas as pl
from jax.experimental.pallas import tpu as pltpu

F32 = jnp.float32
BF16 = jnp.bfloat16

NORM_EPS = 1e-6
SSD_HEAD_DIM = 64
SSD_GROUPS = 2
SSD_STATE = 128
SSD_CONV = 4
SSD_CHUNK = 128
ATT_QK_DIM = 64
ATT_V_DIM = 128
LANES = 128
CONV_HALO = 16
LOG2_E = math.log2(math.e)
Q_SCALE = ATT_QK_DIM ** -0.5 * LOG2_E
ATT_BOUND_MARGIN = 1.01
ATT_MAX_BOUND = 60.0

FFN_ROWS = 512
FFN_COLS = 256
ATT_BLOCK_Q = 2048
ATT_BLOCK_K = 512
ATT_COL_TILE = 256
ATT_SUM_ROWS = 16
VMEM_LIMIT = 56 * 1024 * 1024


def _resident(shape):
    nd = len(shape)
    return pl.BlockSpec(shape, lambda *_: (0,) * nd, pipeline_mode=pl.Buffered(1))


def _rms_norm(x, w):
    return x * lax.rsqrt(jnp.mean(x * x, axis=-1, keepdims=True) + NORM_EPS) * w


def _silu(x):
    h = 0.5 * x
    return h + h * jnp.tanh(h)


def _swiglu(h, wg_ref, wu_ref, wd_ref):
    d_ff = wg_ref.shape[1]
    acc = None
    for c in range(d_ff // FFN_COLS):
        sl = slice(c * FFN_COLS, (c + 1) * FFN_COLS)
        g = jnp.dot(h, wg_ref[:, sl].astype(BF16), preferred_element_type=F32)
        u = jnp.dot(h, wu_ref[:, sl].astype(BF16), preferred_element_type=F32)
        a = (_silu(g) * u).astype(BF16)
        d = jnp.dot(a, wd_ref[sl, :].astype(BF16), preferred_element_type=F32)
        acc = d if acc is None else acc + d
    return acc


def _ffn_kernel(x_ref, nw_ref, wg_ref, wu_ref, wd_ref, o_ref):
    x = x_ref[...]
    h = _rms_norm(x, nw_ref[...]).astype(BF16)
    o_ref[...] = x + 0.5 * _swiglu(h, wg_ref, wu_ref, wd_ref)


def _ffn(x2d, norm_w, wg, wu, wd):
    t, d = x2d.shape
    tm = min(FFN_ROWS, t)
    row = pl.BlockSpec((tm, d), lambda i: (i, 0))
    return pl.pallas_call(
        _ffn_kernel,
        grid=(t // tm,),
        in_specs=[row, _resident((1, d)), _resident(wg.shape), _resident(wu.shape), _resident(wd.shape)],
        out_specs=row,
        out_shape=jax.ShapeDtypeStruct((t, d), F32),
        compiler_params=pltpu.CompilerParams(dimension_semantics=("arbitrary",), vmem_limit_bytes=VMEM_LIMIT),
        name="ffn",
    )(x2d, norm_w, wg, wu, wd)


def _qk_norm(t, w2, scale):
    lo = lax.broadcasted_iota(jnp.int32, t.shape, 1) < ATT_QK_DIM
    x2 = t * t
    s_lo = jnp.sum(jnp.where(lo, x2, 0.0), axis=-1, keepdims=True)
    s_hi = jnp.sum(jnp.where(lo, 0.0, x2), axis=-1, keepdims=True)
    r = lax.rsqrt(jnp.where(lo, s_lo, s_hi) * (1.0 / ATT_QK_DIM) + NORM_EPS)
    return t * r * (w2 * scale)


def _proj_kernel(x_ref, nw_ref, wz_ref, wxbc_ref, wdt_ref, wq_ref, wk_ref, wv_ref, qw_ref, kw_ref,
                 z_ref, xbc_ref, dt_ref, q_ref, k_ref, v_ref):
    h = _rms_norm(x_ref[...], nw_ref[...]).astype(BF16)
    n_heads = wq_ref.shape[1] // LANES
    for w_ref, g_ref, o_ref, scale in ((wq_ref, qw_ref, q_ref, Q_SCALE), (wk_ref, kw_ref, k_ref, 1.0)):
        t = jnp.dot(h, w_ref[...].astype(BF16), preferred_element_type=F32)
        for hd in range(n_heads):
            sl = slice(hd * LANES, (hd + 1) * LANES)
            o_ref[:, sl] = _qk_norm(t[:, sl], g_ref[...], scale).astype(BF16)
    dt_ref[...] = jnp.dot(h, wdt_ref[...].astype(BF16), preferred_element_type=F32)
    v_ref[...] = jnp.dot(h, wv_ref[...].astype(BF16), preferred_element_type=F32).astype(BF16)
    z_ref[...] = jnp.dot(h, wz_ref[...].astype(BF16), preferred_element_type=F32).astype(BF16)
    xbc_ref[...] = jnp.dot(h, wxbc_ref[...].astype(BF16), preferred_element_type=F32).astype(BF16)


def _proj(x2d, norm_w, wz, wxbc, wdt, wq, wk, wv, qw2, kw2):
    t, d = x2d.shape
    tm = min(FFN_ROWS, t)
    row = lambda n: pl.BlockSpec((tm, n), lambda i: (i, 0))
    widths = (wz.shape[1], wxbc.shape[1], wdt.shape[1], wq.shape[1], wk.shape[1], wv.shape[1])
    dtypes = (BF16, BF16, F32, BF16, BF16, BF16)
    return pl.pallas_call(
        _proj_kernel,
        grid=(t // tm,),
        in_specs=[row(d), _resident((1, d))] + [_resident(w.shape) for w in (wz, wxbc, wdt, wq, wk, wv)]
                 + [_resident((1, LANES)), _resident((1, LANES))],
        out_specs=[row(n) for n in widths],
        out_shape=[jax.ShapeDtypeStruct((t, n), dt) for n, dt in zip(widths, dtypes)],
        compiler_params=pltpu.CompilerParams(dimension_semantics=("arbitrary",), vmem_limit_bytes=VMEM_LIMIT),
        name="proj",
    )(x2d, norm_w, wz, wxbc, wdt, wq, wk, wv, qw2, kw2)


def _lane_bcast(x, col):
    return jnp.broadcast_to(x[:, col:col + 1], x.shape)


def _expand_heads(x, pair):
    lo = lax.broadcasted_iota(jnp.int32, x.shape, 1) < SSD_HEAD_DIM
    return jnp.where(lo, _lane_bcast(x, 2 * pair), _lane_bcast(x, 2 * pair + 1))


def _ssd_step(first, z_ref, xbc_ref, halo_ref, dt_ref, cw_ref, cb_ref, dtb_ref, alog_ref, dskip_ref, nw_ref,
              y_ref, state_ref):
    L = SSD_CHUNK
    n_chunks = xbc_ref.shape[0] // L
    width = z_ref.shape[-1]
    n_pairs = width // LANES
    heads_per_group = (width // SSD_HEAD_DIM) // SSD_GROUPS
    gw = width // SSD_GROUPS

    out_t = lax.broadcasted_iota(jnp.int32, (L, CONV_HALO + L), 0)
    in_t = lax.broadcasted_iota(jnp.int32, (L, CONV_HALO + L), 1) - CONV_HALO
    shifts = [jnp.where(in_t == out_t - (SSD_CONV - 1 - kk), 1.0, 0.0).astype(BF16) for kk in range(SSD_CONV - 1)]
    causal = lax.broadcasted_iota(jnp.int32, (L, L), 0) >= lax.broadcasted_iota(jnp.int32, (L, L), 1)
    tril = causal.astype(F32)
    lo = lax.broadcasted_iota(jnp.int32, (L, LANES), 1) < SSD_HEAD_DIM
    zero = jnp.zeros((L, LANES), BF16)
    head_of_lane = lax.broadcasted_iota(jnp.int32, (LANES, width), 1) // SSD_HEAD_DIM
    expand = (lax.broadcasted_iota(jnp.int32, (LANES, width), 0) == head_of_lane).astype(BF16)
    neg_a = -jnp.exp(alog_ref[...])

    halo = halo_ref[...]
    tail = jnp.where(first, jnp.zeros_like(halo), halo)
    for ci in range(n_chunks):
        rows = slice(ci * L, (ci + 1) * L)

        x_cur = xbc_ref[rows, :]
        x_ext = jnp.concatenate([tail, x_cur], axis=0)
        tail = x_cur[L - CONV_HALO:, :]
        conv = cb_ref[...] + cw_ref[SSD_CONV - 1:SSD_CONV, :] * x_cur.astype(F32)
        for kk in range(SSD_CONV - 1):
            conv = conv + cw_ref[kk:kk + 1, :] * jnp.dot(shifts[kk], x_ext, preferred_element_type=F32)
        act = _silu(conv)
        yield
        xs = act[:, :width]
        bm = act[:, width:width + SSD_GROUPS * SSD_STATE].astype(BF16)
        cm = act[:, width + SSD_GROUPS * SSD_STATE:]

        raw = dt_ref[rows, :] + dtb_ref[...]
        dt = jnp.maximum(raw, 0.0) + jnp.log1p(jnp.exp(-jnp.abs(raw)))
        a_cs = LOG2_E * jnp.dot(tril, dt * neg_a, preferred_element_type=F32, precision=lax.Precision.HIGHEST)
        a_last = a_cs[L - 1:L, :]
        f = dt * jnp.exp2(a_last - a_cs)
        a_src_t = a_cs.T - jnp.log2(dt).T

        yield
        cm_bf = cm.astype(BF16)
        cb = [lax.dot_general(cm_bf[:, g * SSD_STATE:(g + 1) * SSD_STATE], bm[:, g * SSD_STATE:(g + 1) * SSD_STATE],
                              (((1,), (1,)), ((), ())), preferred_element_type=F32) for g in range(SSD_GROUPS)]
        xs_bf = xs.astype(BF16)
        y_tiles = []
        for pair in range(n_pairs):
            g = (2 * pair) // heads_per_group
            cm_g = cm[:, g * SSD_STATE:(g + 1) * SSD_STATE]
            sl = slice(pair * LANES, (pair + 1) * LANES)
            x_pair = xs_bf[:, sl]
            prev_pair = state_ref[:, sl].astype(BF16)
            y_pair = None
            for half in range(2):
                hd = 2 * pair + half
                a_col = _lane_bcast(a_cs, hd)
                seg = a_col - a_src_t[hd:hd + 1, :]
                m_h = (cb[g] * jnp.exp2(jnp.where(causal, seg, -jnp.inf))).astype(BF16)
                c_h = (cm_g * jnp.exp2(a_col)).astype(BF16)
                keep = lo if half == 0 else jnp.logical_not(lo)
                rhs = jnp.concatenate([jnp.where(keep, x_pair, zero), jnp.where(keep, prev_pair, zero)], axis=0)
                yh = jnp.dot(jnp.concatenate([m_h, c_h], axis=1), rhs, preferred_element_type=F32)
                y_pair = yh if y_pair is None else y_pair + yh
            y_tiles.append(y_pair)
            yield

        f_hi = f.astype(BF16)
        f_lo = (f - f_hi.astype(F32)).astype(BF16)
        f_exp = (jnp.dot(f_hi, expand, preferred_element_type=F32)
                 + jnp.dot(f_lo, expand, preferred_element_type=F32))
        ea_last = jnp.exp2(a_cs[L - 8:L, :])
        for pair in range(n_pairs):
            g = (2 * pair) // heads_per_group
            sl = slice(pair * LANES, (pair + 1) * LANES)
            bm_g = bm[:, g * SSD_STATE:(g + 1) * SSD_STATE]
            xw = (xs[:, sl] * f_exp[:, sl]).astype(BF16)
            new = lax.dot_general(bm_g, xw, (((0,), (0,)), ((), ())), preferred_element_type=F32)
            chunk_decay = _expand_heads(ea_last, pair)[7:8, :]
            state_ref[:, sl] = state_ref[:, sl] * chunk_decay + new
            if pair % 4 == 3:
                yield

        y = jnp.concatenate(y_tiles, axis=1) + xs * dskip_ref[...]
        gated = y * _silu(z_ref[rows, :].astype(F32))
        for g in range(SSD_GROUPS):
            sl = slice(g * gw, (g + 1) * gw)
            y_ref[rows, sl] = _rms_norm(gated[:, sl], nw_ref[:, sl]).astype(BF16)
        yield


def _attn_kernel(q_ref, k_ref, v_ref, bound_ref, lq1_ref, lk1_ref, lq2_ref, lk2_ref, sw_ref, o_ref, s_ref, acc_ref,
                 qt_ref, *, lambda_init, bounded, interleave=()):
    tq = o_ref.shape[0]
    tk = s_ref.shape[2]
    n_kb = tq // tk
    tc = ATT_COL_TILE
    n_ct = tq // tc
    n_q = q_ref.shape[0] // tq
    qi = pl.program_id(2)
    cols = [slice(c * tc, (c + 1) * tc) for c in range(n_ct)]

    def stage_q_t(blk, qslot):
        q = q_ref[pl.ds(pl.multiple_of(blk * tq, tq), tq), :]
        lo = lax.broadcasted_iota(jnp.int32, q.shape, 1) < ATT_QK_DIM
        zero = jnp.zeros_like(q)
        qt_ref[qslot, 0] = jnp.where(lo, q, zero).T
        qt_ref[qslot, 1] = jnp.where(lo, zero, q).T

    def keys(row0):
        return k_ref[pl.ds(pl.multiple_of(row0, tk), tk), :]

    ones_rows = jnp.ones((ATT_SUM_ROWS, tk), BF16)

    def values(row0):
        return jnp.concatenate([v_ref[pl.ds(pl.multiple_of(row0, tk), tk), :].T, ones_rows], axis=0)

    def qk(k, slot, qslot, c):
        for mp in range(2):
            s_ref[slot, mp, :, cols[c]] = jnp.dot(k, qt_ref[qslot, mp, :, cols[c]], preferred_element_type=F32)

    def softmax_pv(v_ext, slot, m, c, mask=None):
        m = [list(mm) for mm in m]
        for mp in range(2):
            s = s_ref[slot, mp, :, cols[c]]
            if mask is not None:
                s = jnp.where(mask, s, -jnp.inf)
            if bounded:
                p = jnp.exp2(s - bound_ref[...]).astype(BF16)
                acc_ref[mp, :, cols[c]] = acc_ref[mp, :, cols[c]] + jnp.dot(v_ext, p, preferred_element_type=F32)
                continue
            m_old = m[mp][c]
            m_new = jnp.maximum(m_old, jnp.max(s, axis=0, keepdims=True))
            p = jnp.exp2(s - m_new).astype(BF16)
            acc_ref[mp, :, cols[c]] = (jnp.exp2(m_old - m_new) * acc_ref[mp, :, cols[c]]
                                       + jnp.dot(v_ext, p, preferred_element_type=F32))
            m[mp][c] = m_new
        return tuple(tuple(mm) for mm in m)

    q_cur = qi % 2
    q_nxt = 1 - q_cur

    @pl.when(qi == 0)
    def _():
        stage_q_t(0, 0)
        k0 = keys(0)
        for c in range(n_ct):
            qk(k0, 0, 0, c)

    acc_ref[...] = jnp.zeros(acc_ref.shape, F32)
    neg_inf = () if bounded else tuple(jnp.full((1, tc), -jnp.inf, F32) for _ in range(n_ct))

    def pair(i, m):
        row0 = i * 2 * tk
        k_next, v_cur = keys(row0 + tk), values(row0)
        for c in range(n_ct):
            qk(k_next, 1, q_cur, c)
            m = softmax_pv(v_cur, 0, m, c)
        k_next, v_cur = keys(row0 + 2 * tk), values(row0 + tk)
        for c in range(n_ct):
            qk(k_next, 0, q_cur, c)
            m = softmax_pv(v_cur, 1, m, c)
        return m

    m = lax.fori_loop(0, qi * (n_kb // 2), pair, (neg_inf, neg_inf))

    row0 = qi * tq
    r = lax.broadcasted_iota(jnp.int32, (tk, tc), 0)
    col = lax.broadcasted_iota(jnp.int32, (tk, tc), 1)
    n_diag = tk // tc
    masks = [r <= col + j * tc for j in range(n_diag)]
    stage_q_t(jnp.minimum(qi + 1, n_q - 1), q_nxt)
    others = iter(interleave)

    lam = (jnp.exp(jnp.sum(lq1_ref[...] * lk1_ref[...], axis=-1, keepdims=True))
           - jnp.exp(jnp.sum(lq2_ref[...] * lk2_ref[...], axis=-1, keepdims=True)) + lambda_init)

    def finish(c):
        o_t = (acc_ref[0, :ATT_V_DIM, cols[c]] / acc_ref[0, ATT_V_DIM:ATT_V_DIM + 1, cols[c]]
               - lam * (acc_ref[1, :ATT_V_DIM, cols[c]] / acc_ref[1, ATT_V_DIM:ATT_V_DIM + 1, cols[c]]))
        o_ref[cols[c], :] = (_rms_norm(o_t.T, sw_ref[...]) * (1.0 - lambda_init)).astype(BF16)

    for d in range(n_kb):
        slot = d % 2
        last = d == n_kb - 1
        v_cur = values(row0 + d * tk)
        k_next = keys(0) if last else keys(row0 + (d + 1) * tk)
        for c in range(n_ct):
            if last:
                qk(k_next, 0, q_nxt, c)
            elif c >= (d + 1) * n_diag:
                qk(k_next, 1 - slot, q_cur, c)
            if c >= d * n_diag:
                m = softmax_pv(v_cur, slot, m, c, mask=masks[c - d * n_diag] if c < (d + 1) * n_diag else None)
            if c // n_diag == d:
                finish(c)
            next(others, None)
    for _ in others:
        pass


def _mix_kernel(z_ref, xbc_ref, halo_ref, dt_ref, cw_ref, cb_ref, dtb_ref, alog_ref, dskip_ref, ssd_nw_ref,
                q_ref, k_ref, v_ref, bound_ref, lq1_ref, lk1_ref, lq2_ref, lk2_ref, sw_ref,
                y_ref, o_ref, state_ref, s_ref, acc_ref, qt_ref, *, lambda_init, bounded):
    first = jnp.logical_and(pl.program_id(1) == 0, pl.program_id(2) == 0)

    @pl.when(first)
    def _():
        state_ref[...] = jnp.zeros(state_ref.shape, F32)

    ssd = _ssd_step(first, z_ref, xbc_ref, halo_ref, dt_ref, cw_ref, cb_ref, dtb_ref, alog_ref, dskip_ref, ssd_nw_ref,
                    y_ref, state_ref)
    _attn_kernel(q_ref, k_ref, v_ref, bound_ref, lq1_ref, lk1_ref, lq2_ref, lk2_ref, sw_ref, o_ref, s_ref, acc_ref,
                 qt_ref, lambda_init=lambda_init, bounded=bounded, interleave=ssd)


def _mix(z, xbc, dt, conv_w, conv_b, dt_bias, a_log, d_skip_exp, ssd_norm_w,
         q, k, v, bound, lq1, lk1, lq2, lk2, subln_w, lambda_init, bounded):
    bsz, s, width = q.shape
    n_heads = width // ATT_V_DIM
    tq = min(ATT_BLOCK_Q, s)
    n_q = s // tq
    rows = s // (n_heads * n_q)
    assert rows % SSD_CHUNK == 0 and rows % CONV_HALO == 0
    ssd_width, cw = z.shape[-1], xbc.shape[-1]
    slab = lambda n: pl.BlockSpec((None, rows, n), lambda b, h, i: (b, h * n_q + i, 0))
    halo = pl.BlockSpec((None, CONV_HALO, cw),
                        lambda b, h, i: (b, jnp.maximum((h * n_q + i) * (rows // CONV_HALO) - 1, 0), 0))
    o_spec = pl.BlockSpec((None, tq, LANES), lambda b, h, i: (b, i, h))
    head_spec = pl.BlockSpec((None, s, LANES), lambda b, h, i: (b, 0, h))
    small = _resident((1, ATT_QK_DIM))
    return pl.pallas_call(
        functools.partial(_mix_kernel, lambda_init=lambda_init, bounded=bounded),
        grid=(bsz, n_heads, n_q),
        in_specs=[slab(ssd_width), slab(cw), halo, slab(LANES), _resident(conv_w.shape), _resident((1, cw)),
                  _resident((1, LANES)), _resident((1, LANES)), _resident((1, ssd_width)), _resident((1, ssd_width)),
                  head_spec, head_spec, head_spec, _resident((1, 1)), small, small, small, small,
                  _resident((1, ATT_V_DIM))],
        out_specs=[slab(ssd_width), o_spec],
        out_shape=[jax.ShapeDtypeStruct((bsz, s, ssd_width), BF16), jax.ShapeDtypeStruct((bsz, s, width), BF16)],
        scratch_shapes=[pltpu.VMEM((SSD_STATE, ssd_width), F32),
                        pltpu.VMEM((2, 2, min(ATT_BLOCK_K, tq // 2), tq), F32),
                        pltpu.VMEM((2, ATT_V_DIM + ATT_SUM_ROWS, tq), F32),
                        pltpu.VMEM((2, 2, LANES, tq), BF16)],
        compiler_params=pltpu.CompilerParams(dimension_semantics=("arbitrary",) * 3, vmem_limit_bytes=VMEM_LIMIT),
        name="mix",
    )(z, xbc, xbc, dt, conv_w, conv_b, dt_bias, a_log, d_skip_exp, ssd_norm_w,
      q, k, v, bound, lq1, lk1, lq2, lk2, subln_w)


def _out_kernel(x_ref, ys_ref, ya_ref, wos_ref, woa_ref, nw_ref, wg_ref, wu_ref, wd_ref, o_ref):
    x = (x_ref[...] + jnp.dot(ys_ref[...], wos_ref[...], preferred_element_type=F32)
         + jnp.dot(ya_ref[...], woa_ref[...], preferred_element_type=F32))
    h = _rms_norm(x, nw_ref[...]).astype(BF16)
    o_ref[...] = x + 0.5 * _swiglu(h, wg_ref, wu_ref, wd_ref)


def _out(x2d, y_ssd, y_att, wo_ssd, wo_att, norm_w, wg, wu, wd):
    t, d = x2d.shape
    tm = min(FFN_ROWS, t)
    row = lambda n: pl.BlockSpec((tm, n), lambda i: (i, 0))
    return pl.pallas_call(
        _out_kernel,
        grid=(t // tm,),
        in_specs=[row(d), row(y_ssd.shape[1]), row(y_att.shape[1]), _resident(wo_ssd.shape), _resident(wo_att.shape),
                  _resident((1, d)), _resident(wg.shape), _resident(wu.shape), _resident(wd.shape)],
        out_specs=row(d),
        out_shape=jax.ShapeDtypeStruct((t, d), F32),
        compiler_params=pltpu.CompilerParams(dimension_semantics=("arbitrary",), vmem_limit_bytes=VMEM_LIMIT),
        name="out_ffn",
    )(x2d, y_ssd, y_att, wo_ssd, wo_att, norm_w, wg, wu, wd)


def _pad_lanes(v):
    return jnp.zeros((1, LANES), F32).at[0, :v.shape[0]].set(v.astype(F32))


def kernel(x, ffn1_norm_w, ffn1_w_gate, ffn1_w_up, ffn1_w_down, mix_norm_w, w_in, conv_w, conv_b, dt_bias, a_log,
           d_skip, ssd_norm_w, q_norm_w, k_norm_w, lambda_q1, lambda_k1, lambda_q2, lambda_k2, attn_subln_w, w_out,
           ffn2_norm_w, ffn2_w_gate, ffn2_w_up, ffn2_w_down):
    bsz, s, d = x.shape
    depth = w_in.shape[0]
    ssd_heads = dt_bias.shape[1]
    ssd_width = ssd_heads * SSD_HEAD_DIM
    conv_ch = conv_w.shape[2]
    att_width = w_out.shape[1] - ssd_width
    qk_width = (att_width // ATT_V_DIM) * 2 * ATT_QK_DIM
    sizes = (ssd_width, conv_ch, ssd_heads, qk_width, qk_width, att_width)
    offs = [0]
    for n in sizes:
        offs.append(offs[-1] + n)
    row = lambda v: v.astype(F32).reshape(1, -1)

    x2d = x.reshape(bsz * s, d)
    for l in range(depth):
        lambda_init = 0.8 - 0.6 * math.exp(-0.3 * l)
        wz, wxbc, wdt, wq, wk, wv = (w_in[l][:, offs[i]:offs[i + 1]] for i in range(6))
        wdt = jnp.pad(wdt, ((0, 0), (0, LANES - ssd_heads)))
        wo_ssd, wo_att = w_out[l][:ssd_width].astype(BF16), w_out[l][ssd_width:].astype(BF16)

        x1 = _ffn(x2d, row(ffn1_norm_w[l]), ffn1_w_gate[l], ffn1_w_up[l], ffn1_w_down[l])
        z, xbc, dt, q, k, v = _proj(x1, row(mix_norm_w[l]), wz, wxbc, wdt, wq, wk, wv,
                                    row(jnp.tile(q_norm_w[l], 2)), row(jnp.tile(k_norm_w[l], 2)))
        score_bound = (ATT_QK_DIM * Q_SCALE * ATT_BOUND_MARGIN * jnp.max(jnp.abs(q_norm_w[l]))
                       * jnp.max(jnp.abs(k_norm_w[l]))).astype(F32)
        mix_args = (z.reshape(bsz, s, -1), xbc.reshape(bsz, s, -1), dt.reshape(bsz, s, -1),
                    conv_w[l].astype(F32), row(conv_b[l]), _pad_lanes(dt_bias[l]), _pad_lanes(a_log[l]),
                    row(jnp.repeat(d_skip[l], SSD_HEAD_DIM)), row(ssd_norm_w[l]),
                    q.reshape(bsz, s, -1), k.reshape(bsz, s, -1), v.reshape(bsz, s, -1), score_bound.reshape(1, 1),
                    row(lambda_q1[l]), row(lambda_k1[l]), row(lambda_q2[l]), row(lambda_k2[l]),
                    row(attn_subln_w[l]))
        y_ssd, y_att = lax.cond(score_bound <= ATT_MAX_BOUND,
                                lambda *a: _mix(*a, lambda_init, True), lambda *a: _mix(*a, lambda_init, False),
                                *mix_args)
        x2d = _out(x1, y_ssd.reshape(bsz * s, -1), y_att.reshape(bsz * s, -1), wo_ssd, wo_att,
                   row(ffn2_norm_w[l]), ffn2_w_gate[l], ffn2_w_up[l], ffn2_w_down[l])
    return x2d.reshape(bsz, s, d)
```

```python
import functools
import math

import jax
import jax.numpy as jnp
from jax import lax
from jax.experimental import pallas as pl
from jax.experimental.pallas import tpu as pltpu

F32 = jnp.float32
BF16 = jnp.bfloat16

NORM_EPS = 1e-6
SSD_HEAD_DIM = 64
SSD_GROUPS = 2
SSD_STATE = 128
SSD_CONV = 4
SSD_CHUNK = 128
ATT_QK_DIM = 64
ATT_V_DIM = 128
LANES = 128
CONV_HALO = 16
LOG2_E = math.log2(math.e)
Q_SCALE = ATT_QK_DIM ** -0.5 * LOG2_E
ATT_BOUND_MARGIN = 1.01
ATT_MAX_BOUND = 60.0

FFN_ROWS = 512
FFN_COLS = 256
ATT_BLOCK_Q = 2048
ATT_BLOCK_K = 512
ATT_COL_TILE = 256
ATT_SUM_ROWS = 16
VMEM_LIMIT = 56 * 1024 * 1024


def _resident(shape):
    nd = len(shape)
    return pl.BlockSpec(shape, lambda *_: (0,) * nd, pipeline_mode=pl.Buffered(1))


def _rms_norm(x, w):
    return x * lax.rsqrt(jnp.mean(x * x, axis=-1, keepdims=True) + NORM_EPS) * w


def _silu(x):
    h = 0.5 * x
    return h + h * jnp.tanh(h)


def _swiglu(h, wg_ref, wu_ref, wd_ref):
    d_ff = wg_ref.shape[1]
    acc = None
    for c in range(d_ff // FFN_COLS):
        sl = slice(c * FFN_COLS, (c + 1) * FFN_COLS)
        g = jnp.dot(h, wg_ref[:, sl].astype(BF16), preferred_element_type=F32)
        u = jnp.dot(h, wu_ref[:, sl].astype(BF16), preferred_element_type=F32)
        a = (_silu(g) * u).astype(BF16)
        d = jnp.dot(a, wd_ref[sl, :].astype(BF16), preferred_element_type=F32)
        acc = d if acc is None else acc + d
    return acc


def _ffn_kernel(x_ref, nw_ref, wg_ref, wu_ref, wd_ref, o_ref):
    x = x_ref[...]
    h = _rms_norm(x, nw_ref[...]).astype(BF16)
    o_ref[...] = x + 0.5 * _swiglu(h, wg_ref, wu_ref, wd_ref)


def _ffn(x2d, norm_w, wg, wu, wd):
    t, d = x2d.shape
    tm = min(FFN_ROWS, t)
    row = pl.BlockSpec((tm, d), lambda i: (i, 0))
    return pl.pallas_call(
        _ffn_kernel,
        grid=(t // tm,),
        in_specs=[row, _resident((1, d)), _resident(wg.shape), _resident(wu.shape), _resident(wd.shape)],
        out_specs=row,
        out_shape=jax.ShapeDtypeStruct((t, d), F32),
        compiler_params=pltpu.CompilerParams(dimension_semantics=("arbitrary",), vmem_limit_bytes=VMEM_LIMIT),
        name="ffn",
    )(x2d, norm_w, wg, wu, wd)


def _qk_norm(t, w2, scale):
    lo = lax.broadcasted_iota(jnp.int32, t.shape, 1) < ATT_QK_DIM
    x2 = t * t
    s_lo = jnp.sum(jnp.where(lo, x2, 0.0), axis=-1, keepdims=True)
    s_hi = jnp.sum(jnp.where(lo, 0.0, x2), axis=-1, keepdims=True)
    r = lax.rsqrt(jnp.where(lo, s_lo, s_hi) * (1.0 / ATT_QK_DIM) + NORM_EPS)
    return t * r * (w2 * scale)


def _proj_kernel(x_ref, nw_ref, w_ref, qw_ref, kw_ref, z_ref, xbc_ref, dt_ref, q_ref, k_ref, v_ref, wqkv_ref, *,
                 offsets):
    o_z, o_xbc, o_dt, o_q, o_k, o_v, o_end = offsets

    @pl.when(pl.program_id(0) == 0)
    def _():
        wqkv_ref[...] = w_ref[:, o_q:o_end].astype(BF16)

    h = _rms_norm(x_ref[...], nw_ref[...]).astype(BF16)
    n_heads = (o_k - o_q) // LANES
    for lo_col, g_ref, o_ref, scale in ((0, qw_ref, q_ref, Q_SCALE), (o_k - o_q, kw_ref, k_ref, 1.0)):
        t = jnp.dot(h, wqkv_ref[:, lo_col:lo_col + o_k - o_q], preferred_element_type=F32)
        for hd in range(n_heads):
            sl = slice(hd * LANES, (hd + 1) * LANES)
            o_ref[:, sl] = _qk_norm(t[:, sl], g_ref[...], scale).astype(BF16)
    w_dt = w_ref[:, o_dt:o_dt + LANES]
    w_dt = jnp.where(lax.broadcasted_iota(jnp.int32, w_dt.shape, 1) < o_q - o_dt, w_dt, 0.0).astype(BF16)
    dt_ref[...] = jnp.dot(h, w_dt, preferred_element_type=F32)
    v_ref[...] = jnp.dot(h, wqkv_ref[:, o_v - o_q:], preferred_element_type=F32).astype(BF16)
    z_ref[...] = jnp.dot(h, w_ref[:, o_z:o_xbc].astype(BF16), preferred_element_type=F32).astype(BF16)
    xbc_ref[...] = jnp.dot(h, w_ref[:, o_xbc:o_dt].astype(BF16), preferred_element_type=F32).astype(BF16)


def _proj(x2d, norm_w, w, offsets, qw2, kw2):
    t, d = x2d.shape
    tm = min(FFN_ROWS, t)
    row = lambda n: pl.BlockSpec((tm, n), lambda i: (i, 0))
    o_z, o_xbc, o_dt, o_q, o_k, o_v, o_end = offsets
    assert o_dt % LANES == 0 and o_q - o_dt <= LANES and o_dt + LANES <= o_end
    widths = (o_xbc - o_z, o_dt - o_xbc, LANES, o_k - o_q, o_v - o_k, o_end - o_v)
    dtypes = (BF16, BF16, F32, BF16, BF16, BF16)
    return pl.pallas_call(
        functools.partial(_proj_kernel, offsets=offsets),
        grid=(t // tm,),
        in_specs=[row(d), _resident((1, d)), _resident(w.shape), _resident((1, LANES)), _resident((1, LANES))],
        out_specs=[row(n) for n in widths],
        out_shape=[jax.ShapeDtypeStruct((t, n), dt) for n, dt in zip(widths, dtypes)],
        scratch_shapes=[pltpu.VMEM((d, o_end - o_q), BF16)],
        compiler_params=pltpu.CompilerParams(dimension_semantics=("arbitrary",), vmem_limit_bytes=VMEM_LIMIT),
        name="proj",
    )(x2d, norm_w, w, qw2, kw2)


def _lane_bcast(x, col):
    return jnp.broadcast_to(x[:, col:col + 1], x.shape)


def _expand_heads(x, pair):
    lo = lax.broadcasted_iota(jnp.int32, x.shape, 1) < SSD_HEAD_DIM
    return jnp.where(lo, _lane_bcast(x, 2 * pair), _lane_bcast(x, 2 * pair + 1))


def _ssd_step(first, z_ref, xbc_ref, halo_ref, dt_ref, cw_ref, cb_ref, dtb_ref, alog_ref, dskip_ref, nw_ref,
              y_ref, state_ref):
    L = SSD_CHUNK
    n_chunks = xbc_ref.shape[0] // L
    width = z_ref.shape[-1]
    n_pairs = width // LANES
    heads_per_group = (width // SSD_HEAD_DIM) // SSD_GROUPS
    gw = width // SSD_GROUPS

    out_t = lax.broadcasted_iota(jnp.int32, (L, CONV_HALO + L), 0)
    in_t = lax.broadcasted_iota(jnp.int32, (L, CONV_HALO + L), 1) - CONV_HALO
    shifts = [jnp.where(in_t == out_t - (SSD_CONV - 1 - kk), 1.0, 0.0).astype(BF16) for kk in range(SSD_CONV - 1)]
    causal = lax.broadcasted_iota(jnp.int32, (L, L), 0) >= lax.broadcasted_iota(jnp.int32, (L, L), 1)
    tril = causal.astype(F32)
    lo = lax.broadcasted_iota(jnp.int32, (L, LANES), 1) < SSD_HEAD_DIM
    zero = jnp.zeros((L, LANES), BF16)
    head_of_lane = lax.broadcasted_iota(jnp.int32, (LANES, width), 1) // SSD_HEAD_DIM
    expand = (lax.broadcasted_iota(jnp.int32, (LANES, width), 0) == head_of_lane).astype(BF16)
    neg_a = -jnp.exp(alog_ref[...])

    halo = halo_ref[...]
    tail = jnp.where(first, jnp.zeros_like(halo), halo)
    for ci in range(n_chunks):
        rows = slice(ci * L, (ci + 1) * L)

        x_cur = xbc_ref[rows, :]
        x_ext = jnp.concatenate([tail, x_cur], axis=0)
        tail = x_cur[L - CONV_HALO:, :]
        conv = cb_ref[...] + cw_ref[SSD_CONV - 1:SSD_CONV, :] * x_cur.astype(F32)
        for kk in range(SSD_CONV - 1):
            conv = conv + cw_ref[kk:kk + 1, :] * jnp.dot(shifts[kk], x_ext, preferred_element_type=F32)
        act = _silu(conv)
        yield
        xs = act[:, :width]
        bm = act[:, width:width + SSD_GROUPS * SSD_STATE].astype(BF16)
        cm = act[:, width + SSD_GROUPS * SSD_STATE:]

        raw = dt_ref[rows, :] + dtb_ref[...]
        dt = jnp.maximum(raw, 0.0) + jnp.log1p(jnp.exp(-jnp.abs(raw)))
        a_cs = LOG2_E * jnp.dot(tril, dt * neg_a, preferred_element_type=F32, precision=lax.Precision.HIGHEST)
        a_last = a_cs[L - 1:L, :]
        f = dt * jnp.exp2(a_last - a_cs)
        a_src_t = a_cs.T - jnp.log2(dt).T

        yield
        cm_bf = cm.astype(BF16)
        cb = [lax.dot_general(cm_bf[:, g * SSD_STATE:(g + 1) * SSD_STATE], bm[:, g * SSD_STATE:(g + 1) * SSD_STATE],
                              (((1,), (1,)), ((), ())), preferred_element_type=F32) for g in range(SSD_GROUPS)]
        xs_bf = xs.astype(BF16)
        y_tiles = []
        for pair in range(n_pairs):
            g = (2 * pair) // heads_per_group
            cm_g = cm[:, g * SSD_STATE:(g + 1) * SSD_STATE]
            sl = slice(pair * LANES, (pair + 1) * LANES)
            x_pair = xs_bf[:, sl]
            prev_pair = state_ref[:, sl].astype(BF16)
            y_pair = None
            for half in range(2):
                hd = 2 * pair + half
                a_col = _lane_bcast(a_cs, hd)
                seg = a_col - a_src_t[hd:hd + 1, :]
                m_h = (cb[g] * jnp.exp2(jnp.where(causal, seg, -jnp.inf))).astype(BF16)
                c_h = (cm_g * jnp.exp2(a_col)).astype(BF16)
                keep = lo if half == 0 else jnp.logical_not(lo)
                rhs = jnp.concatenate([jnp.where(keep, x_pair, zero), jnp.where(keep, prev_pair, zero)], axis=0)
                yh = jnp.dot(jnp.concatenate([m_h, c_h], axis=1), rhs, preferred_element_type=F32)
                y_pair = yh if y_pair is None else y_pair + yh
            y_tiles.append(y_pair)
            yield

        f_hi = f.astype(BF16)
        f_lo = (f - f_hi.astype(F32)).astype(BF16)
        f_exp = (jnp.dot(f_hi, expand, preferred_element_type=F32)
                 + jnp.dot(f_lo, expand, preferred_element_type=F32))
        ea_last = jnp.exp2(a_cs[L - 8:L, :])
        for pair in range(n_pairs):
            g = (2 * pair) // heads_per_group
            sl = slice(pair * LANES, (pair + 1) * LANES)
            bm_g = bm[:, g * SSD_STATE:(g + 1) * SSD_STATE]
            xw = (xs[:, sl] * f_exp[:, sl]).astype(BF16)
            new = lax.dot_general(bm_g, xw, (((0,), (0,)), ((), ())), preferred_element_type=F32)
            chunk_decay = _expand_heads(ea_last, pair)[7:8, :]
            state_ref[:, sl] = state_ref[:, sl] * chunk_decay + new
            if pair % 4 == 3:
                yield

        y = jnp.concatenate(y_tiles, axis=1) + xs * dskip_ref[...]
        gated = y * _silu(z_ref[rows, :].astype(F32))
        for g in range(SSD_GROUPS):
            sl = slice(g * gw, (g + 1) * gw)
            y_ref[rows, sl] = _rms_norm(gated[:, sl], nw_ref[:, sl]).astype(BF16)
        yield


def _attn_kernel(q_ref, k_ref, v_ref, bound_ref, lq1_ref, lk1_ref, lq2_ref, lk2_ref, sw_ref, o_ref, s_ref, acc_ref,
                 qt_ref, *, lambda_init, bounded, interleave=()):
    tq = o_ref.shape[0]
    tk = s_ref.shape[2]
    n_kb = tq // tk
    tc = ATT_COL_TILE
    n_ct = tq // tc
    n_q = q_ref.shape[0] // tq
    qi = pl.program_id(2)
    cols = [slice(c * tc, (c + 1) * tc) for c in range(n_ct)]

    def stage_q_t(blk, qslot):
        q = q_ref[pl.ds(pl.multiple_of(blk * tq, tq), tq), :]
        lo = lax.broadcasted_iota(jnp.int32, q.shape, 1) < ATT_QK_DIM
        zero = jnp.zeros_like(q)
        qt_ref[qslot, 0] = jnp.where(lo, q, zero).T
        qt_ref[qslot, 1] = jnp.where(lo, zero, q).T

    def keys(row0):
        return k_ref[pl.ds(pl.multiple_of(row0, tk), tk), :]

    ones_rows = jnp.ones((ATT_SUM_ROWS, tk), BF16)

    def values(row0):
        return jnp.concatenate([v_ref[pl.ds(pl.multiple_of(row0, tk), tk), :].T, ones_rows], axis=0)

    def qk(k, slot, qslot, c):
        for mp in range(2):
            s_ref[slot, mp, :, cols[c]] = jnp.dot(k, qt_ref[qslot, mp, :, cols[c]], preferred_element_type=F32)

    def softmax_pv(v_ext, slot, m, c, mask=None):
        m = [list(mm) for mm in m]
        for mp in range(2):
            s = s_ref[slot, mp, :, cols[c]]
            if mask is not None:
                s = jnp.where(mask, s, -jnp.inf)
            if bounded:
                p = jnp.exp2(s - bound_ref[...]).astype(BF16)
                acc_ref[mp, :, cols[c]] = acc_ref[mp, :, cols[c]] + jnp.dot(v_ext, p, preferred_element_type=F32)
                continue
            m_old = m[mp][c]
            m_new = jnp.maximum(m_old, jnp.max(s, axis=0, keepdims=True))
            p = jnp.exp2(s - m_new).astype(BF16)
            acc_ref[mp, :, cols[c]] = (jnp.exp2(m_old - m_new) * acc_ref[mp, :, cols[c]]
                                       + jnp.dot(v_ext, p, preferred_element_type=F32))
            m[mp][c] = m_new
        return tuple(tuple(mm) for mm in m)

    q_cur = qi % 2
    q_nxt = 1 - q_cur

    @pl.when(qi == 0)
    def _():
        stage_q_t(0, 0)
        k0 = keys(0)
        for c in range(n_ct):
            qk(k0, 0, 0, c)

    acc_ref[...] = jnp.zeros(acc_ref.shape, F32)
    neg_inf = () if bounded else tuple(jnp.full((1, tc), -jnp.inf, F32) for _ in range(n_ct))

    def pair(i, m):
        row0 = i * 2 * tk
        k_next, v_cur = keys(row0 + tk), values(row0)
        for c in range(n_ct):
            qk(k_next, 1, q_cur, c)
            m = softmax_pv(v_cur, 0, m, c)
        k_next, v_cur = keys(row0 + 2 * tk), values(row0 + tk)
        for c in range(n_ct):
            qk(k_next, 0, q_cur, c)
            m = softmax_pv(v_cur, 1, m, c)
        return m

    m = lax.fori_loop(0, qi * (n_kb // 2), pair, (neg_inf, neg_inf))

    row0 = qi * tq
    r = lax.broadcasted_iota(jnp.int32, (tk, tc), 0)
    col = lax.broadcasted_iota(jnp.int32, (tk, tc), 1)
    n_diag = tk // tc
    masks = [r <= col + j * tc for j in range(n_diag)]
    stage_q_t(jnp.minimum(qi + 1, n_q - 1), q_nxt)
    others = iter(interleave)

    lam = (jnp.exp(jnp.sum(lq1_ref[...] * lk1_ref[...], axis=-1, keepdims=True))
           - jnp.exp(jnp.sum(lq2_ref[...] * lk2_ref[...], axis=-1, keepdims=True)) + lambda_init)

    def finish(c):
        o_t = (acc_ref[0, :ATT_V_DIM, cols[c]] / acc_ref[0, ATT_V_DIM:ATT_V_DIM + 1, cols[c]]
               - lam * (acc_ref[1, :ATT_V_DIM, cols[c]] / acc_ref[1, ATT_V_DIM:ATT_V_DIM + 1, cols[c]]))
        o_ref[cols[c], :] = (_rms_norm(o_t.T, sw_ref[...]) * (1.0 - lambda_init)).astype(BF16)

    for d in range(n_kb):
        slot = d % 2
        last = d == n_kb - 1
        v_cur = values(row0 + d * tk)
        k_next = keys(0) if last else keys(row0 + (d + 1) * tk)
        for c in range(n_ct):
            if last:
                qk(k_next, 0, q_nxt, c)
            elif c >= (d + 1) * n_diag:
                qk(k_next, 1 - slot, q_cur, c)
            if c >= d * n_diag:
                m = softmax_pv(v_cur, slot, m, c, mask=masks[c - d * n_diag] if c < (d + 1) * n_diag else None)
            if c // n_diag == d:
                finish(c)
            next(others, None)
    for _ in others:
        pass


def _mix_kernel(z_ref, xbc_ref, halo_ref, dt_ref, cw_ref, cb_ref, dtb_ref, alog_ref, dskip_ref, ssd_nw_ref,
                q_ref, k_ref, v_ref, bound_ref, lq1_ref, lk1_ref, lq2_ref, lk2_ref, sw_ref,
                y_ref, o_ref, state_ref, s_ref, acc_ref, qt_ref, *, lambda_init, bounded):
    first = jnp.logical_and(pl.program_id(1) == 0, pl.program_id(2) == 0)

    @pl.when(first)
    def _():
        state_ref[...] = jnp.zeros(state_ref.shape, F32)

    ssd = _ssd_step(first, z_ref, xbc_ref, halo_ref, dt_ref, cw_ref, cb_ref, dtb_ref, alog_ref, dskip_ref, ssd_nw_ref,
                    y_ref, state_ref)
    _attn_kernel(q_ref, k_ref, v_ref, bound_ref, lq1_ref, lk1_ref, lq2_ref, lk2_ref, sw_ref, o_ref, s_ref, acc_ref,
                 qt_ref, lambda_init=lambda_init, bounded=bounded, interleave=ssd)


def _mix(z, xbc, dt, conv_w, conv_b, dt_bias, a_log, d_skip_exp, ssd_norm_w,
         q, k, v, bound, lq1, lk1, lq2, lk2, subln_w, lambda_init, bounded):
    bsz, s, width = q.shape
    n_heads = width // ATT_V_DIM
    tq = min(ATT_BLOCK_Q, s)
    n_q = s // tq
    rows = s // (n_heads * n_q)
    assert rows % SSD_CHUNK == 0 and rows % CONV_HALO == 0
    ssd_width, cw = z.shape[-1], xbc.shape[-1]
    slab = lambda n: pl.BlockSpec((None, rows, n), lambda b, h, i: (b, h * n_q + i, 0))
    halo = pl.BlockSpec((None, CONV_HALO, cw),
                        lambda b, h, i: (b, jnp.maximum((h * n_q + i) * (rows // CONV_HALO) - 1, 0), 0))
    o_spec = pl.BlockSpec((None, tq, LANES), lambda b, h, i: (b, i, h))
    head_spec = pl.BlockSpec((None, s, LANES), lambda b, h, i: (b, 0, h))
    small = _resident((1, ATT_QK_DIM))
    return pl.pallas_call(
        functools.partial(_mix_kernel, lambda_init=lambda_init, bounded=bounded),
        grid=(bsz, n_heads, n_q),
        in_specs=[slab(ssd_width), slab(cw), halo, slab(LANES), _resident(conv_w.shape), _resident((1, cw)),
                  _resident((1, LANES)), _resident((1, LANES)), _resident((1, ssd_width)), _resident((1, ssd_width)),
                  head_spec, head_spec, head_spec, _resident((1, 1)), small, small, small, small,
                  _resident((1, ATT_V_DIM))],
        out_specs=[slab(ssd_width), o_spec],
        out_shape=[jax.ShapeDtypeStruct((bsz, s, ssd_width), BF16), jax.ShapeDtypeStruct((bsz, s, width), BF16)],
        scratch_shapes=[pltpu.VMEM((SSD_STATE, ssd_width), F32),
                        pltpu.VMEM((2, 2, min(ATT_BLOCK_K, tq // 2), tq), F32),
                        pltpu.VMEM((2, ATT_V_DIM + ATT_SUM_ROWS, tq), F32),
                        pltpu.VMEM((2, 2, LANES, tq), BF16)],
        compiler_params=pltpu.CompilerParams(dimension_semantics=("arbitrary",) * 3, vmem_limit_bytes=VMEM_LIMIT),
        name="mix",
    )(z, xbc, xbc, dt, conv_w, conv_b, dt_bias, a_log, d_skip_exp, ssd_norm_w,
      q, k, v, bound, lq1, lk1, lq2, lk2, subln_w)


def _out_kernel(x_ref, ys_ref, ya_ref, wos_ref, woa_ref, nw_ref, wg_ref, wu_ref, wd_ref, o_ref):
    x = (x_ref[...] + jnp.dot(ys_ref[...], wos_ref[...], preferred_element_type=F32)
         + jnp.dot(ya_ref[...], woa_ref[...], preferred_element_type=F32))
    h = _rms_norm(x, nw_ref[...]).astype(BF16)
    o_ref[...] = x + 0.5 * _swiglu(h, wg_ref, wu_ref, wd_ref)


def _out(x2d, y_ssd, y_att, wo_ssd, wo_att, norm_w, wg, wu, wd):
    t, d = x2d.shape
    tm = min(FFN_ROWS, t)
    row = lambda n: pl.BlockSpec((tm, n), lambda i: (i, 0))
    return pl.pallas_call(
        _out_kernel,
        grid=(t // tm,),
        in_specs=[row(d), row(y_ssd.shape[1]), row(y_att.shape[1]), _resident(wo_ssd.shape), _resident(wo_att.shape),
                  _resident((1, d)), _resident(wg.shape), _resident(wu.shape), _resident(wd.shape)],
        out_specs=row(d),
        out_shape=jax.ShapeDtypeStruct((t, d), F32),
        compiler_params=pltpu.CompilerParams(dimension_semantics=("arbitrary",), vmem_limit_bytes=VMEM_LIMIT),
        name="out_ffn",
    )(x2d, y_ssd, y_att, wo_ssd, wo_att, norm_w, wg, wu, wd)


def _pad_lanes(v):
    return jnp.zeros((1, LANES), F32).at[0, :v.shape[0]].set(v.astype(F32))


def kernel(x, ffn1_norm_w, ffn1_w_gate, ffn1_w_up, ffn1_w_down, mix_norm_w, w_in, conv_w, conv_b, dt_bias, a_log,
           d_skip, ssd_norm_w, q_norm_w, k_norm_w, lambda_q1, lambda_k1, lambda_q2, lambda_k2, attn_subln_w, w_out,
           ffn2_norm_w, ffn2_w_gate, ffn2_w_up, ffn2_w_down):
    bsz, s, d = x.shape
    depth = w_in.shape[0]
    ssd_heads = dt_bias.shape[1]
    ssd_width = ssd_heads * SSD_HEAD_DIM
    conv_ch = conv_w.shape[2]
    att_width = w_out.shape[1] - ssd_width
    qk_width = (att_width // ATT_V_DIM) * 2 * ATT_QK_DIM
    sizes = (ssd_width, conv_ch, ssd_heads, qk_width, qk_width, att_width)
    offs = [0]
    for n in sizes:
        offs.append(offs[-1] + n)
    row = lambda v: v.astype(F32).reshape(1, -1)

    x2d = x.reshape(bsz * s, d)
    for l in range(depth):
        lambda_init = 0.8 - 0.6 * math.exp(-0.3 * l)
        wo_ssd, wo_att = w_out[l][:ssd_width].astype(BF16), w_out[l][ssd_width:].astype(BF16)

        x1 = _ffn(x2d, row(ffn1_norm_w[l]), ffn1_w_gate[l], ffn1_w_up[l], ffn1_w_down[l])
        z, xbc, dt, q, k, v = _proj(x1, row(mix_norm_w[l]), w_in[l], tuple(offs),
                                    row(jnp.tile(q_norm_w[l], 2)), row(jnp.tile(k_norm_w[l], 2)))
        score_bound = (ATT_QK_DIM * Q_SCALE * ATT_BOUND_MARGIN * jnp.max(jnp.abs(q_norm_w[l]))
                       * jnp.max(jnp.abs(k_norm_w[l]))).astype(F32)
        mix_args = (z.reshape(bsz, s, -1), xbc.reshape(bsz, s, -1), dt.reshape(bsz, s, -1),
                    conv_w[l].astype(F32), row(conv_b[l]), _pad_lanes(dt_bias[l]), _pad_lanes(a_log[l]),
                    row(jnp.repeat(d_skip[l], SSD_HEAD_DIM)), row(ssd_norm_w[l]),
                    q.reshape(bsz, s, -1), k.reshape(bsz, s, -1), v.reshape(bsz, s, -1), score_bound.reshape(1, 1),
                    row(lambda_q1[l]), row(lambda_k1[l]), row(lambda_q2[l]), row(lambda_k2[l]),
                    row(attn_subln_w[l]))
        y_ssd, y_att = lax.cond(score_bound <= ATT_MAX_BOUND,
                                lambda *a: _mix(*a, lambda_init, True), lambda *a: _mix(*a, lambda_init, False),
                                *mix_args)
        x2d = _out(x1, y_ssd.reshape(bsz * s, -1), y_att.reshape(bsz * s, -1), wo_ssd, wo_att,
                   row(ffn2_norm_w[l]), ffn2_w_gate[l], ffn2_w_up[l], ffn2_w_down[l])
    return x2d.reshape(bsz, s, d)
```

```python
import functools
import math

import jax
import jax.numpy as jnp
from jax import lax
from jax.experimental import pallas as pl
from jax.experimental.pallas import tpu as pltpu

F32 = jnp.float32
BF16 = jnp.bfloat16

NORM_EPS = 1e-6
SSD_HEAD_DIM = 64
SSD_GROUPS = 2
SSD_STATE = 128
SSD_CONV = 4
SSD_CHUNK = 128
ATT_QK_DIM = 64
ATT_V_DIM = 128
LANES = 128
CONV_HALO = 16
LOG2_E = math.log2(math.e)
Q_SCALE = ATT_QK_DIM ** -0.5 * LOG2_E
ATT_BOUND_MARGIN = 1.01
ATT_MAX_BOUND = 60.0

FFN_ROWS = 512
FFN_COLS = 256
ATT_BLOCK_Q = 2048
ATT_BLOCK_K = 512
ATT_COL_TILE = 256
ATT_SUM_ROWS = 16
VMEM_LIMIT = 56 * 1024 * 1024


def _resident(shape):
    nd = len(shape)
    return pl.BlockSpec(shape, lambda *_: (0,) * nd, pipeline_mode=pl.Buffered(1))


def _rms_norm(x, w):
    return x * lax.rsqrt(jnp.mean(x * x, axis=-1, keepdims=True) + NORM_EPS) * w


def _silu(x):
    h = 0.5 * x
    return h + h * jnp.tanh(h)


def _swiglu(h, wg_ref, wu_ref, wd_ref):
    d_ff = wg_ref.shape[1]
    acc = None
    for c in range(d_ff // FFN_COLS):
        sl = slice(c * FFN_COLS, (c + 1) * FFN_COLS)
        g = jnp.dot(h, wg_ref[:, sl].astype(BF16), preferred_element_type=F32)
        u = jnp.dot(h, wu_ref[:, sl].astype(BF16), preferred_element_type=F32)
        a = (_silu(g) * u).astype(BF16)
        d = jnp.dot(a, wd_ref[sl, :].astype(BF16), preferred_element_type=F32)
        acc = d if acc is None else acc + d
    return acc


def _ffn_kernel(x_ref, nw_ref, wg_ref, wu_ref, wd_ref, o_ref):
    x = x_ref[...]
    h = _rms_norm(x, nw_ref[...]).astype(BF16)
    o_ref[...] = x + 0.5 * _swiglu(h, wg_ref, wu_ref, wd_ref)


def _ffn(x2d, norm_w, wg, wu, wd):
    t, d = x2d.shape
    tm = min(FFN_ROWS, t)
    row = pl.BlockSpec((tm, d), lambda i: (i, 0))
    return pl.pallas_call(
        _ffn_kernel,
        grid=(t // tm,),
        in_specs=[row, _resident((1, d)), _resident(wg.shape), _resident(wu.shape), _resident(wd.shape)],
        out_specs=row,
        out_shape=jax.ShapeDtypeStruct((t, d), F32),
        compiler_params=pltpu.CompilerParams(dimension_semantics=("arbitrary",), vmem_limit_bytes=VMEM_LIMIT),
        name="ffn",
    )(x2d, norm_w, wg, wu, wd)


def _qk_norm(t, w2, scale):
    lo = lax.broadcasted_iota(jnp.int32, t.shape, 1) < ATT_QK_DIM
    x2 = t * t
    s_lo = jnp.sum(jnp.where(lo, x2, 0.0), axis=-1, keepdims=True)
    s_hi = jnp.sum(jnp.where(lo, 0.0, x2), axis=-1, keepdims=True)
    r = lax.rsqrt(jnp.where(lo, s_lo, s_hi) * (1.0 / ATT_QK_DIM) + NORM_EPS)
    return t * r * (w2 * scale)


def _proj_kernel(x_ref, nw_ref, w_ref, qw_ref, kw_ref, z_ref, xbc_ref, dt_ref, q_ref, k_ref, v_ref, wqkv_ref, *,
                 offsets):
    o_z, o_xbc, o_dt, o_q, o_k, o_v, o_end = offsets

    @pl.when(pl.program_id(0) == 0)
    def _():
        wqkv_ref[...] = w_ref[:, o_q:o_end].astype(BF16)

    h = _rms_norm(x_ref[...], nw_ref[...]).astype(BF16)
    n_heads = (o_k - o_q) // LANES
    for lo_col, g_ref, o_ref, scale in ((0, qw_ref, q_ref, Q_SCALE), (o_k - o_q, kw_ref, k_ref, 1.0)):
        t = jnp.dot(h, wqkv_ref[:, lo_col:lo_col + o_k - o_q], preferred_element_type=F32)
        for hd in range(n_heads):
            sl = slice(hd * LANES, (hd + 1) * LANES)
            o_ref[:, sl] = _qk_norm(t[:, sl], g_ref[...], scale).astype(BF16)
    w_dt = w_ref[:, o_dt:o_dt + LANES]
    w_dt = jnp.where(lax.broadcasted_iota(jnp.int32, w_dt.shape, 1) < o_q - o_dt, w_dt, 0.0).astype(BF16)
    dt_ref[...] = jnp.dot(h, w_dt, preferred_element_type=F32)
    v_ref[...] = jnp.dot(h, wqkv_ref[:, o_v - o_q:], preferred_element_type=F32).astype(BF16)
    z_ref[...] = jnp.dot(h, w_ref[:, o_z:o_xbc].astype(BF16), preferred_element_type=F32).astype(BF16)
    xbc_ref[...] = jnp.dot(h, w_ref[:, o_xbc:o_dt].astype(BF16), preferred_element_type=F32).astype(BF16)


def _proj(x2d, norm_w, w_stack, layer, offsets, qw2, kw2):
    t, d = x2d.shape
    w_spec = pl.BlockSpec((None,) + w_stack.shape[1:], lambda i: (layer, 0, 0), pipeline_mode=pl.Buffered(1))
    tm = min(FFN_ROWS, t)
    row = lambda n: pl.BlockSpec((tm, n), lambda i: (i, 0))
    o_z, o_xbc, o_dt, o_q, o_k, o_v, o_end = offsets
    assert o_dt % LANES == 0 and o_q - o_dt <= LANES and o_dt + LANES <= o_end
    widths = (o_xbc - o_z, o_dt - o_xbc, LANES, o_k - o_q, o_v - o_k, o_end - o_v)
    dtypes = (BF16, BF16, F32, BF16, BF16, BF16)
    return pl.pallas_call(
        functools.partial(_proj_kernel, offsets=offsets),
        grid=(t // tm,),
        in_specs=[row(d), _resident((1, d)), w_spec, _resident((1, LANES)), _resident((1, LANES))],
        out_specs=[row(n) for n in widths],
        out_shape=[jax.ShapeDtypeStruct((t, n), dt) for n, dt in zip(widths, dtypes)],
        scratch_shapes=[pltpu.VMEM((d, o_end - o_q), BF16)],
        compiler_params=pltpu.CompilerParams(dimension_semantics=("arbitrary",), vmem_limit_bytes=VMEM_LIMIT),
        name="proj",
    )(x2d, norm_w, w_stack, qw2, kw2)


def _lane_bcast(x, col):
    return jnp.broadcast_to(x[:, col:col + 1], x.shape)


def _expand_heads(x, pair):
    lo = lax.broadcasted_iota(jnp.int32, x.shape, 1) < SSD_HEAD_DIM
    return jnp.where(lo, _lane_bcast(x, 2 * pair), _lane_bcast(x, 2 * pair + 1))


def _ssd_step(first, z_ref, xbc_ref, halo_ref, dt_ref, cw_ref, cb_ref, dtb_ref, alog_ref, dskip_ref, nw_ref,
              y_ref, state_ref):
    L = SSD_CHUNK
    n_chunks = xbc_ref.shape[0] // L
    width = z_ref.shape[-1]
    n_pairs = width // LANES
    heads_per_group = (width // SSD_HEAD_DIM) // SSD_GROUPS
    gw = width // SSD_GROUPS

    out_t = lax.broadcasted_iota(jnp.int32, (L, CONV_HALO + L), 0)
    in_t = lax.broadcasted_iota(jnp.int32, (L, CONV_HALO + L), 1) - CONV_HALO
    shifts = [jnp.where(in_t == out_t - (SSD_CONV - 1 - kk), 1.0, 0.0).astype(BF16) for kk in range(SSD_CONV - 1)]
    causal = lax.broadcasted_iota(jnp.int32, (L, L), 0) >= lax.broadcasted_iota(jnp.int32, (L, L), 1)
    tril = causal.astype(F32)
    lo = lax.broadcasted_iota(jnp.int32, (L, LANES), 1) < SSD_HEAD_DIM
    zero = jnp.zeros((L, LANES), BF16)
    head_of_lane = lax.broadcasted_iota(jnp.int32, (LANES, width), 1) // SSD_HEAD_DIM
    expand = (lax.broadcasted_iota(jnp.int32, (LANES, width), 0) == head_of_lane).astype(BF16)
    neg_a = -jnp.exp(alog_ref[...])

    halo = halo_ref[...]
    tail = jnp.where(first, jnp.zeros_like(halo), halo)
    for ci in range(n_chunks):
        rows = slice(ci * L, (ci + 1) * L)

        x_cur = xbc_ref[rows, :]
        x_ext = jnp.concatenate([tail, x_cur], axis=0)
        tail = x_cur[L - CONV_HALO:, :]
        conv = cb_ref[...] + cw_ref[SSD_CONV - 1:SSD_CONV, :] * x_cur.astype(F32)
        for kk in range(SSD_CONV - 1):
            conv = conv + cw_ref[kk:kk + 1, :] * jnp.dot(shifts[kk], x_ext, preferred_element_type=F32)
        act = _silu(conv)
        yield
        xs = act[:, :width]
        bm = act[:, width:width + SSD_GROUPS * SSD_STATE].astype(BF16)
        cm = act[:, width + SSD_GROUPS * SSD_STATE:]

        raw = dt_ref[rows, :] + dtb_ref[...]
        dt = jnp.maximum(raw, 0.0) + jnp.log1p(jnp.exp(-jnp.abs(raw)))
        a_cs = LOG2_E * jnp.dot(tril, dt * neg_a, preferred_element_type=F32, precision=lax.Precision.HIGHEST)
        a_last = a_cs[L - 1:L, :]
        f = dt * jnp.exp2(a_last - a_cs)
        a_src_t = a_cs.T - jnp.log2(dt).T

        yield
        cm_bf = cm.astype(BF16)
        cb = [lax.dot_general(cm_bf[:, g * SSD_STATE:(g + 1) * SSD_STATE], bm[:, g * SSD_STATE:(g + 1) * SSD_STATE],
                              (((1,), (1,)), ((), ())), preferred_element_type=F32) for g in range(SSD_GROUPS)]
        xs_bf = xs.astype(BF16)
        y_tiles = []
        for pair in range(n_pairs):
            g = (2 * pair) // heads_per_group
            cm_g = cm[:, g * SSD_STATE:(g + 1) * SSD_STATE]
            sl = slice(pair * LANES, (pair + 1) * LANES)
            x_pair = xs_bf[:, sl]
            prev_pair = state_ref[:, sl].astype(BF16)
            y_pair = None
            for half in range(2):
                hd = 2 * pair + half
                a_col = _lane_bcast(a_cs, hd)
                seg = a_col - a_src_t[hd:hd + 1, :]
                m_h = (cb[g] * jnp.exp2(jnp.where(causal, seg, -jnp.inf))).astype(BF16)
                c_h = (cm_g * jnp.exp2(a_col)).astype(BF16)
                keep = lo if half == 0 else jnp.logical_not(lo)
                rhs = jnp.concatenate([jnp.where(keep, x_pair, zero), jnp.where(keep, prev_pair, zero)], axis=0)
                yh = jnp.dot(jnp.concatenate([m_h, c_h], axis=1), rhs, preferred_element_type=F32)
                y_pair = yh if y_pair is None else y_pair + yh
            y_tiles.append(y_pair)
            yield

        f_hi = f.astype(BF16)
        f_lo = (f - f_hi.astype(F32)).astype(BF16)
        f_exp = (jnp.dot(f_hi, expand, preferred_element_type=F32)
                 + jnp.dot(f_lo, expand, preferred_element_type=F32))
        ea_last = jnp.exp2(a_cs[L - 8:L, :])
        for pair in range(n_pairs):
            g = (2 * pair) // heads_per_group
            sl = slice(pair * LANES, (pair + 1) * LANES)
            bm_g = bm[:, g * SSD_STATE:(g + 1) * SSD_STATE]
            xw = (xs[:, sl] * f_exp[:, sl]).astype(BF16)
            new = lax.dot_general(bm_g, xw, (((0,), (0,)), ((), ())), preferred_element_type=F32)
            chunk_decay = _expand_heads(ea_last, pair)[7:8, :]
            state_ref[:, sl] = state_ref[:, sl] * chunk_decay + new
            if pair % 4 == 3:
                yield

        y = jnp.concatenate(y_tiles, axis=1) + xs * dskip_ref[...]
        gated = y * _silu(z_ref[rows, :].astype(F32))
        for g in range(SSD_GROUPS):
            sl = slice(g * gw, (g + 1) * gw)
            y_ref[rows, sl] = _rms_norm(gated[:, sl], nw_ref[:, sl]).astype(BF16)
        yield


def _attn_kernel(q_ref, k_ref, v_ref, bound_ref, lq1_ref, lk1_ref, lq2_ref, lk2_ref, sw_ref, o_ref, s_ref, acc_ref,
                 qt_ref, *, lambda_init, bounded, interleave=()):
    tq = o_ref.shape[0]
    tk = s_ref.shape[2]
    n_kb = tq // tk
    tc = ATT_COL_TILE
    n_ct = tq // tc
    n_q = q_ref.shape[0] // tq
    qi = pl.program_id(2)
    cols = [slice(c * tc, (c + 1) * tc) for c in range(n_ct)]

    def stage_q_t(blk, qslot):
        q = q_ref[pl.ds(pl.multiple_of(blk * tq, tq), tq), :]
        lo = lax.broadcasted_iota(jnp.int32, q.shape, 1) < ATT_QK_DIM
        zero = jnp.zeros_like(q)
        qt_ref[qslot, 0] = jnp.where(lo, q, zero).T
        qt_ref[qslot, 1] = jnp.where(lo, zero, q).T

    def keys(row0):
        return k_ref[pl.ds(pl.multiple_of(row0, tk), tk), :]

    ones_rows = jnp.ones((ATT_SUM_ROWS, tk), BF16)

    def values(row0):
        return jnp.concatenate([v_ref[pl.ds(pl.multiple_of(row0, tk), tk), :].T, ones_rows], axis=0)

    def qk(k, slot, qslot, c):
        for mp in range(2):
            s_ref[slot, mp, :, cols[c]] = jnp.dot(k, qt_ref[qslot, mp, :, cols[c]], preferred_element_type=F32)

    def softmax_pv(v_ext, slot, m, c, mask=None):
        m = [list(mm) for mm in m]
        for mp in range(2):
            s = s_ref[slot, mp, :, cols[c]]
            if mask is not None:
                s = jnp.where(mask, s, -jnp.inf)
            if bounded:
                p = jnp.exp2(s - bound_ref[...]).astype(BF16)
                acc_ref[mp, :, cols[c]] = acc_ref[mp, :, cols[c]] + jnp.dot(v_ext, p, preferred_element_type=F32)
                continue
            m_old = m[mp][c]
            m_new = jnp.maximum(m_old, jnp.max(s, axis=0, keepdims=True))
            p = jnp.exp2(s - m_new).astype(BF16)
            acc_ref[mp, :, cols[c]] = (jnp.exp2(m_old - m_new) * acc_ref[mp, :, cols[c]]
                                       + jnp.dot(v_ext, p, preferred_element_type=F32))
            m[mp][c] = m_new
        return tuple(tuple(mm) for mm in m)

    q_cur = qi % 2
    q_nxt = 1 - q_cur

    @pl.when(qi == 0)
    def _():
        stage_q_t(0, 0)
        k0 = keys(0)
        for c in range(n_ct):
            qk(k0, 0, 0, c)

    acc_ref[...] = jnp.zeros(acc_ref.shape, F32)
    neg_inf = () if bounded else tuple(jnp.full((1, tc), -jnp.inf, F32) for _ in range(n_ct))

    def pair(i, m):
        row0 = i * 2 * tk
        k_next, v_cur = keys(row0 + tk), values(row0)
        for c in range(n_ct):
            qk(k_next, 1, q_cur, c)
            m = softmax_pv(v_cur, 0, m, c)
        k_next, v_cur = keys(row0 + 2 * tk), values(row0 + tk)
        for c in range(n_ct):
            qk(k_next, 0, q_cur, c)
            m = softmax_pv(v_cur, 1, m, c)
        return m

    m = lax.fori_loop(0, qi * (n_kb // 2), pair, (neg_inf, neg_inf))

    row0 = qi * tq
    r = lax.broadcasted_iota(jnp.int32, (tk, tc), 0)
    col = lax.broadcasted_iota(jnp.int32, (tk, tc), 1)
    n_diag = tk // tc
    masks = [r <= col + j * tc for j in range(n_diag)]
    stage_q_t(jnp.minimum(qi + 1, n_q - 1), q_nxt)
    others = iter(interleave)

    lam = (jnp.exp(jnp.sum(lq1_ref[...] * lk1_ref[...], axis=-1, keepdims=True))
           - jnp.exp(jnp.sum(lq2_ref[...] * lk2_ref[...], axis=-1, keepdims=True)) + lambda_init)

    def finish(c):
        o_t = (acc_ref[0, :ATT_V_DIM, cols[c]] / acc_ref[0, ATT_V_DIM:ATT_V_DIM + 1, cols[c]]
               - lam * (acc_ref[1, :ATT_V_DIM, cols[c]] / acc_ref[1, ATT_V_DIM:ATT_V_DIM + 1, cols[c]]))
        o_ref[cols[c], :] = (_rms_norm(o_t.T, sw_ref[...]) * (1.0 - lambda_init)).astype(BF16)

    for d in range(n_kb):
        slot = d % 2
        last = d == n_kb - 1
        v_cur = values(row0 + d * tk)
        k_next = keys(0) if last else keys(row0 + (d + 1) * tk)
        for c in range(n_ct):
            if last:
                qk(k_next, 0, q_nxt, c)
            elif c >= (d + 1) * n_diag:
                qk(k_next, 1 - slot, q_cur, c)
            if c >= d * n_diag:
                m = softmax_pv(v_cur, slot, m, c, mask=masks[c - d * n_diag] if c < (d + 1) * n_diag else None)
            if c // n_diag == d:
                finish(c)
            next(others, None)
    for _ in others:
        pass


def _mix_kernel(z_ref, xbc_ref, halo_ref, dt_ref, cw_ref, cb_ref, dtb_ref, alog_ref, dskip_ref, ssd_nw_ref,
                q_ref, k_ref, v_ref, bound_ref, lq1_ref, lk1_ref, lq2_ref, lk2_ref, sw_ref,
                y_ref, o_ref, state_ref, s_ref, acc_ref, qt_ref, *, lambda_init, bounded):
    first = jnp.logical_and(pl.program_id(1) == 0, pl.program_id(2) == 0)

    @pl.when(first)
    def _():
        state_ref[...] = jnp.zeros(state_ref.shape, F32)

    ssd = _ssd_step(first, z_ref, xbc_ref, halo_ref, dt_ref, cw_ref, cb_ref, dtb_ref, alog_ref, dskip_ref, ssd_nw_ref,
                    y_ref, state_ref)
    _attn_kernel(q_ref, k_ref, v_ref, bound_ref, lq1_ref, lk1_ref, lq2_ref, lk2_ref, sw_ref, o_ref, s_ref, acc_ref,
                 qt_ref, lambda_init=lambda_init, bounded=bounded, interleave=ssd)


def _mix(z, xbc, dt, conv_w, conv_b, dt_bias, a_log, d_skip_exp, ssd_norm_w,
         q, k, v, bound, lq1, lk1, lq2, lk2, subln_w, lambda_init, bounded):
    bsz, s, width = q.shape
    n_heads = width // ATT_V_DIM
    tq = min(ATT_BLOCK_Q, s)
    n_q = s // tq
    rows = s // (n_heads * n_q)
    assert rows % SSD_CHUNK == 0 and rows % CONV_HALO == 0
    ssd_width, cw = z.shape[-1], xbc.shape[-1]
    slab = lambda n: pl.BlockSpec((None, rows, n), lambda b, h, i: (b, h * n_q + i, 0))
    halo = pl.BlockSpec((None, CONV_HALO, cw),
                        lambda b, h, i: (b, jnp.maximum((h * n_q + i) * (rows // CONV_HALO) - 1, 0), 0))
    o_spec = pl.BlockSpec((None, tq, LANES), lambda b, h, i: (b, i, h))
    head_spec = pl.BlockSpec((None, s, LANES), lambda b, h, i: (b, 0, h))
    small = _resident((1, ATT_QK_DIM))
    return pl.pallas_call(
        functools.partial(_mix_kernel, lambda_init=lambda_init, bounded=bounded),
        grid=(bsz, n_heads, n_q),
        in_specs=[slab(ssd_width), slab(cw), halo, slab(LANES), _resident(conv_w.shape), _resident((1, cw)),
                  _resident((1, LANES)), _resident((1, LANES)), _resident((1, ssd_width)), _resident((1, ssd_width)),
                  head_spec, head_spec, head_spec, _resident((1, 1)), small, small, small, small,
                  _resident((1, ATT_V_DIM))],
        out_specs=[slab(ssd_width), o_spec],
        out_shape=[jax.ShapeDtypeStruct((bsz, s, ssd_width), BF16), jax.ShapeDtypeStruct((bsz, s, width), BF16)],
        scratch_shapes=[pltpu.VMEM((SSD_STATE, ssd_width), F32),
                        pltpu.VMEM((2, 2, min(ATT_BLOCK_K, tq // 2), tq), F32),
                        pltpu.VMEM((2, ATT_V_DIM + ATT_SUM_ROWS, tq), F32),
                        pltpu.VMEM((2, 2, LANES, tq), BF16)],
        compiler_params=pltpu.CompilerParams(dimension_semantics=("arbitrary",) * 3, vmem_limit_bytes=VMEM_LIMIT),
        name="mix",
    )(z, xbc, xbc, dt, conv_w, conv_b, dt_bias, a_log, d_skip_exp, ssd_norm_w,
      q, k, v, bound, lq1, lk1, lq2, lk2, subln_w)


def _out_kernel(x_ref, ys_ref, ya_ref, wos_ref, woa_ref, nw_ref, wg_ref, wu_ref, wd_ref, o_ref):
    x = (x_ref[...] + jnp.dot(ys_ref[...], wos_ref[...], preferred_element_type=F32)
         + jnp.dot(ya_ref[...], woa_ref[...], preferred_element_type=F32))
    h = _rms_norm(x, nw_ref[...]).astype(BF16)
    o_ref[...] = x + 0.5 * _swiglu(h, wg_ref, wu_ref, wd_ref)


def _out(x2d, y_ssd, y_att, wo_ssd, wo_att, norm_w, wg, wu, wd):
    t, d = x2d.shape
    tm = min(FFN_ROWS, t)
    row = lambda n: pl.BlockSpec((tm, n), lambda i: (i, 0))
    return pl.pallas_call(
        _out_kernel,
        grid=(t // tm,),
        in_specs=[row(d), row(y_ssd.shape[1]), row(y_att.shape[1]), _resident(wo_ssd.shape), _resident(wo_att.shape),
                  _resident((1, d)), _resident(wg.shape), _resident(wu.shape), _resident(wd.shape)],
        out_specs=row(d),
        out_shape=jax.ShapeDtypeStruct((t, d), F32),
        compiler_params=pltpu.CompilerParams(dimension_semantics=("arbitrary",), vmem_limit_bytes=VMEM_LIMIT),
        name="out_ffn",
    )(x2d, y_ssd, y_att, wo_ssd, wo_att, norm_w, wg, wu, wd)


def _pad_lanes(v):
    return jnp.zeros((1, LANES), F32).at[0, :v.shape[0]].set(v.astype(F32))


def kernel(x, ffn1_norm_w, ffn1_w_gate, ffn1_w_up, ffn1_w_down, mix_norm_w, w_in, conv_w, conv_b, dt_bias, a_log,
           d_skip, ssd_norm_w, q_norm_w, k_norm_w, lambda_q1, lambda_k1, lambda_q2, lambda_k2, attn_subln_w, w_out,
           ffn2_norm_w, ffn2_w_gate, ffn2_w_up, ffn2_w_down):
    bsz, s, d = x.shape
    depth = w_in.shape[0]
    ssd_heads = dt_bias.shape[1]
    ssd_width = ssd_heads * SSD_HEAD_DIM
    conv_ch = conv_w.shape[2]
    att_width = w_out.shape[1] - ssd_width
    qk_width = (att_width // ATT_V_DIM) * 2 * ATT_QK_DIM
    sizes = (ssd_width, conv_ch, ssd_heads, qk_width, qk_width, att_width)
    offs = [0]
    for n in sizes:
        offs.append(offs[-1] + n)
    row = lambda v: v.astype(F32).reshape(1, -1)

    x2d = x.reshape(bsz * s, d)
    for l in range(depth):
        lambda_init = 0.8 - 0.6 * math.exp(-0.3 * l)
        wo_ssd, wo_att = w_out[l][:ssd_width].astype(BF16), w_out[l][ssd_width:].astype(BF16)

        x1 = _ffn(x2d, row(ffn1_norm_w[l]), ffn1_w_gate[l], ffn1_w_up[l], ffn1_w_down[l])
        z, xbc, dt, q, k, v = _proj(x1, row(mix_norm_w[l]), w_in, l, tuple(offs),
                                    row(jnp.tile(q_norm_w[l], 2)), row(jnp.tile(k_norm_w[l], 2)))
        score_bound = (ATT_QK_DIM * Q_SCALE * ATT_BOUND_MARGIN * jnp.max(jnp.abs(q_norm_w[l]))
                       * jnp.max(jnp.abs(k_norm_w[l]))).astype(F32)
        mix_args = (z.reshape(bsz, s, -1), xbc.reshape(bsz, s, -1), dt.reshape(bsz, s, -1),
                    conv_w[l].astype(F32), row(conv_b[l]), _pad_lanes(dt_bias[l]), _pad_lanes(a_log[l]),
                    row(jnp.repeat(d_skip[l], SSD_HEAD_DIM)), row(ssd_norm_w[l]),
                    q.reshape(bsz, s, -1), k.reshape(bsz, s, -1), v.reshape(bsz, s, -1), score_bound.reshape(1, 1),
                    row(lambda_q1[l]), row(lambda_k1[l]), row(lambda_q2[l]), row(lambda_k2[l]),
                    row(attn_subln_w[l]))
        y_ssd, y_att = lax.cond(score_bound <= ATT_MAX_BOUND,
                                lambda *a: _mix(*a, lambda_init, True), lambda *a: _mix(*a, lambda_init, False),
                                *mix_args)
        x2d = _out(x1, y_ssd.reshape(bsz * s, -1), y_att.reshape(bsz * s, -1), wo_ssd, wo_att,
                   row(ffn2_norm_w[l]), ffn2_w_gate[l], ffn2_w_up[l], ffn2_w_down[l])
    return x2d.reshape(bsz, s, d)
```

```python
import functools
import math

import jax
import jax.numpy as jnp
from jax import lax
from jax.experimental import pallas as pl
from jax.experimental.pallas import tpu as pltpu

F32 = jnp.float32
BF16 = jnp.bfloat16

NORM_EPS = 1e-6
SSD_HEAD_DIM = 64
SSD_GROUPS = 2
SSD_STATE = 128
SSD_CONV = 4
SSD_CHUNK = 128
ATT_QK_DIM = 64
ATT_V_DIM = 128
LANES = 128
CONV_HALO = 16
LOG2_E = math.log2(math.e)
Q_SCALE = ATT_QK_DIM ** -0.5 * LOG2_E
ATT_BOUND_MARGIN = 1.01
ATT_MAX_BOUND = 60.0

FFN_ROWS = 512
FFN_COLS = 256
ATT_BLOCK_Q = 2048
ATT_BLOCK_K = 512
ATT_COL_TILE = 256
ATT_SUM_ROWS = 16
VMEM_LIMIT = 56 * 1024 * 1024


def _resident(shape):
    nd = len(shape)
    return pl.BlockSpec(shape, lambda *_: (0,) * nd, pipeline_mode=pl.Buffered(1))


def _rms_norm(x, w):
    return x * lax.rsqrt(jnp.mean(x * x, axis=-1, keepdims=True) + NORM_EPS) * w


def _silu(x):
    h = 0.5 * x
    return h + h * jnp.tanh(h)


def _swiglu(h, wg_ref, wu_ref, wd_ref):
    d_ff = wg_ref.shape[1]
    acc = None
    for c in range(d_ff // FFN_COLS):
        sl = slice(c * FFN_COLS, (c + 1) * FFN_COLS)
        g = jnp.dot(h, wg_ref[:, sl].astype(BF16), preferred_element_type=F32)
        u = jnp.dot(h, wu_ref[:, sl].astype(BF16), preferred_element_type=F32)
        a = (_silu(g) * u).astype(BF16)
        d = jnp.dot(a, wd_ref[sl, :].astype(BF16), preferred_element_type=F32)
        acc = d if acc is None else acc + d
    return acc


def _ffn_kernel(x_ref, nw_ref, wg_ref, wu_ref, wd_ref, o_ref):
    x = x_ref[...]
    h = _rms_norm(x, nw_ref[...]).astype(BF16)
    o_ref[...] = x + 0.5 * _swiglu(h, wg_ref, wu_ref, wd_ref)


def _ffn(x2d, norm_w, wg, wu, wd):
    t, d = x2d.shape
    tm = min(FFN_ROWS, t)
    row = pl.BlockSpec((tm, d), lambda i: (i, 0))
    return pl.pallas_call(
        _ffn_kernel,
        grid=(t // tm,),
        in_specs=[row, _resident((1, d)), _resident(wg.shape), _resident(wu.shape), _resident(wd.shape)],
        out_specs=row,
        out_shape=jax.ShapeDtypeStruct((t, d), F32),
        compiler_params=pltpu.CompilerParams(dimension_semantics=("arbitrary",), vmem_limit_bytes=VMEM_LIMIT),
        name="ffn",
    )(x2d, norm_w, wg, wu, wd)


def _qk_norm(t, w2, scale):
    lo = lax.broadcasted_iota(jnp.int32, t.shape, 1) < ATT_QK_DIM
    x2 = t * t
    s_lo = jnp.sum(jnp.where(lo, x2, 0.0), axis=-1, keepdims=True)
    s_hi = jnp.sum(jnp.where(lo, 0.0, x2), axis=-1, keepdims=True)
    r = lax.rsqrt(jnp.where(lo, s_lo, s_hi) * (1.0 / ATT_QK_DIM) + NORM_EPS)
    return t * r * (w2 * scale)


def _proj_kernel(x_ref, nw_ref, wt_ref, qw_ref, kw_ref, z_ref, xbc_ref, dt_ref, q_ref, k_ref, v_ref, *, offsets):
    o_z, o_xbc, o_dt, o_q, o_k, o_v, o_end = offsets
    h = _rms_norm(x_ref[...], nw_ref[...]).astype(BF16)

    def project(w_rows):
        return lax.dot_general(h, w_rows.astype(BF16), (((1,), (1,)), ((), ())), preferred_element_type=F32)

    n_heads = (o_k - o_q) // LANES
    for lo_row, g_ref, o_ref, scale in ((o_q, qw_ref, q_ref, Q_SCALE), (o_k, kw_ref, k_ref, 1.0)):
        t = project(wt_ref[lo_row:lo_row + o_k - o_q, :])
        for hd in range(n_heads):
            sl = slice(hd * LANES, (hd + 1) * LANES)
            o_ref[:, sl] = _qk_norm(t[:, sl], g_ref[...], scale).astype(BF16)
    w_dt = wt_ref[o_dt:o_dt + LANES, :]
    w_dt = jnp.where(lax.broadcasted_iota(jnp.int32, w_dt.shape, 0) < o_q - o_dt, w_dt, 0.0)
    dt_ref[...] = project(w_dt)
    v_ref[...] = project(wt_ref[o_v:o_end, :]).astype(BF16)
    z_ref[...] = project(wt_ref[o_z:o_xbc, :]).astype(BF16)
    xbc_ref[...] = project(wt_ref[o_xbc:o_dt, :]).astype(BF16)


def _proj(x2d, norm_w, wt_stack, layer, offsets, qw2, kw2):
    t, d = x2d.shape
    w_spec = pl.BlockSpec((None,) + wt_stack.shape[1:], lambda i: (layer, 0, 0), pipeline_mode=pl.Buffered(1))
    tm = min(FFN_ROWS, t)
    row = lambda n: pl.BlockSpec((tm, n), lambda i: (i, 0))
    o_z, o_xbc, o_dt, o_q, o_k, o_v, o_end = offsets
    assert all(o % 16 == 0 for o in offsets) and o_q - o_dt <= LANES and o_dt + LANES <= o_end
    widths = (o_xbc - o_z, o_dt - o_xbc, LANES, o_k - o_q, o_v - o_k, o_end - o_v)
    dtypes = (BF16, BF16, F32, BF16, BF16, BF16)
    return pl.pallas_call(
        functools.partial(_proj_kernel, offsets=offsets),
        grid=(t // tm,),
        in_specs=[row(d), _resident((1, d)), w_spec, _resident((1, LANES)), _resident((1, LANES))],
        out_specs=[row(n) for n in widths],
        out_shape=[jax.ShapeDtypeStruct((t, n), dt) for n, dt in zip(widths, dtypes)],
        compiler_params=pltpu.CompilerParams(dimension_semantics=("arbitrary",), vmem_limit_bytes=VMEM_LIMIT),
        name="proj",
    )(x2d, norm_w, wt_stack, qw2, kw2)


def _lane_bcast(x, col):
    return jnp.broadcast_to(x[:, col:col + 1], x.shape)


def _expand_heads(x, pair):
    lo = lax.broadcasted_iota(jnp.int32, x.shape, 1) < SSD_HEAD_DIM
    return jnp.where(lo, _lane_bcast(x, 2 * pair), _lane_bcast(x, 2 * pair + 1))


def _ssd_step(first, z_ref, xbc_ref, halo_ref, dt_ref, cw_ref, cb_ref, dtb_ref, alog_ref, dskip_ref, nw_ref,
              y_ref, state_ref):
    L = SSD_CHUNK
    n_chunks = xbc_ref.shape[0] // L
    width = z_ref.shape[-1]
    n_pairs = width // LANES
    heads_per_group = (width // SSD_HEAD_DIM) // SSD_GROUPS
    gw = width // SSD_GROUPS

    out_t = lax.broadcasted_iota(jnp.int32, (L, CONV_HALO + L), 0)
    in_t = lax.broadcasted_iota(jnp.int32, (L, CONV_HALO + L), 1) - CONV_HALO
    shifts = [jnp.where(in_t == out_t - (SSD_CONV - 1 - kk), 1.0, 0.0).astype(BF16) for kk in range(SSD_CONV - 1)]
    causal = lax.broadcasted_iota(jnp.int32, (L, L), 0) >= lax.broadcasted_iota(jnp.int32, (L, L), 1)
    tril = causal.astype(F32)
    lo = lax.broadcasted_iota(jnp.int32, (L, LANES), 1) < SSD_HEAD_DIM
    zero = jnp.zeros((L, LANES), BF16)
    head_of_lane = lax.broadcasted_iota(jnp.int32, (LANES, width), 1) // SSD_HEAD_DIM
    expand = (lax.broadcasted_iota(jnp.int32, (LANES, width), 0) == head_of_lane).astype(BF16)
    neg_a = -jnp.exp(alog_ref[...])

    halo = halo_ref[...]
    tail = jnp.where(first, jnp.zeros_like(halo), halo)
    for ci in range(n_chunks):
        rows = slice(ci * L, (ci + 1) * L)

        x_cur = xbc_ref[rows, :]
        x_ext = jnp.concatenate([tail, x_cur], axis=0)
        tail = x_cur[L - CONV_HALO:, :]
        conv = cb_ref[...] + cw_ref[SSD_CONV - 1:SSD_CONV, :] * x_cur.astype(F32)
        for kk in range(SSD_CONV - 1):
            conv = conv + cw_ref[kk:kk + 1, :] * jnp.dot(shifts[kk], x_ext, preferred_element_type=F32)
        act = _silu(conv)
        yield
        xs = act[:, :width]
        bm = act[:, width:width + SSD_GROUPS * SSD_STATE].astype(BF16)
        cm = act[:, width + SSD_GROUPS * SSD_STATE:]

        raw = dt_ref[rows, :] + dtb_ref[...]
        dt = jnp.maximum(raw, 0.0) + jnp.log1p(jnp.exp(-jnp.abs(raw)))
        a_cs = LOG2_E * jnp.dot(tril, dt * neg_a, preferred_element_type=F32, precision=lax.Precision.HIGHEST)
        a_last = a_cs[L - 1:L, :]
        f = dt * jnp.exp2(a_last - a_cs)
        a_src_t = a_cs.T - jnp.log2(dt).T

        yield
        cm_bf = cm.astype(BF16)
        cb = [lax.dot_general(cm_bf[:, g * SSD_STATE:(g + 1) * SSD_STATE], bm[:, g * SSD_STATE:(g + 1) * SSD_STATE],
                              (((1,), (1,)), ((), ())), preferred_element_type=F32) for g in range(SSD_GROUPS)]
        xs_bf = xs.astype(BF16)
        y_tiles = []
        for pair in range(n_pairs):
            g = (2 * pair) // heads_per_group
            cm_g = cm[:, g * SSD_STATE:(g + 1) * SSD_STATE]
            sl = slice(pair * LANES, (pair + 1) * LANES)
            x_pair = xs_bf[:, sl]
            prev_pair = state_ref[:, sl].astype(BF16)
            y_pair = None
            for half in range(2):
                hd = 2 * pair + half
                a_col = _lane_bcast(a_cs, hd)
                seg = a_col - a_src_t[hd:hd + 1, :]
                m_h = (cb[g] * jnp.exp2(jnp.where(causal, seg, -jnp.inf))).astype(BF16)
                c_h = (cm_g * jnp.exp2(a_col)).astype(BF16)
                keep = lo if half == 0 else jnp.logical_not(lo)
                rhs = jnp.concatenate([jnp.where(keep, x_pair, zero), jnp.where(keep, prev_pair, zero)], axis=0)
                yh = jnp.dot(jnp.concatenate([m_h, c_h], axis=1), rhs, preferred_element_type=F32)
                y_pair = yh if y_pair is None else y_pair + yh
            y_tiles.append(y_pair)
            yield

        f_hi = f.astype(BF16)
        f_lo = (f - f_hi.astype(F32)).astype(BF16)
        f_exp = (jnp.dot(f_hi, expand, preferred_element_type=F32)
                 + jnp.dot(f_lo, expand, preferred_element_type=F32))
        ea_last = jnp.exp2(a_cs[L - 8:L, :])
        for pair in range(n_pairs):
            g = (2 * pair) // heads_per_group
            sl = slice(pair * LANES, (pair + 1) * LANES)
            bm_g = bm[:, g * SSD_STATE:(g + 1) * SSD_STATE]
            xw = (xs[:, sl] * f_exp[:, sl]).astype(BF16)
            new = lax.dot_general(bm_g, xw, (((0,), (0,)), ((), ())), preferred_element_type=F32)
            chunk_decay = _expand_heads(ea_last, pair)[7:8, :]
            state_ref[:, sl] = state_ref[:, sl] * chunk_decay + new
            if pair % 4 == 3:
                yield

        y = jnp.concatenate(y_tiles, axis=1) + xs * dskip_ref[...]
        gated = y * _silu(z_ref[rows, :].astype(F32))
        for g in range(SSD_GROUPS):
            sl = slice(g * gw, (g + 1) * gw)
            y_ref[rows, sl] = _rms_norm(gated[:, sl], nw_ref[:, sl]).astype(BF16)
        yield


def _attn_kernel(q_ref, k_ref, v_ref, bound_ref, lq1_ref, lk1_ref, lq2_ref, lk2_ref, sw_ref, o_ref, s_ref, acc_ref,
                 qt_ref, *, lambda_init, bounded, interleave=()):
    tq = o_ref.shape[0]
    tk = s_ref.shape[2]
    n_kb = tq // tk
    tc = ATT_COL_TILE
    n_ct = tq // tc
    n_q = q_ref.shape[0] // tq
    qi = pl.program_id(2)
    cols = [slice(c * tc, (c + 1) * tc) for c in range(n_ct)]

    def stage_q_t(blk, qslot):
        q = q_ref[pl.ds(pl.multiple_of(blk * tq, tq), tq), :]
        lo = lax.broadcasted_iota(jnp.int32, q.shape, 1) < ATT_QK_DIM
        zero = jnp.zeros_like(q)
        qt_ref[qslot, 0] = jnp.where(lo, q, zero).T
        qt_ref[qslot, 1] = jnp.where(lo, zero, q).T

    def keys(row0):
        return k_ref[pl.ds(pl.multiple_of(row0, tk), tk), :]

    ones_rows = jnp.ones((ATT_SUM_ROWS, tk), BF16)

    def values(row0):
        return jnp.concatenate([v_ref[pl.ds(pl.multiple_of(row0, tk), tk), :].T, ones_rows], axis=0)

    def qk(k, slot, qslot, c):
        for mp in range(2):
            s_ref[slot, mp, :, cols[c]] = jnp.dot(k, qt_ref[qslot, mp, :, cols[c]], preferred_element_type=F32)

    def softmax_pv(v_ext, slot, m, c, mask=None):
        m = [list(mm) for mm in m]
        for mp in range(2):
            s = s_ref[slot, mp, :, cols[c]]
            if mask is not None:
                s = jnp.where(mask, s, -jnp.inf)
            if bounded:
                p = jnp.exp2(s - bound_ref[...]).astype(BF16)
                acc_ref[mp, :, cols[c]] = acc_ref[mp, :, cols[c]] + jnp.dot(v_ext, p, preferred_element_type=F32)
                continue
            m_old = m[mp][c]
            m_new = jnp.maximum(m_old, jnp.max(s, axis=0, keepdims=True))
            p = jnp.exp2(s - m_new).astype(BF16)
            acc_ref[mp, :, cols[c]] = (jnp.exp2(m_old - m_new) * acc_ref[mp, :, cols[c]]
                                       + jnp.dot(v_ext, p, preferred_element_type=F32))
            m[mp][c] = m_new
        return tuple(tuple(mm) for mm in m)

    q_cur = qi % 2
    q_nxt = 1 - q_cur

    @pl.when(qi == 0)
    def _():
        stage_q_t(0, 0)
        k0 = keys(0)
        for c in range(n_ct):
            qk(k0, 0, 0, c)

    acc_ref[...] = jnp.zeros(acc_ref.shape, F32)
    neg_inf = () if bounded else tuple(jnp.full((1, tc), -jnp.inf, F32) for _ in range(n_ct))

    def pair(i, m):
        row0 = i * 2 * tk
        k_next, v_cur = keys(row0 + tk), values(row0)
        for c in range(n_ct):
            qk(k_next, 1, q_cur, c)
            m = softmax_pv(v_cur, 0, m, c)
        k_next, v_cur = keys(row0 + 2 * tk), values(row0 + tk)
        for c in range(n_ct):
            qk(k_next, 0, q_cur, c)
            m = softmax_pv(v_cur, 1, m, c)
        return m

    m = lax.fori_loop(0, qi * (n_kb // 2), pair, (neg_inf, neg_inf))

    row0 = qi * tq
    r = lax.broadcasted_iota(jnp.int32, (tk, tc), 0)
    col = lax.broadcasted_iota(jnp.int32, (tk, tc), 1)
    n_diag = tk // tc
    masks = [r <= col + j * tc for j in range(n_diag)]
    stage_q_t(jnp.minimum(qi + 1, n_q - 1), q_nxt)
    others = iter(interleave)

    lam = (jnp.exp(jnp.sum(lq1_ref[...] * lk1_ref[...], axis=-1, keepdims=True))
           - jnp.exp(jnp.sum(lq2_ref[...] * lk2_ref[...], axis=-1, keepdims=True)) + lambda_init)

    def finish(c):
        o_t = (acc_ref[0, :ATT_V_DIM, cols[c]] / acc_ref[0, ATT_V_DIM:ATT_V_DIM + 1, cols[c]]
               - lam * (acc_ref[1, :ATT_V_DIM, cols[c]] / acc_ref[1, ATT_V_DIM:ATT_V_DIM + 1, cols[c]]))
        o_ref[cols[c], :] = (_rms_norm(o_t.T, sw_ref[...]) * (1.0 - lambda_init)).astype(BF16)

    for d in range(n_kb):
        slot = d % 2
        last = d == n_kb - 1
        v_cur = values(row0 + d * tk)
        k_next = keys(0) if last else keys(row0 + (d + 1) * tk)
        for c in range(n_ct):
            if last:
                qk(k_next, 0, q_nxt, c)
            elif c >= (d + 1) * n_diag:
                qk(k_next, 1 - slot, q_cur, c)
            if c >= d * n_diag:
                m = softmax_pv(v_cur, slot, m, c, mask=masks[c - d * n_diag] if c < (d + 1) * n_diag else None)
            if c // n_diag == d:
                finish(c)
            next(others, None)
    for _ in others:
        pass


def _mix_kernel(z_ref, xbc_ref, halo_ref, dt_ref, cw_ref, cb_ref, dtb_ref, alog_ref, dskip_ref, ssd_nw_ref,
                q_ref, k_ref, v_ref, bound_ref, lq1_ref, lk1_ref, lq2_ref, lk2_ref, sw_ref,
                y_ref, o_ref, state_ref, s_ref, acc_ref, qt_ref, *, lambda_init, bounded):
    first = jnp.logical_and(pl.program_id(1) == 0, pl.program_id(2) == 0)

    @pl.when(first)
    def _():
        state_ref[...] = jnp.zeros(state_ref.shape, F32)

    ssd = _ssd_step(first, z_ref, xbc_ref, halo_ref, dt_ref, cw_ref, cb_ref, dtb_ref, alog_ref, dskip_ref, ssd_nw_ref,
                    y_ref, state_ref)
    _attn_kernel(q_ref, k_ref, v_ref, bound_ref, lq1_ref, lk1_ref, lq2_ref, lk2_ref, sw_ref, o_ref, s_ref, acc_ref,
                 qt_ref, lambda_init=lambda_init, bounded=bounded, interleave=ssd)


def _mix(z, xbc, dt, conv_w, conv_b, dt_bias, a_log, d_skip_exp, ssd_norm_w,
         q, k, v, bound, lq1, lk1, lq2, lk2, subln_w, lambda_init, bounded):
    bsz, s, width = q.shape
    n_heads = width // ATT_V_DIM
    tq = min(ATT_BLOCK_Q, s)
    n_q = s // tq
    rows = s // (n_heads * n_q)
    assert rows % SSD_CHUNK == 0 and rows % CONV_HALO == 0
    ssd_width, cw = z.shape[-1], xbc.shape[-1]
    slab = lambda n: pl.BlockSpec((None, rows, n), lambda b, h, i: (b, h * n_q + i, 0))
    halo = pl.BlockSpec((None, CONV_HALO, cw),
                        lambda b, h, i: (b, jnp.maximum((h * n_q + i) * (rows // CONV_HALO) - 1, 0), 0))
    o_spec = pl.BlockSpec((None, tq, LANES), lambda b, h, i: (b, i, h))
    head_spec = pl.BlockSpec((None, s, LANES), lambda b, h, i: (b, 0, h))
    small = _resident((1, ATT_QK_DIM))
    return pl.pallas_call(
        functools.partial(_mix_kernel, lambda_init=lambda_init, bounded=bounded),
        grid=(bsz, n_heads, n_q),
        in_specs=[slab(ssd_width), slab(cw), halo, slab(LANES), _resident(conv_w.shape), _resident((1, cw)),
                  _resident((1, LANES)), _resident((1, LANES)), _resident((1, ssd_width)), _resident((1, ssd_width)),
                  head_spec, head_spec, head_spec, _resident((1, 1)), small, small, small, small,
                  _resident((1, ATT_V_DIM))],
        out_specs=[slab(ssd_width), o_spec],
        out_shape=[jax.ShapeDtypeStruct((bsz, s, ssd_width), BF16), jax.ShapeDtypeStruct((bsz, s, width), BF16)],
        scratch_shapes=[pltpu.VMEM((SSD_STATE, ssd_width), F32),
                        pltpu.VMEM((2, 2, min(ATT_BLOCK_K, tq // 2), tq), F32),
                        pltpu.VMEM((2, ATT_V_DIM + ATT_SUM_ROWS, tq), F32),
                        pltpu.VMEM((2, 2, LANES, tq), BF16)],
        compiler_params=pltpu.CompilerParams(dimension_semantics=("arbitrary",) * 3, vmem_limit_bytes=VMEM_LIMIT),
        name="mix",
    )(z, xbc, xbc, dt, conv_w, conv_b, dt_bias, a_log, d_skip_exp, ssd_norm_w,
      q, k, v, bound, lq1, lk1, lq2, lk2, subln_w)


def _out_kernel(x_ref, ys_ref, ya_ref, wos_ref, woa_ref, nw_ref, wg_ref, wu_ref, wd_ref, o_ref):
    x = (x_ref[...] + jnp.dot(ys_ref[...], wos_ref[...], preferred_element_type=F32)
         + jnp.dot(ya_ref[...], woa_ref[...], preferred_element_type=F32))
    h = _rms_norm(x, nw_ref[...]).astype(BF16)
    o_ref[...] = x + 0.5 * _swiglu(h, wg_ref, wu_ref, wd_ref)


def _out(x2d, y_ssd, y_att, wo_ssd, wo_att, norm_w, wg, wu, wd):
    t, d = x2d.shape
    tm = min(FFN_ROWS, t)
    row = lambda n: pl.BlockSpec((tm, n), lambda i: (i, 0))
    return pl.pallas_call(
        _out_kernel,
        grid=(t // tm,),
        in_specs=[row(d), row(y_ssd.shape[1]), row(y_att.shape[1]), _resident(wo_ssd.shape), _resident(wo_att.shape),
                  _resident((1, d)), _resident(wg.shape), _resident(wu.shape), _resident(wd.shape)],
        out_specs=row(d),
        out_shape=jax.ShapeDtypeStruct((t, d), F32),
        compiler_params=pltpu.CompilerParams(dimension_semantics=("arbitrary",), vmem_limit_bytes=VMEM_LIMIT),
        name="out_ffn",
    )(x2d, y_ssd, y_att, wo_ssd, wo_att, norm_w, wg, wu, wd)


def _pad_lanes(v):
    return jnp.zeros((1, LANES), F32).at[0, :v.shape[0]].set(v.astype(F32))


def kernel(x, ffn1_norm_w, ffn1_w_gate, ffn1_w_up, ffn1_w_down, mix_norm_w, w_in, conv_w, conv_b, dt_bias, a_log,
           d_skip, ssd_norm_w, q_norm_w, k_norm_w, lambda_q1, lambda_k1, lambda_q2, lambda_k2, attn_subln_w, w_out,
           ffn2_norm_w, ffn2_w_gate, ffn2_w_up, ffn2_w_down):
    bsz, s, d = x.shape
    depth = w_in.shape[0]
    ssd_heads = dt_bias.shape[1]
    ssd_width = ssd_heads * SSD_HEAD_DIM
    conv_ch = conv_w.shape[2]
    att_width = w_out.shape[1] - ssd_width
    qk_width = (att_width // ATT_V_DIM) * 2 * ATT_QK_DIM
    sizes = (ssd_width, conv_ch, ssd_heads, qk_width, qk_width, att_width)
    offs = [0]
    for n in sizes:
        offs.append(offs[-1] + n)
    row = lambda v: v.astype(F32).reshape(1, -1)

    x2d = x.reshape(bsz * s, d)
    for l in range(depth):
        lambda_init = 0.8 - 0.6 * math.exp(-0.3 * l)
        wo_ssd, wo_att = w_out[l][:ssd_width].astype(BF16), w_out[l][ssd_width:].astype(BF16)

        x1 = _ffn(x2d, row(ffn1_norm_w[l]), ffn1_w_gate[l], ffn1_w_up[l], ffn1_w_down[l])
        z, xbc, dt, q, k, v = _proj(x1, row(mix_norm_w[l]), jnp.swapaxes(w_in, 1, 2), l, tuple(offs),
                                    row(jnp.tile(q_norm_w[l], 2)), row(jnp.tile(k_norm_w[l], 2)))
        score_bound = (ATT_QK_DIM * Q_SCALE * ATT_BOUND_MARGIN * jnp.max(jnp.abs(q_norm_w[l]))
                       * jnp.max(jnp.abs(k_norm_w[l]))).astype(F32)
        mix_args = (z.reshape(bsz, s, -1), xbc.reshape(bsz, s, -1), dt.reshape(bsz, s, -1),
                    conv_w[l].astype(F32), row(conv_b[l]), _pad_lanes(dt_bias[l]), _pad_lanes(a_log[l]),
                    row(jnp.repeat(d_skip[l], SSD_HEAD_DIM)), row(ssd_norm_w[l]),
                    q.reshape(bsz, s, -1), k.reshape(bsz, s, -1), v.reshape(bsz, s, -1), score_bound.reshape(1, 1),
                    row(lambda_q1[l]), row(lambda_k1[l]), row(lambda_q2[l]), row(lambda_k2[l]),
                    row(attn_subln_w[l]))
        y_ssd, y_att = lax.cond(score_bound <= ATT_MAX_BOUND,
                                lambda *a: _mix(*a, lambda_init, True), lambda *a: _mix(*a, lambda_init, False),
                                *mix_args)
        x2d = _out(x1, y_ssd.reshape(bsz * s, -1), y_att.reshape(bsz * s, -1), wo_ssd, wo_att,
                   row(ffn2_norm_w[l]), ffn2_w_gate[l], ffn2_w_up[l], ffn2_w_down[l])
    return x2d.reshape(bsz, s, d)
```

```python
import functools
import math

import jax
import jax.numpy as jnp
from jax import lax
from jax.experimental import pallas as pl
from jax.experimental.pallas import tpu as pltpu

F32 = jnp.float32
BF16 = jnp.bfloat16

NORM_EPS = 1e-6
SSD_HEAD_DIM = 64
SSD_GROUPS = 2
SSD_STATE = 128
SSD_CONV = 4
SSD_CHUNK = 128
ATT_QK_DIM = 64
ATT_V_DIM = 128
LANES = 128
CONV_HALO = 16
LOG2_E = math.log2(math.e)
Q_SCALE = ATT_QK_DIM ** -0.5 * LOG2_E
ATT_BOUND_MARGIN = 1.01
ATT_MAX_BOUND = 60.0

FFN_ROWS = 512
FFN_COLS = 256
ATT_BLOCK_Q = 2048
ATT_BLOCK_K = 512
ATT_COL_TILE = 256
ATT_SUM_ROWS = 16
VMEM_LIMIT = 56 * 1024 * 1024


def _resident(shape):
    nd = len(shape)
    return pl.BlockSpec(shape, lambda *_: (0,) * nd, pipeline_mode=pl.Buffered(1))


def _rms_norm(x, w):
    return x * lax.rsqrt(jnp.mean(x * x, axis=-1, keepdims=True) + NORM_EPS) * w


def _silu(x):
    h = 0.5 * x
    return h + h * jnp.tanh(h)


def _swiglu(h, wg_ref, wu_ref, wd_ref):
    d_ff = wg_ref.shape[1]
    acc = None
    for c in range(d_ff // FFN_COLS):
        sl = slice(c * FFN_COLS, (c + 1) * FFN_COLS)
        g = jnp.dot(h, wg_ref[:, sl].astype(BF16), preferred_element_type=F32)
        u = jnp.dot(h, wu_ref[:, sl].astype(BF16), preferred_element_type=F32)
        a = (_silu(g) * u).astype(BF16)
        d = jnp.dot(a, wd_ref[sl, :].astype(BF16), preferred_element_type=F32)
        acc = d if acc is None else acc + d
    return acc


def _ffn_kernel(x_ref, nw_ref, wg_ref, wu_ref, wd_ref, o_ref):
    x = x_ref[...]
    h = _rms_norm(x, nw_ref[...]).astype(BF16)
    o_ref[...] = x + 0.5 * _swiglu(h, wg_ref, wu_ref, wd_ref)


def _ffn(x2d, norm_w, wg, wu, wd):
    t, d = x2d.shape
    tm = min(FFN_ROWS, t)
    row = pl.BlockSpec((tm, d), lambda i: (i, 0))
    return pl.pallas_call(
        _ffn_kernel,
        grid=(t // tm,),
        in_specs=[row, _resident((1, d)), _resident(wg.shape), _resident(wu.shape), _resident(wd.shape)],
        out_specs=row,
        out_shape=jax.ShapeDtypeStruct((t, d), F32),
        compiler_params=pltpu.CompilerParams(dimension_semantics=("arbitrary",), vmem_limit_bytes=VMEM_LIMIT),
        name="ffn",
    )(x2d, norm_w, wg, wu, wd)


def _qk_norm(t, w2, scale):
    lo = lax.broadcasted_iota(jnp.int32, t.shape, 1) < ATT_QK_DIM
    x2 = t * t
    s_lo = jnp.sum(jnp.where(lo, x2, 0.0), axis=-1, keepdims=True)
    s_hi = jnp.sum(jnp.where(lo, 0.0, x2), axis=-1, keepdims=True)
    r = lax.rsqrt(jnp.where(lo, s_lo, s_hi) * (1.0 / ATT_QK_DIM) + NORM_EPS)
    return t * r * (w2 * scale)


def _proj_kernel(x_ref, nw_ref, wt_ref, qw_ref, kw_ref, z_ref, xbc_ref, dt_ref, q_ref, k_ref, v_ref, *, offsets):
    o_z, o_xbc, o_dt, o_q, o_k, o_v, o_end = offsets
    h = _rms_norm(x_ref[...], nw_ref[...]).astype(BF16)

    def project(w_rows):
        return lax.dot_general(h, w_rows.astype(BF16), (((1,), (1,)), ((), ())), preferred_element_type=F32)

    n_heads = (o_k - o_q) // LANES
    for lo_row, g_ref, o_ref, scale in ((o_q, qw_ref, q_ref, Q_SCALE), (o_k, kw_ref, k_ref, 1.0)):
        t = project(wt_ref[lo_row:lo_row + o_k - o_q, :])
        for hd in range(n_heads):
            sl = slice(hd * LANES, (hd + 1) * LANES)
            o_ref[:, sl] = _qk_norm(t[:, sl], g_ref[...], scale).astype(BF16)
    w_dt = wt_ref[o_dt:o_dt + LANES, :]
    w_dt = jnp.where(lax.broadcasted_iota(jnp.int32, w_dt.shape, 0) < o_q - o_dt, w_dt, 0.0)
    dt_ref[...] = project(w_dt)
    v_ref[...] = project(wt_ref[o_v:o_end, :]).astype(BF16)
    z_ref[...] = project(wt_ref[o_z:o_xbc, :]).astype(BF16)
    xbc_ref[...] = project(wt_ref[o_xbc:o_dt, :]).astype(BF16)


def _proj(x2d, norm_w, wt_stack, layer, offsets, qw2, kw2):
    t, d = x2d.shape
    w_spec = pl.BlockSpec((None,) + wt_stack.shape[1:], lambda i: (layer, 0, 0), pipeline_mode=pl.Buffered(1))
    tm = min(FFN_ROWS, t)
    row = lambda n: pl.BlockSpec((tm, n), lambda i: (i, 0))
    o_z, o_xbc, o_dt, o_q, o_k, o_v, o_end = offsets
    assert all(o % 16 == 0 for o in offsets) and o_q - o_dt <= LANES and o_dt + LANES <= o_end
    widths = (o_xbc - o_z, o_dt - o_xbc, LANES, o_k - o_q, o_v - o_k, o_end - o_v)
    dtypes = (BF16, BF16, F32, BF16, BF16, BF16)
    return pl.pallas_call(
        functools.partial(_proj_kernel, offsets=offsets),
        grid=(t // tm,),
        in_specs=[row(d), _resident((1, d)), w_spec, _resident((1, LANES)), _resident((1, LANES))],
        out_specs=[row(n) for n in widths],
        out_shape=[jax.ShapeDtypeStruct((t, n), dt) for n, dt in zip(widths, dtypes)],
        compiler_params=pltpu.CompilerParams(dimension_semantics=("arbitrary",), vmem_limit_bytes=VMEM_LIMIT),
        name="proj",
    )(x2d, norm_w, wt_stack, qw2, kw2)


def _lane_bcast(x, col):
    return jnp.broadcast_to(x[:, col:col + 1], x.shape)


def _expand_heads(x, pair):
    lo = lax.broadcasted_iota(jnp.int32, x.shape, 1) < SSD_HEAD_DIM
    return jnp.where(lo, _lane_bcast(x, 2 * pair), _lane_bcast(x, 2 * pair + 1))


def _ssd_step(first, z_ref, xbc_ref, halo_ref, dt_ref, cw_ref, cb_ref, dtb_ref, alog_ref, dskip_ref, nw_ref,
              y_ref, state_ref):
    L = SSD_CHUNK
    n_chunks = xbc_ref.shape[0] // L
    width = z_ref.shape[-1]
    n_pairs = width // LANES
    heads_per_group = (width // SSD_HEAD_DIM) // SSD_GROUPS
    gw = width // SSD_GROUPS

    out_t = lax.broadcasted_iota(jnp.int32, (L, CONV_HALO + L), 0)
    in_t = lax.broadcasted_iota(jnp.int32, (L, CONV_HALO + L), 1) - CONV_HALO
    shifts = [jnp.where(in_t == out_t - (SSD_CONV - 1 - kk), 1.0, 0.0).astype(BF16) for kk in range(SSD_CONV - 1)]
    causal = lax.broadcasted_iota(jnp.int32, (L, L), 0) >= lax.broadcasted_iota(jnp.int32, (L, L), 1)
    tril = causal.astype(F32)
    lo = lax.broadcasted_iota(jnp.int32, (L, LANES), 1) < SSD_HEAD_DIM
    zero = jnp.zeros((L, LANES), BF16)
    head_of_lane = lax.broadcasted_iota(jnp.int32, (LANES, width), 1) // SSD_HEAD_DIM
    expand = (lax.broadcasted_iota(jnp.int32, (LANES, width), 0) == head_of_lane).astype(BF16)
    neg_a = -jnp.exp(alog_ref[...])

    halo = halo_ref[...]
    tail = jnp.where(first, jnp.zeros_like(halo), halo)
    for ci in range(n_chunks):
        rows = slice(ci * L, (ci + 1) * L)

        x_cur = xbc_ref[rows, :]
        x_ext = jnp.concatenate([tail, x_cur], axis=0)
        tail = x_cur[L - CONV_HALO:, :]
        conv = cb_ref[...] + cw_ref[SSD_CONV - 1:SSD_CONV, :] * x_cur.astype(F32)
        for kk in range(SSD_CONV - 1):
            conv = conv + cw_ref[kk:kk + 1, :] * jnp.dot(shifts[kk], x_ext, preferred_element_type=F32)
        act = _silu(conv)
        yield
        xs = act[:, :width]
        bm = act[:, width:width + SSD_GROUPS * SSD_STATE].astype(BF16)
        cm = act[:, width + SSD_GROUPS * SSD_STATE:]

        raw = dt_ref[rows, :] + dtb_ref[...]
        dt = jnp.maximum(raw, 0.0) + jnp.log1p(jnp.exp(-jnp.abs(raw)))
        a_cs = LOG2_E * jnp.dot(tril, dt * neg_a, preferred_element_type=F32, precision=lax.Precision.HIGHEST)
        a_last = a_cs[L - 1:L, :]
        f = dt * jnp.exp2(a_last - a_cs)
        a_src_t = a_cs.T - jnp.log2(dt).T

        yield
        cm_bf = cm.astype(BF16)
        cb = [lax.dot_general(cm_bf[:, g * SSD_STATE:(g + 1) * SSD_STATE], bm[:, g * SSD_STATE:(g + 1) * SSD_STATE],
                              (((1,), (1,)), ((), ())), preferred_element_type=F32) for g in range(SSD_GROUPS)]
        xs_bf = xs.astype(BF16)
        y_tiles = []
        for pair in range(n_pairs):
            g = (2 * pair) // heads_per_group
            cm_g = cm[:, g * SSD_STATE:(g + 1) * SSD_STATE]
            sl = slice(pair * LANES, (pair + 1) * LANES)
            x_pair = xs_bf[:, sl]
            prev_pair = state_ref[:, sl].astype(BF16)
            y_pair = None
            for half in range(2):
                hd = 2 * pair + half
                a_col = _lane_bcast(a_cs, hd)
                seg = a_col - a_src_t[hd:hd + 1, :]
                m_h = (cb[g] * jnp.exp2(jnp.where(causal, seg, -jnp.inf))).astype(BF16)
                c_h = (cm_g * jnp.exp2(a_col)).astype(BF16)
                keep = lo if half == 0 else jnp.logical_not(lo)
                rhs = jnp.concatenate([jnp.where(keep, x_pair, zero), jnp.where(keep, prev_pair, zero)], axis=0)
                yh = jnp.dot(jnp.concatenate([m_h, c_h], axis=1), rhs, preferred_element_type=F32)
                y_pair = yh if y_pair is None else y_pair + yh
            y_tiles.append(y_pair)
            yield

        f_hi = f.astype(BF16)
        f_lo = (f - f_hi.astype(F32)).astype(BF16)
        f_exp = (jnp.dot(f_hi, expand, preferred_element_type=F32)
                 + jnp.dot(f_lo, expand, preferred_element_type=F32))
        ea_last = jnp.exp2(a_cs[L - 8:L, :])
        for pair in range(n_pairs):
            g = (2 * pair) // heads_per_group
            sl = slice(pair * LANES, (pair + 1) * LANES)
            bm_g = bm[:, g * SSD_STATE:(g + 1) * SSD_STATE]
            xw = (xs[:, sl] * f_exp[:, sl]).astype(BF16)
            new = lax.dot_general(bm_g, xw, (((0,), (0,)), ((), ())), preferred_element_type=F32)
            chunk_decay = _expand_heads(ea_last, pair)[7:8, :]
            state_ref[:, sl] = state_ref[:, sl] * chunk_decay + new
            if pair % 4 == 3:
                yield

        y = jnp.concatenate(y_tiles, axis=1) + xs * dskip_ref[...]
        gated = y * _silu(z_ref[rows, :].astype(F32))
        for g in range(SSD_GROUPS):
            sl = slice(g * gw, (g + 1) * gw)
            y_ref[rows, sl] = _rms_norm(gated[:, sl], nw_ref[:, sl]).astype(BF16)
        yield


def _attn_kernel(q_ref, k_ref, v_ref, qn_ref, kn_ref, bound_ref, lq1_ref, lk1_ref, lq2_ref, lk2_ref, sw_ref, o_ref,
                 s_ref, acc_ref, qt_ref, *, lambda_init, bounded, interleave=()):
    tq = o_ref.shape[0]
    tk = s_ref.shape[2]
    n_kb = tq // tk
    tc = ATT_COL_TILE
    n_ct = tq // tc
    n_q = q_ref.shape[0] // tq
    qi = pl.program_id(2)
    step = (pl.program_id(0) * pl.num_programs(1) + pl.program_id(1)) * n_q + qi
    head_done = qi == n_q - 1
    cols = [slice(c * tc, (c + 1) * tc) for c in range(n_ct)]

    def stage_q_t(q, qslot):
        lo = lax.broadcasted_iota(jnp.int32, q.shape, 1) < ATT_QK_DIM
        zero = jnp.zeros_like(q)
        qt_ref[qslot, 0] = jnp.where(lo, q, zero).T
        qt_ref[qslot, 1] = jnp.where(lo, zero, q).T

    def keys(row0):
        return k_ref[pl.ds(pl.multiple_of(row0, tk), tk), :]

    ones_rows = jnp.ones((ATT_SUM_ROWS, tk), BF16)

    def values(row0):
        return jnp.concatenate([v_ref[pl.ds(pl.multiple_of(row0, tk), tk), :].T, ones_rows], axis=0)

    def qk(k, slot, qslot, c):
        for mp in range(2):
            s_ref[slot, mp, :, cols[c]] = jnp.dot(k, qt_ref[qslot, mp, :, cols[c]], preferred_element_type=F32)

    def softmax_pv(v_ext, slot, m, c, mask=None):
        m = [list(mm) for mm in m]
        for mp in range(2):
            s = s_ref[slot, mp, :, cols[c]]
            if mask is not None:
                s = jnp.where(mask, s, -jnp.inf)
            if bounded:
                p = jnp.exp2(s - bound_ref[...]).astype(BF16)
                acc_ref[mp, :, cols[c]] = acc_ref[mp, :, cols[c]] + jnp.dot(v_ext, p, preferred_element_type=F32)
                continue
            m_old = m[mp][c]
            m_new = jnp.maximum(m_old, jnp.max(s, axis=0, keepdims=True))
            p = jnp.exp2(s - m_new).astype(BF16)
            acc_ref[mp, :, cols[c]] = (jnp.exp2(m_old - m_new) * acc_ref[mp, :, cols[c]]
                                       + jnp.dot(v_ext, p, preferred_element_type=F32))
            m[mp][c] = m_new
        return tuple(tuple(mm) for mm in m)

    q_cur = step % 2
    q_nxt = 1 - q_cur

    @pl.when(step == 0)
    def _():
        stage_q_t(q_ref[0:tq, :], 0)
        k0 = keys(0)
        for c in range(n_ct):
            qk(k0, 0, 0, c)

    acc_ref[...] = jnp.zeros(acc_ref.shape, F32)
    neg_inf = () if bounded else tuple(jnp.full((1, tc), -jnp.inf, F32) for _ in range(n_ct))

    def pair(i, m):
        row0 = i * 2 * tk
        k_next, v_cur = keys(row0 + tk), values(row0)
        for c in range(n_ct):
            qk(k_next, 1, q_cur, c)
            m = softmax_pv(v_cur, 0, m, c)
        k_next, v_cur = keys(row0 + 2 * tk), values(row0 + tk)
        for c in range(n_ct):
            qk(k_next, 0, q_cur, c)
            m = softmax_pv(v_cur, 1, m, c)
        return m

    m = lax.fori_loop(0, qi * (n_kb // 2), pair, (neg_inf, neg_inf))

    row0 = qi * tq
    r = lax.broadcasted_iota(jnp.int32, (tk, tc), 0)
    col = lax.broadcasted_iota(jnp.int32, (tk, tc), 1)
    n_diag = tk // tc
    masks = [r <= col + j * tc for j in range(n_diag)]
    q_same_head = q_ref[pl.ds(pl.multiple_of(jnp.minimum(qi + 1, n_q - 1) * tq, tq), tq), :]
    stage_q_t(jnp.where(head_done, qn_ref[...], q_same_head), q_nxt)
    k_first = jnp.where(head_done, kn_ref[...], keys(0))
    others = iter(interleave)

    lam = (jnp.exp(jnp.sum(lq1_ref[...] * lk1_ref[...], axis=-1, keepdims=True))
           - jnp.exp(jnp.sum(lq2_ref[...] * lk2_ref[...], axis=-1, keepdims=True)) + lambda_init)

    def finish(c):
        o_t = (acc_ref[0, :ATT_V_DIM, cols[c]] / acc_ref[0, ATT_V_DIM:ATT_V_DIM + 1, cols[c]]
               - lam * (acc_ref[1, :ATT_V_DIM, cols[c]] / acc_ref[1, ATT_V_DIM:ATT_V_DIM + 1, cols[c]]))
        o_ref[cols[c], :] = (_rms_norm(o_t.T, sw_ref[...]) * (1.0 - lambda_init)).astype(BF16)

    for d in range(n_kb):
        slot = d % 2
        last = d == n_kb - 1
        v_cur = values(row0 + d * tk)
        k_next = k_first if last else keys(row0 + (d + 1) * tk)
        for c in range(n_ct):
            if last:
                qk(k_next, 0, q_nxt, c)
            elif c >= (d + 1) * n_diag:
                qk(k_next, 1 - slot, q_cur, c)
            if c >= d * n_diag:
                m = softmax_pv(v_cur, slot, m, c, mask=masks[c - d * n_diag] if c < (d + 1) * n_diag else None)
            if c // n_diag == d:
                finish(c)
            next(others, None)
    for _ in others:
        pass


def _mix_kernel(z_ref, xbc_ref, halo_ref, dt_ref, cw_ref, cb_ref, dtb_ref, alog_ref, dskip_ref, ssd_nw_ref,
                q_ref, k_ref, v_ref, qn_ref, kn_ref, bound_ref, lq1_ref, lk1_ref, lq2_ref, lk2_ref, sw_ref,
                y_ref, o_ref, state_ref, s_ref, acc_ref, qt_ref, *, lambda_init, bounded):
    first = jnp.logical_and(pl.program_id(1) == 0, pl.program_id(2) == 0)

    @pl.when(first)
    def _():
        state_ref[...] = jnp.zeros(state_ref.shape, F32)

    ssd = _ssd_step(first, z_ref, xbc_ref, halo_ref, dt_ref, cw_ref, cb_ref, dtb_ref, alog_ref, dskip_ref, ssd_nw_ref,
                    y_ref, state_ref)
    _attn_kernel(q_ref, k_ref, v_ref, qn_ref, kn_ref, bound_ref, lq1_ref, lk1_ref, lq2_ref, lk2_ref, sw_ref, o_ref,
                 s_ref, acc_ref, qt_ref, lambda_init=lambda_init, bounded=bounded, interleave=ssd)


def _mix(z, xbc, dt, conv_w, conv_b, dt_bias, a_log, d_skip_exp, ssd_norm_w,
         q, k, v, bound, lq1, lk1, lq2, lk2, subln_w, lambda_init, bounded):
    bsz, s, width = q.shape
    n_heads = width // ATT_V_DIM
    tq = min(ATT_BLOCK_Q, s)
    n_q = s // tq
    rows = s // (n_heads * n_q)
    assert rows % SSD_CHUNK == 0 and rows % CONV_HALO == 0
    ssd_width, cw = z.shape[-1], xbc.shape[-1]
    slab = lambda n: pl.BlockSpec((None, rows, n), lambda b, h, i: (b, h * n_q + i, 0))
    halo = pl.BlockSpec((None, CONV_HALO, cw),
                        lambda b, h, i: (b, jnp.maximum((h * n_q + i) * (rows // CONV_HALO) - 1, 0), 0))
    o_spec = pl.BlockSpec((None, tq, LANES), lambda b, h, i: (b, i, h))
    head_spec = pl.BlockSpec((None, s, LANES), lambda b, h, i: (b, 0, h))
    tk = min(ATT_BLOCK_K, tq // 2)

    def next_head(b, h, i):
        nxt = jnp.minimum(b * n_heads + h + 1, bsz * n_heads - 1)
        return nxt // n_heads, 0, nxt % n_heads

    q_next_spec = pl.BlockSpec((None, tq, LANES), next_head)
    k_next_spec = pl.BlockSpec((None, tk, LANES), next_head)
    small = _resident((1, ATT_QK_DIM))
    return pl.pallas_call(
        functools.partial(_mix_kernel, lambda_init=lambda_init, bounded=bounded),
        grid=(bsz, n_heads, n_q),
        in_specs=[slab(ssd_width), slab(cw), halo, slab(LANES), _resident(conv_w.shape), _resident((1, cw)),
                  _resident((1, LANES)), _resident((1, LANES)), _resident((1, ssd_width)), _resident((1, ssd_width)),
                  head_spec, head_spec, head_spec, q_next_spec, k_next_spec, _resident((1, 1)), small, small, small,
                  small,
                  _resident((1, ATT_V_DIM))],
        out_specs=[slab(ssd_width), o_spec],
        out_shape=[jax.ShapeDtypeStruct((bsz, s, ssd_width), BF16), jax.ShapeDtypeStruct((bsz, s, width), BF16)],
        scratch_shapes=[pltpu.VMEM((SSD_STATE, ssd_width), F32),
                        pltpu.VMEM((2, 2, tk, tq), F32),
                        pltpu.VMEM((2, ATT_V_DIM + ATT_SUM_ROWS, tq), F32),
                        pltpu.VMEM((2, 2, LANES, tq), BF16)],
        compiler_params=pltpu.CompilerParams(dimension_semantics=("arbitrary",) * 3, vmem_limit_bytes=VMEM_LIMIT),
        name="mix",
    )(z, xbc, xbc, dt, conv_w, conv_b, dt_bias, a_log, d_skip_exp, ssd_norm_w,
      q, k, v, q, k, bound, lq1, lk1, lq2, lk2, subln_w)


def _out_kernel(x_ref, ys_ref, ya_ref, wos_ref, woa_ref, nw_ref, wg_ref, wu_ref, wd_ref, o_ref):
    x = (x_ref[...] + jnp.dot(ys_ref[...], wos_ref[...], preferred_element_type=F32)
         + jnp.dot(ya_ref[...], woa_ref[...], preferred_element_type=F32))
    h = _rms_norm(x, nw_ref[...]).astype(BF16)
    o_ref[...] = x + 0.5 * _swiglu(h, wg_ref, wu_ref, wd_ref)


def _out(x2d, y_ssd, y_att, wo_ssd, wo_att, norm_w, wg, wu, wd):
    t, d = x2d.shape
    tm = min(FFN_ROWS, t)
    row = lambda n: pl.BlockSpec((tm, n), lambda i: (i, 0))
    return pl.pallas_call(
        _out_kernel,
        grid=(t // tm,),
        in_specs=[row(d), row(y_ssd.shape[1]), row(y_att.shape[1]), _resident(wo_ssd.shape), _resident(wo_att.shape),
                  _resident((1, d)), _resident(wg.shape), _resident(wu.shape), _resident(wd.shape)],
        out_specs=row(d),
        out_shape=jax.ShapeDtypeStruct((t, d), F32),
        compiler_params=pltpu.CompilerParams(dimension_semantics=("arbitrary",), vmem_limit_bytes=VMEM_LIMIT),
        name="out_ffn",
    )(x2d, y_ssd, y_att, wo_ssd, wo_att, norm_w, wg, wu, wd)


def _pad_lanes(v):
    return jnp.zeros((1, LANES), F32).at[0, :v.shape[0]].set(v.astype(F32))


def kernel(x, ffn1_norm_w, ffn1_w_gate, ffn1_w_up, ffn1_w_down, mix_norm_w, w_in, conv_w, conv_b, dt_bias, a_log,
           d_skip, ssd_norm_w, q_norm_w, k_norm_w, lambda_q1, lambda_k1, lambda_q2, lambda_k2, attn_subln_w, w_out,
           ffn2_norm_w, ffn2_w_gate, ffn2_w_up, ffn2_w_down):
    bsz, s, d = x.shape
    depth = w_in.shape[0]
    ssd_heads = dt_bias.shape[1]
    ssd_width = ssd_heads * SSD_HEAD_DIM
    conv_ch = conv_w.shape[2]
    att_width = w_out.shape[1] - ssd_width
    qk_width = (att_width // ATT_V_DIM) * 2 * ATT_QK_DIM
    sizes = (ssd_width, conv_ch, ssd_heads, qk_width, qk_width, att_width)
    offs = [0]
    for n in sizes:
        offs.append(offs[-1] + n)
    row = lambda v: v.astype(F32).reshape(1, -1)

    x2d = x.reshape(bsz * s, d)
    for l in range(depth):
        lambda_init = 0.8 - 0.6 * math.exp(-0.3 * l)
        wo_ssd, wo_att = w_out[l][:ssd_width].astype(BF16), w_out[l][ssd_width:].astype(BF16)

        x1 = _ffn(x2d, row(ffn1_norm_w[l]), ffn1_w_gate[l], ffn1_w_up[l], ffn1_w_down[l])
        z, xbc, dt, q, k, v = _proj(x1, row(mix_norm_w[l]), jnp.swapaxes(w_in, 1, 2), l, tuple(offs),
                                    row(jnp.tile(q_norm_w[l], 2)), row(jnp.tile(k_norm_w[l], 2)))
        score_bound = (ATT_QK_DIM * Q_SCALE * ATT_BOUND_MARGIN * jnp.max(jnp.abs(q_norm_w[l]))
                       * jnp.max(jnp.abs(k_norm_w[l]))).astype(F32)
        mix_args = (z.reshape(bsz, s, -1), xbc.reshape(bsz, s, -1), dt.reshape(bsz, s, -1),
                    conv_w[l].astype(F32), row(conv_b[l]), _pad_lanes(dt_bias[l]), _pad_lanes(a_log[l]),
                    row(jnp.repeat(d_skip[l], SSD_HEAD_DIM)), row(ssd_norm_w[l]),
                    q.reshape(bsz, s, -1), k.reshape(bsz, s, -1), v.reshape(bsz, s, -1), score_bound.reshape(1, 1),
                    row(lambda_q1[l]), row(lambda_k1[l]), row(lambda_q2[l]), row(lambda_k2[l]),
                    row(attn_subln_w[l]))
        y_ssd, y_att = lax.cond(score_bound <= ATT_MAX_BOUND,
                                lambda *a: _mix(*a, lambda_init, True), lambda *a: _mix(*a, lambda_init, False),
                                *mix_args)
        x2d = _out(x1, y_ssd.reshape(bsz * s, -1), y_att.reshape(bsz * s, -1), wo_ssd, wo_att,
                   row(ffn2_norm_w[l]), ffn2_w_gate[l], ffn2_w_up[l], ffn2_w_down[l])
    return x2d.reshape(bsz, s, d)
```

```python
import functools
import math

import jax
import jax.numpy as jnp
from jax import lax
from jax.experimental import pallas as pl
from jax.experimental.pallas import tpu as pltpu

F32 = jnp.float32
BF16 = jnp.bfloat16

NORM_EPS = 1e-6
SSD_HEAD_DIM = 64
SSD_GROUPS = 2
SSD_STATE = 128
SSD_CONV = 4
SSD_CHUNK = 128
ATT_QK_DIM = 64
ATT_V_DIM = 128
LANES = 128
CONV_HALO = 16
LOG2_E = math.log2(math.e)
Q_SCALE = ATT_QK_DIM ** -0.5 * LOG2_E
ATT_BOUND_MARGIN = 1.01
ATT_MAX_BOUND = 60.0

FFN_ROWS = 512
FFN_COLS = 256
ATT_BLOCK_Q = 2048
ATT_BLOCK_K = 512
ATT_COL_TILE = 256
ATT_SUM_ROWS = 16
VMEM_LIMIT = 56 * 1024 * 1024


def _resident(shape):
    nd = len(shape)
    return pl.BlockSpec(shape, lambda *_: (0,) * nd, pipeline_mode=pl.Buffered(1))


def _rms_norm(x, w):
    return x * lax.rsqrt(jnp.mean(x * x, axis=-1, keepdims=True) + NORM_EPS) * w


def _silu(x):
    h = 0.5 * x
    return h + h * jnp.tanh(h)


def _swiglu(h, wg_ref, wu_ref, wd_ref):
    d_ff = wg_ref.shape[1]
    acc = None
    for c in range(d_ff // FFN_COLS):
        sl = slice(c * FFN_COLS, (c + 1) * FFN_COLS)
        g = jnp.dot(h, wg_ref[:, sl].astype(BF16), preferred_element_type=F32)
        u = jnp.dot(h, wu_ref[:, sl].astype(BF16), preferred_element_type=F32)
        a = (_silu(g) * u).astype(BF16)
        d = jnp.dot(a, wd_ref[sl, :].astype(BF16), preferred_element_type=F32)
        acc = d if acc is None else acc + d
    return acc


def _ffn_kernel(x_ref, nw_ref, wg_ref, wu_ref, wd_ref, o_ref):
    x = x_ref[...]
    h = _rms_norm(x, nw_ref[...]).astype(BF16)
    o_ref[...] = x + 0.5 * _swiglu(h, wg_ref, wu_ref, wd_ref)


def _ffn(x2d, norm_w, wg, wu, wd):
    t, d = x2d.shape
    tm = min(FFN_ROWS, t)
    row = pl.BlockSpec((tm, d), lambda i: (i, 0))
    return pl.pallas_call(
        _ffn_kernel,
        grid=(t // tm,),
        in_specs=[row, _resident((1, d)), _resident(wg.shape), _resident(wu.shape), _resident(wd.shape)],
        out_specs=row,
        out_shape=jax.ShapeDtypeStruct((t, d), F32),
        compiler_params=pltpu.CompilerParams(dimension_semantics=("arbitrary",), vmem_limit_bytes=VMEM_LIMIT),
        name="ffn",
    )(x2d, norm_w, wg, wu, wd)


def _qk_norm(t, w2, scale):
    lo = lax.broadcasted_iota(jnp.int32, t.shape, 1) < ATT_QK_DIM
    x2 = t * t
    s_lo = jnp.sum(jnp.where(lo, x2, 0.0), axis=-1, keepdims=True)
    s_hi = jnp.sum(jnp.where(lo, 0.0, x2), axis=-1, keepdims=True)
    r = lax.rsqrt(jnp.where(lo, s_lo, s_hi) * (1.0 / ATT_QK_DIM) + NORM_EPS)
    return t * r * (w2 * scale)


def _proj_kernel(x_ref, nw_ref, wt_ref, qw_ref, kw_ref, z_ref, xbc_ref, dt_ref, q_ref, k_ref, v_ref, *, offsets):
    o_z, o_xbc, o_dt, o_q, o_k, o_v, o_end = offsets
    h = _rms_norm(x_ref[...], nw_ref[...]).astype(BF16)

    def project(w_rows):
        return lax.dot_general(h, w_rows.astype(BF16), (((1,), (1,)), ((), ())), preferred_element_type=F32)

    n_heads = (o_k - o_q) // LANES
    for lo_row, g_ref, o_ref, scale in ((o_q, qw_ref, q_ref, Q_SCALE), (o_k, kw_ref, k_ref, 1.0)):
        t = project(wt_ref[lo_row:lo_row + o_k - o_q, :])
        for hd in range(n_heads):
            sl = slice(hd * LANES, (hd + 1) * LANES)
            o_ref[:, sl] = _qk_norm(t[:, sl], g_ref[...], scale).astype(BF16)
    w_dt = wt_ref[o_dt:o_dt + LANES, :]
    w_dt = jnp.where(lax.broadcasted_iota(jnp.int32, w_dt.shape, 0) < o_q - o_dt, w_dt, 0.0)
    dt_ref[...] = project(w_dt)
    v_ref[...] = project(wt_ref[o_v:o_end, :]).astype(BF16)
    z_ref[...] = project(wt_ref[o_z:o_xbc, :]).astype(BF16)
    xbc_ref[...] = project(wt_ref[o_xbc:o_dt, :]).astype(BF16)


def _proj(x2d, norm_w, wt_stack, layer, offsets, qw2, kw2):
    t, d = x2d.shape
    w_spec = pl.BlockSpec((None,) + wt_stack.shape[1:], lambda i: (layer, 0, 0), pipeline_mode=pl.Buffered(1))
    tm = min(FFN_ROWS, t)
    row = lambda n: pl.BlockSpec((tm, n), lambda i: (i, 0))
    o_z, o_xbc, o_dt, o_q, o_k, o_v, o_end = offsets
    assert all(o % 16 == 0 for o in offsets) and o_q - o_dt <= LANES and o_dt + LANES <= o_end
    widths = (o_xbc - o_z, o_dt - o_xbc, LANES, o_k - o_q, o_v - o_k, o_end - o_v)
    dtypes = (BF16, BF16, F32, BF16, BF16, BF16)
    return pl.pallas_call(
        functools.partial(_proj_kernel, offsets=offsets),
        grid=(t // tm,),
        in_specs=[row(d), _resident((1, d)), w_spec, _resident((1, LANES)), _resident((1, LANES))],
        out_specs=[row(n) for n in widths],
        out_shape=[jax.ShapeDtypeStruct((t, n), dt) for n, dt in zip(widths, dtypes)],
        compiler_params=pltpu.CompilerParams(dimension_semantics=("arbitrary",), vmem_limit_bytes=VMEM_LIMIT),
        name="proj",
    )(x2d, norm_w, wt_stack, qw2, kw2)


def _lane_bcast(x, col):
    return jnp.broadcast_to(x[:, col:col + 1], x.shape)


def _expand_heads(x, pair):
    lo = lax.broadcasted_iota(jnp.int32, x.shape, 1) < SSD_HEAD_DIM
    return jnp.where(lo, _lane_bcast(x, 2 * pair), _lane_bcast(x, 2 * pair + 1))


def _ssd_step(first, z_ref, xbc_ref, halo_ref, dt_ref, cw_ref, cb_ref, dtb_ref, alog_ref, dskip_ref, nw_ref,
              y_ref, state_ref):
    L = SSD_CHUNK
    n_chunks = xbc_ref.shape[0] // L
    width = z_ref.shape[-1]
    n_pairs = width // LANES
    heads_per_group = (width // SSD_HEAD_DIM) // SSD_GROUPS
    gw = width // SSD_GROUPS

    out_t = lax.broadcasted_iota(jnp.int32, (L, CONV_HALO + L), 0)
    in_t = lax.broadcasted_iota(jnp.int32, (L, CONV_HALO + L), 1) - CONV_HALO
    shifts = [jnp.where(in_t == out_t - (SSD_CONV - 1 - kk), 1.0, 0.0).astype(BF16) for kk in range(SSD_CONV - 1)]
    causal = lax.broadcasted_iota(jnp.int32, (L, L), 0) >= lax.broadcasted_iota(jnp.int32, (L, L), 1)
    tril = causal.astype(F32)
    lo = lax.broadcasted_iota(jnp.int32, (L, LANES), 1) < SSD_HEAD_DIM
    zero = jnp.zeros((L, LANES), BF16)
    head_of_lane = lax.broadcasted_iota(jnp.int32, (LANES, width), 1) // SSD_HEAD_DIM
    expand = (lax.broadcasted_iota(jnp.int32, (LANES, width), 0) == head_of_lane).astype(BF16)
    neg_a = -jnp.exp(alog_ref[...])

    halo = halo_ref[...]
    tail = jnp.where(first, jnp.zeros_like(halo), halo)
    for ci in range(n_chunks):
        rows = slice(ci * L, (ci + 1) * L)

        x_cur = xbc_ref[rows, :]
        x_ext = jnp.concatenate([tail, x_cur], axis=0)
        tail = x_cur[L - CONV_HALO:, :]
        conv = cb_ref[...] + cw_ref[SSD_CONV - 1:SSD_CONV, :] * x_cur.astype(F32)
        for kk in range(SSD_CONV - 1):
            conv = conv + cw_ref[kk:kk + 1, :] * jnp.dot(shifts[kk], x_ext, preferred_element_type=F32)
        act = _silu(conv)
        yield
        xs = act[:, :width]
        bm = act[:, width:width + SSD_GROUPS * SSD_STATE].astype(BF16)
        cm = act[:, width + SSD_GROUPS * SSD_STATE:]

        raw = dt_ref[rows, :] + dtb_ref[...]
        dt = jnp.maximum(raw, 0.0) + jnp.log1p(jnp.exp(-jnp.abs(raw)))
        a_cs = LOG2_E * jnp.dot(tril, dt * neg_a, preferred_element_type=F32, precision=lax.Precision.HIGHEST)
        a_last = a_cs[L - 1:L, :]
        f = dt * jnp.exp2(a_last - a_cs)
        a_src_t = a_cs.T - jnp.log2(dt).T

        yield
        cm_bf = cm.astype(BF16)
        cb = [lax.dot_general(cm_bf[:, g * SSD_STATE:(g + 1) * SSD_STATE], bm[:, g * SSD_STATE:(g + 1) * SSD_STATE],
                              (((1,), (1,)), ((), ())), preferred_element_type=F32) for g in range(SSD_GROUPS)]
        xs_bf = xs.astype(BF16)
        y_tiles = []
        for pair in range(n_pairs):
            g = (2 * pair) // heads_per_group
            cm_g = cm[:, g * SSD_STATE:(g + 1) * SSD_STATE]
            sl = slice(pair * LANES, (pair + 1) * LANES)
            x_pair = xs_bf[:, sl]
            prev_pair = state_ref[:, sl].astype(BF16)
            y_pair = None
            for half in range(2):
                hd = 2 * pair + half
                a_col = _lane_bcast(a_cs, hd)
                seg = a_col - a_src_t[hd:hd + 1, :]
                m_h = (cb[g] * jnp.exp2(jnp.where(causal, seg, -jnp.inf))).astype(BF16)
                c_h = (cm_g * jnp.exp2(a_col)).astype(BF16)
                keep = lo if half == 0 else jnp.logical_not(lo)
                rhs = jnp.concatenate([jnp.where(keep, x_pair, zero), jnp.where(keep, prev_pair, zero)], axis=0)
                yh = jnp.dot(jnp.concatenate([m_h, c_h], axis=1), rhs, preferred_element_type=F32)
                y_pair = yh if y_pair is None else y_pair + yh
            y_tiles.append(y_pair)
            yield

        f_hi = f.astype(BF16)
        f_lo = (f - f_hi.astype(F32)).astype(BF16)
        f_exp = (jnp.dot(f_hi, expand, preferred_element_type=F32)
                 + jnp.dot(f_lo, expand, preferred_element_type=F32))
        ea_last = jnp.exp2(a_cs[L - 8:L, :])
        for pair in range(n_pairs):
            g = (2 * pair) // heads_per_group
            sl = slice(pair * LANES, (pair + 1) * LANES)
            bm_g = bm[:, g * SSD_STATE:(g + 1) * SSD_STATE]
            xw = (xs[:, sl] * f_exp[:, sl]).astype(BF16)
            new = lax.dot_general(bm_g, xw, (((0,), (0,)), ((), ())), preferred_element_type=F32)
            chunk_decay = _expand_heads(ea_last, pair)[7:8, :]
            state_ref[:, sl] = state_ref[:, sl] * chunk_decay + new
            if pair % 4 == 3:
                yield

        y = jnp.concatenate(y_tiles, axis=1) + xs * dskip_ref[...]
        gated = y * _silu(z_ref[rows, :].astype(F32))
        for g in range(SSD_GROUPS):
            sl = slice(g * gw, (g + 1) * gw)
            y_ref[rows, sl] = _rms_norm(gated[:, sl], nw_ref[:, sl]).astype(BF16)
        yield


def _attn_kernel(q_ref, k_ref, v_ref, qn_ref, kn_ref, lq1_ref, lk1_ref, lq2_ref, lk2_ref, sw_ref, o_ref,
                 s_ref, acc_ref, qt_ref, *, lambda_init, bounded, interleave=()):
    tq = o_ref.shape[0]
    tk = s_ref.shape[2]
    n_kb = tq // tk
    tc = ATT_COL_TILE
    n_ct = tq // tc
    n_q = q_ref.shape[0] // tq
    qi = pl.program_id(2)
    step = (pl.program_id(0) * pl.num_programs(1) + pl.program_id(1)) * n_q + qi
    head_done = qi == n_q - 1
    cols = [slice(c * tc, (c + 1) * tc) for c in range(n_ct)]

    def stage_q_t(q, qslot):
        lo = lax.broadcasted_iota(jnp.int32, q.shape, 1) < ATT_QK_DIM
        zero = jnp.zeros_like(q)
        qt_ref[qslot, 0] = jnp.where(lo, q, zero).T
        qt_ref[qslot, 1] = jnp.where(lo, zero, q).T

    def keys(row0):
        return k_ref[pl.ds(pl.multiple_of(row0, tk), tk), :]

    ones_rows = jnp.ones((ATT_SUM_ROWS, tk), BF16)

    def values(row0):
        return jnp.concatenate([v_ref[pl.ds(pl.multiple_of(row0, tk), tk), :].T, ones_rows], axis=0)

    def qk(k, slot, qslot, c):
        for mp in range(2):
            s_ref[slot, mp, :, cols[c]] = jnp.dot(k, qt_ref[qslot, mp, :, cols[c]], preferred_element_type=F32)

    def softmax_pv(v_ext, slot, m, c, mask=None):
        m = [list(mm) for mm in m]
        for mp in range(2):
            s = s_ref[slot, mp, :, cols[c]]
            if mask is not None:
                s = jnp.where(mask, s, -jnp.inf)
            if bounded:
                p = jnp.exp2(s).astype(BF16)
                acc_ref[mp, :, cols[c]] = acc_ref[mp, :, cols[c]] + jnp.dot(v_ext, p, preferred_element_type=F32)
                continue
            m_old = m[mp][c]
            m_new = jnp.maximum(m_old, jnp.max(s, axis=0, keepdims=True))
            p = jnp.exp2(s - m_new).astype(BF16)
            acc_ref[mp, :, cols[c]] = (jnp.exp2(m_old - m_new) * acc_ref[mp, :, cols[c]]
                                       + jnp.dot(v_ext, p, preferred_element_type=F32))
            m[mp][c] = m_new
        return tuple(tuple(mm) for mm in m)

    q_cur = step % 2
    q_nxt = 1 - q_cur

    @pl.when(step == 0)
    def _():
        stage_q_t(q_ref[0:tq, :], 0)
        k0 = keys(0)
        for c in range(n_ct):
            qk(k0, 0, 0, c)

    acc_ref[...] = jnp.zeros(acc_ref.shape, F32)
    neg_inf = () if bounded else tuple(jnp.full((1, tc), -jnp.inf, F32) for _ in range(n_ct))

    def pair(i, m):
        row0 = i * 2 * tk
        k_next, v_cur = keys(row0 + tk), values(row0)
        for c in range(n_ct):
            qk(k_next, 1, q_cur, c)
            m = softmax_pv(v_cur, 0, m, c)
        k_next, v_cur = keys(row0 + 2 * tk), values(row0 + tk)
        for c in range(n_ct):
            qk(k_next, 0, q_cur, c)
            m = softmax_pv(v_cur, 1, m, c)
        return m

    m = lax.fori_loop(0, qi * (n_kb // 2), pair, (neg_inf, neg_inf))

    row0 = qi * tq
    r = lax.broadcasted_iota(jnp.int32, (tk, tc), 0)
    col = lax.broadcasted_iota(jnp.int32, (tk, tc), 1)
    n_diag = tk // tc
    masks = [r <= col + j * tc for j in range(n_diag)]
    q_same_head = q_ref[pl.ds(pl.multiple_of(jnp.minimum(qi + 1, n_q - 1) * tq, tq), tq), :]
    stage_q_t(jnp.where(head_done, qn_ref[...], q_same_head), q_nxt)
    k_first = jnp.where(head_done, kn_ref[...], keys(0))
    others = iter(interleave)

    lam = (jnp.exp(jnp.sum(lq1_ref[...] * lk1_ref[...], axis=-1, keepdims=True))
           - jnp.exp(jnp.sum(lq2_ref[...] * lk2_ref[...], axis=-1, keepdims=True)) + lambda_init)

    def finish(c):
        o_t = (acc_ref[0, :ATT_V_DIM, cols[c]] / acc_ref[0, ATT_V_DIM:ATT_V_DIM + 1, cols[c]]
               - lam * (acc_ref[1, :ATT_V_DIM, cols[c]] / acc_ref[1, ATT_V_DIM:ATT_V_DIM + 1, cols[c]]))
        o_ref[cols[c], :] = (_rms_norm(o_t.T, sw_ref[...]) * (1.0 - lambda_init)).astype(BF16)

    for d in range(n_kb):
        slot = d % 2
        last = d == n_kb - 1
        v_cur = values(row0 + d * tk)
        k_next = k_first if last else keys(row0 + (d + 1) * tk)
        for c in range(n_ct):
            if last:
                qk(k_next, 0, q_nxt, c)
            elif c >= (d + 1) * n_diag:
                qk(k_next, 1 - slot, q_cur, c)
            if c >= d * n_diag:
                m = softmax_pv(v_cur, slot, m, c, mask=masks[c - d * n_diag] if c < (d + 1) * n_diag else None)
            if c // n_diag == d:
                finish(c)
            for _ in range(0 if d == 0 else 2 if last else 1):
                next(others, None)
    for _ in others:
        pass


def _mix_kernel(z_ref, xbc_ref, halo_ref, dt_ref, cw_ref, cb_ref, dtb_ref, alog_ref, dskip_ref, ssd_nw_ref,
                q_ref, k_ref, v_ref, qn_ref, kn_ref, lq1_ref, lk1_ref, lq2_ref, lk2_ref, sw_ref,
                y_ref, o_ref, state_ref, s_ref, acc_ref, qt_ref, *, lambda_init, bounded):
    first = jnp.logical_and(pl.program_id(1) == 0, pl.program_id(2) == 0)

    @pl.when(first)
    def _():
        state_ref[...] = jnp.zeros(state_ref.shape, F32)

    ssd = _ssd_step(first, z_ref, xbc_ref, halo_ref, dt_ref, cw_ref, cb_ref, dtb_ref, alog_ref, dskip_ref, ssd_nw_ref,
                    y_ref, state_ref)
    _attn_kernel(q_ref, k_ref, v_ref, qn_ref, kn_ref, lq1_ref, lk1_ref, lq2_ref, lk2_ref, sw_ref, o_ref,
                 s_ref, acc_ref, qt_ref, lambda_init=lambda_init, bounded=bounded, interleave=ssd)


def _mix(z, xbc, dt, conv_w, conv_b, dt_bias, a_log, d_skip_exp, ssd_norm_w,
         q, k, v, lq1, lk1, lq2, lk2, subln_w, lambda_init, bounded):
    bsz, s, width = q.shape
    n_heads = width // ATT_V_DIM
    tq = min(ATT_BLOCK_Q, s)
    n_q = s // tq
    rows = s // (n_heads * n_q)
    assert rows % SSD_CHUNK == 0 and rows % CONV_HALO == 0
    ssd_width, cw = z.shape[-1], xbc.shape[-1]
    slab = lambda n: pl.BlockSpec((None, rows, n), lambda b, h, i: (b, h * n_q + i, 0))
    halo = pl.BlockSpec((None, CONV_HALO, cw),
                        lambda b, h, i: (b, jnp.maximum((h * n_q + i) * (rows // CONV_HALO) - 1, 0), 0))
    o_spec = pl.BlockSpec((None, tq, LANES), lambda b, h, i: (b, i, h))
    head_spec = pl.BlockSpec((None, s, LANES), lambda b, h, i: (b, 0, h))
    tk = min(ATT_BLOCK_K, tq // 2)

    def next_head(b, h, i):
        nxt = jnp.minimum(b * n_heads + h + 1, bsz * n_heads - 1)
        return nxt // n_heads, 0, nxt % n_heads

    q_next_spec = pl.BlockSpec((None, tq, LANES), next_head)
    k_next_spec = pl.BlockSpec((None, tk, LANES), next_head)
    small = _resident((1, ATT_QK_DIM))
    return pl.pallas_call(
        functools.partial(_mix_kernel, lambda_init=lambda_init, bounded=bounded),
        grid=(bsz, n_heads, n_q),
        in_specs=[slab(ssd_width), slab(cw), halo, slab(LANES), _resident(conv_w.shape), _resident((1, cw)),
                  _resident((1, LANES)), _resident((1, LANES)), _resident((1, ssd_width)), _resident((1, ssd_width)),
                  head_spec, head_spec, head_spec, q_next_spec, k_next_spec, small, small, small, small,
                  _resident((1, ATT_V_DIM))],
        out_specs=[slab(ssd_width), o_spec],
        out_shape=[jax.ShapeDtypeStruct((bsz, s, ssd_width), BF16), jax.ShapeDtypeStruct((bsz, s, width), BF16)],
        scratch_shapes=[pltpu.VMEM((SSD_STATE, ssd_width), F32),
                        pltpu.VMEM((2, 2, tk, tq), F32),
                        pltpu.VMEM((2, ATT_V_DIM + ATT_SUM_ROWS, tq), F32),
                        pltpu.VMEM((2, 2, LANES, tq), BF16)],
        compiler_params=pltpu.CompilerParams(dimension_semantics=("arbitrary",) * 3, vmem_limit_bytes=VMEM_LIMIT),
        name="mix",
    )(z, xbc, xbc, dt, conv_w, conv_b, dt_bias, a_log, d_skip_exp, ssd_norm_w,
      q, k, v, q, k, lq1, lk1, lq2, lk2, subln_w)


def _out_kernel(x_ref, ys_ref, ya_ref, wos_ref, woa_ref, nw_ref, wg_ref, wu_ref, wd_ref, o_ref):
    x = (x_ref[...] + jnp.dot(ys_ref[...], wos_ref[...], preferred_element_type=F32)
         + jnp.dot(ya_ref[...], woa_ref[...], preferred_element_type=F32))
    h = _rms_norm(x, nw_ref[...]).astype(BF16)
    o_ref[...] = x + 0.5 * _swiglu(h, wg_ref, wu_ref, wd_ref)


def _out(x2d, y_ssd, y_att, wo_ssd, wo_att, norm_w, wg, wu, wd):
    t, d = x2d.shape
    tm = min(FFN_ROWS, t)
    row = lambda n: pl.BlockSpec((tm, n), lambda i: (i, 0))
    return pl.pallas_call(
        _out_kernel,
        grid=(t // tm,),
        in_specs=[row(d), row(y_ssd.shape[1]), row(y_att.shape[1]), _resident(wo_ssd.shape), _resident(wo_att.shape),
                  _resident((1, d)), _resident(wg.shape), _resident(wu.shape), _resident(wd.shape)],
        out_specs=row(d),
        out_shape=jax.ShapeDtypeStruct((t, d), F32),
        compiler_params=pltpu.CompilerParams(dimension_semantics=("arbitrary",), vmem_limit_bytes=VMEM_LIMIT),
        name="out_ffn",
    )(x2d, y_ssd, y_att, wo_ssd, wo_att, norm_w, wg, wu, wd)


def _pad_lanes(v):
    return jnp.zeros((1, LANES), F32).at[0, :v.shape[0]].set(v.astype(F32))


def kernel(x, ffn1_norm_w, ffn1_w_gate, ffn1_w_up, ffn1_w_down, mix_norm_w, w_in, conv_w, conv_b, dt_bias, a_log,
           d_skip, ssd_norm_w, q_norm_w, k_norm_w, lambda_q1, lambda_k1, lambda_q2, lambda_k2, attn_subln_w, w_out,
           ffn2_norm_w, ffn2_w_gate, ffn2_w_up, ffn2_w_down):
    bsz, s, d = x.shape
    depth = w_in.shape[0]
    ssd_heads = dt_bias.shape[1]
    ssd_width = ssd_heads * SSD_HEAD_DIM
    conv_ch = conv_w.shape[2]
    att_width = w_out.shape[1] - ssd_width
    qk_width = (att_width // ATT_V_DIM) * 2 * ATT_QK_DIM
    sizes = (ssd_width, conv_ch, ssd_heads, qk_width, qk_width, att_width)
    offs = [0]
    for n in sizes:
        offs.append(offs[-1] + n)
    row = lambda v: v.astype(F32).reshape(1, -1)

    x2d = x.reshape(bsz * s, d)
    for l in range(depth):
        lambda_init = 0.8 - 0.6 * math.exp(-0.3 * l)
        wo_ssd, wo_att = w_out[l][:ssd_width].astype(BF16), w_out[l][ssd_width:].astype(BF16)

        x1 = _ffn(x2d, row(ffn1_norm_w[l]), ffn1_w_gate[l], ffn1_w_up[l], ffn1_w_down[l])
        z, xbc, dt, q, k, v = _proj(x1, row(mix_norm_w[l]), jnp.swapaxes(w_in, 1, 2), l, tuple(offs),
                                    row(jnp.tile(q_norm_w[l], 2)), row(jnp.tile(k_norm_w[l], 2)))
        score_bound = (ATT_QK_DIM * Q_SCALE * ATT_BOUND_MARGIN * jnp.max(jnp.abs(q_norm_w[l]))
                       * jnp.max(jnp.abs(k_norm_w[l]))).astype(F32)
        mix_args = (z.reshape(bsz, s, -1), xbc.reshape(bsz, s, -1), dt.reshape(bsz, s, -1),
                    conv_w[l].astype(F32), row(conv_b[l]), _pad_lanes(dt_bias[l]), _pad_lanes(a_log[l]),
                    row(jnp.repeat(d_skip[l], SSD_HEAD_DIM)), row(ssd_norm_w[l]),
                    q.reshape(bsz, s, -1), k.reshape(bsz, s, -1), v.reshape(bsz, s, -1),
                    row(lambda_q1[l]), row(lambda_k1[l]), row(lambda_q2[l]), row(lambda_k2[l]),
                    row(attn_subln_w[l]))
        y_ssd, y_att = lax.cond(score_bound <= ATT_MAX_BOUND,
                                lambda *a: _mix(*a, lambda_init, True), lambda *a: _mix(*a, lambda_init, False),
                                *mix_args)
        x2d = _out(x1, y_ssd.reshape(bsz * s, -1), y_att.reshape(bsz * s, -1), wo_ssd, wo_att,
                   row(ffn2_norm_w[l]), ffn2_w_gate[l], ffn2_w_up[l], ffn2_w_down[l])
    return x2d.reshape(bsz, s, d)
```

```python
import functools
import math

import jax
import jax.numpy as jnp
from jax import lax
from jax.experimental import pallas as pl
from jax.experimental.pallas import tpu as pltpu

F32 = jnp.float32
BF16 = jnp.bfloat16

NORM_EPS = 1e-6
SSD_HEAD_DIM = 64
SSD_GROUPS = 2
SSD_STATE = 128
SSD_CONV = 4
SSD_CHUNK = 128
ATT_QK_DIM = 64
ATT_V_DIM = 128
LANES = 128
CONV_HALO = 16
LOG2_E = math.log2(math.e)
Q_SCALE = ATT_QK_DIM ** -0.5 * LOG2_E
ATT_BOUND_MARGIN = 1.01
ATT_MAX_BOUND = 60.0

FFN_ROWS = 512
FFN_COLS = 256
ATT_BLOCK_Q = 2048
ATT_BLOCK_K = 512
ATT_COL_TILE = 256
ATT_SUM_ROWS = 16
VMEM_LIMIT = 56 * 1024 * 1024


def _resident(shape):
    nd = len(shape)
    return pl.BlockSpec(shape, lambda *_: (0,) * nd, pipeline_mode=pl.Buffered(1))


def _rms_norm(x, w):
    return x * lax.rsqrt(jnp.mean(x * x, axis=-1, keepdims=True) + NORM_EPS) * w


def _silu(x):
    h = 0.5 * x
    return h + h * jnp.tanh(h)


def _swiglu(h, wg_ref, wu_ref, wd_ref):
    d_ff = wg_ref.shape[1]
    acc = None
    for c in range(d_ff // FFN_COLS):
        sl = slice(c * FFN_COLS, (c + 1) * FFN_COLS)
        g = jnp.dot(h, wg_ref[:, sl].astype(BF16), preferred_element_type=F32)
        u = jnp.dot(h, wu_ref[:, sl].astype(BF16), preferred_element_type=F32)
        a = (_silu(g) * u).astype(BF16)
        d = jnp.dot(a, wd_ref[sl, :].astype(BF16), preferred_element_type=F32)
        acc = d if acc is None else acc + d
    return acc


def _ffn_kernel(x_ref, nw_ref, wg_ref, wu_ref, wd_ref, o_ref):
    x = x_ref[...]
    h = _rms_norm(x, nw_ref[...]).astype(BF16)
    o_ref[...] = x + 0.5 * _swiglu(h, wg_ref, wu_ref, wd_ref)


def _ffn(x2d, norm_w, wg, wu, wd):
    t, d = x2d.shape
    tm = min(FFN_ROWS, t)
    row = pl.BlockSpec((tm, d), lambda i: (i, 0))
    return pl.pallas_call(
        _ffn_kernel,
        grid=(t // tm,),
        in_specs=[row, _resident((1, d)), _resident(wg.shape), _resident(wu.shape), _resident(wd.shape)],
        out_specs=row,
        out_shape=jax.ShapeDtypeStruct((t, d), F32),
        compiler_params=pltpu.CompilerParams(dimension_semantics=("arbitrary",), vmem_limit_bytes=VMEM_LIMIT),
        name="ffn",
    )(x2d, norm_w, wg, wu, wd)


def _qk_norm(t, w2, scale):
    lo = lax.broadcasted_iota(jnp.int32, t.shape, 1) < ATT_QK_DIM
    x2 = t * t
    s_lo = jnp.sum(jnp.where(lo, x2, 0.0), axis=-1, keepdims=True)
    s_hi = jnp.sum(jnp.where(lo, 0.0, x2), axis=-1, keepdims=True)
    r = lax.rsqrt(jnp.where(lo, s_lo, s_hi) * (1.0 / ATT_QK_DIM) + NORM_EPS)
    return t * r * (w2 * scale)


def _proj_kernel(x_ref, nw_ref, wt_ref, qw_ref, kw_ref, z_ref, xbc_ref, dt_ref, q_ref, k_ref, v_ref, *, offsets):
    o_z, o_xbc, o_dt, o_q, o_k, o_v, o_end = offsets
    h = _rms_norm(x_ref[...], nw_ref[...]).astype(BF16)

    def project(w_rows):
        return lax.dot_general(h, w_rows.astype(BF16), (((1,), (1,)), ((), ())), preferred_element_type=F32)

    n_heads = (o_k - o_q) // LANES
    for lo_row, g_ref, o_ref, scale in ((o_q, qw_ref, q_ref, Q_SCALE), (o_k, kw_ref, k_ref, 1.0)):
        t = project(wt_ref[lo_row:lo_row + o_k - o_q, :])
        for hd in range(n_heads):
            sl = slice(hd * LANES, (hd + 1) * LANES)
            o_ref[:, sl] = _qk_norm(t[:, sl], g_ref[...], scale).astype(BF16)
    w_dt = wt_ref[o_dt:o_dt + LANES, :]
    w_dt = jnp.where(lax.broadcasted_iota(jnp.int32, w_dt.shape, 0) < o_q - o_dt, w_dt, 0.0)
    dt_ref[...] = project(w_dt)
    v_ref[...] = lax.dot_general(wt_ref[o_v:o_end, :].astype(BF16), h, (((1,), (1,)), ((), ())),
                                 preferred_element_type=F32).astype(BF16)
    z_ref[...] = project(wt_ref[o_z:o_xbc, :]).astype(BF16)
    xbc_ref[...] = project(wt_ref[o_xbc:o_dt, :]).astype(BF16)


def _proj(x2d, norm_w, wt_stack, layer, offsets, qw2, kw2):
    t, d = x2d.shape
    w_spec = pl.BlockSpec((None,) + wt_stack.shape[1:], lambda i: (layer, 0, 0), pipeline_mode=pl.Buffered(1))
    tm = min(FFN_ROWS, t)
    row = lambda n: pl.BlockSpec((tm, n), lambda i: (i, 0))
    o_z, o_xbc, o_dt, o_q, o_k, o_v, o_end = offsets
    assert all(o % 16 == 0 for o in offsets) and o_q - o_dt <= LANES and o_dt + LANES <= o_end
    widths = (o_xbc - o_z, o_dt - o_xbc, LANES, o_k - o_q, o_v - o_k, o_end - o_v)
    dtypes = (BF16, BF16, F32, BF16, BF16, BF16)
    return pl.pallas_call(
        functools.partial(_proj_kernel, offsets=offsets),
        grid=(t // tm,),
        in_specs=[row(d), _resident((1, d)), w_spec, _resident((1, LANES)), _resident((1, LANES))],
        out_specs=[row(n) for n in widths[:-1]] + [pl.BlockSpec((widths[-1], tm), lambda i: (0, i))],
        out_shape=[jax.ShapeDtypeStruct((t, n), dt) for n, dt in zip(widths[:-1], dtypes)]
                  + [jax.ShapeDtypeStruct((widths[-1], t), BF16)],
        compiler_params=pltpu.CompilerParams(dimension_semantics=("arbitrary",), vmem_limit_bytes=VMEM_LIMIT),
        name="proj",
    )(x2d, norm_w, wt_stack, qw2, kw2)


def _lane_bcast(x, col):
    return jnp.broadcast_to(x[:, col:col + 1], x.shape)


def _expand_heads(x, pair):
    lo = lax.broadcasted_iota(jnp.int32, x.shape, 1) < SSD_HEAD_DIM
    return jnp.where(lo, _lane_bcast(x, 2 * pair), _lane_bcast(x, 2 * pair + 1))


def _ssd_step(first, z_ref, xbc_ref, halo_ref, dt_ref, cw_ref, cb_ref, dtb_ref, alog_ref, dskip_ref, nw_ref,
              y_ref, state_ref):
    L = SSD_CHUNK
    n_chunks = xbc_ref.shape[0] // L
    width = z_ref.shape[-1]
    n_pairs = width // LANES
    heads_per_group = (width // SSD_HEAD_DIM) // SSD_GROUPS
    gw = width // SSD_GROUPS

    out_t = lax.broadcasted_iota(jnp.int32, (L, CONV_HALO + L), 0)
    in_t = lax.broadcasted_iota(jnp.int32, (L, CONV_HALO + L), 1) - CONV_HALO
    shifts = [jnp.where(in_t == out_t - (SSD_CONV - 1 - kk), 1.0, 0.0).astype(BF16) for kk in range(SSD_CONV - 1)]
    causal = lax.broadcasted_iota(jnp.int32, (L, L), 0) >= lax.broadcasted_iota(jnp.int32, (L, L), 1)
    tril = causal.astype(F32)
    lo = lax.broadcasted_iota(jnp.int32, (L, LANES), 1) < SSD_HEAD_DIM
    zero = jnp.zeros((L, LANES), BF16)
    head_of_lane = lax.broadcasted_iota(jnp.int32, (LANES, width), 1) // SSD_HEAD_DIM
    expand = (lax.broadcasted_iota(jnp.int32, (LANES, width), 0) == head_of_lane).astype(BF16)
    neg_a = -jnp.exp(alog_ref[...])

    halo = halo_ref[...]
    tail = jnp.where(first, jnp.zeros_like(halo), halo)
    for ci in range(n_chunks):
        rows = slice(ci * L, (ci + 1) * L)

        x_cur = xbc_ref[rows, :]
        x_ext = jnp.concatenate([tail, x_cur], axis=0)
        tail = x_cur[L - CONV_HALO:, :]
        conv = cb_ref[...] + cw_ref[SSD_CONV - 1:SSD_CONV, :] * x_cur.astype(F32)
        for kk in range(SSD_CONV - 1):
            conv = conv + cw_ref[kk:kk + 1, :] * jnp.dot(shifts[kk], x_ext, preferred_element_type=F32)
        act = _silu(conv)
        yield
        xs = act[:, :width]
        bm = act[:, width:width + SSD_GROUPS * SSD_STATE].astype(BF16)
        cm = act[:, width + SSD_GROUPS * SSD_STATE:]

        raw = dt_ref[rows, :] + dtb_ref[...]
        dt = jnp.maximum(raw, 0.0) + jnp.log1p(jnp.exp(-jnp.abs(raw)))
        a_cs = LOG2_E * jnp.dot(tril, dt * neg_a, preferred_element_type=F32, precision=lax.Precision.HIGHEST)
        a_last = a_cs[L - 1:L, :]
        f = dt * jnp.exp2(a_last - a_cs)
        a_src_t = a_cs.T - jnp.log2(dt).T

        yield
        cm_bf = cm.astype(BF16)
        cb = [lax.dot_general(cm_bf[:, g * SSD_STATE:(g + 1) * SSD_STATE], bm[:, g * SSD_STATE:(g + 1) * SSD_STATE],
                              (((1,), (1,)), ((), ())), preferred_element_type=F32) for g in range(SSD_GROUPS)]
        xs_bf = xs.astype(BF16)
        y_tiles = []
        for pair in range(n_pairs):
            g = (2 * pair) // heads_per_group
            cm_g = cm[:, g * SSD_STATE:(g + 1) * SSD_STATE]
            sl = slice(pair * LANES, (pair + 1) * LANES)
            x_pair = xs_bf[:, sl]
            prev_pair = state_ref[:, sl].astype(BF16)
            y_pair = None
            for half in range(2):
                hd = 2 * pair + half
                a_col = _lane_bcast(a_cs, hd)
                seg = a_col - a_src_t[hd:hd + 1, :]
                m_h = (cb[g] * jnp.exp2(jnp.where(causal, seg, -jnp.inf))).astype(BF16)
                c_h = (cm_g * jnp.exp2(a_col)).astype(BF16)
                keep = lo if half == 0 else jnp.logical_not(lo)
                rhs = jnp.concatenate([jnp.where(keep, x_pair, zero), jnp.where(keep, prev_pair, zero)], axis=0)
                yh = jnp.dot(jnp.concatenate([m_h, c_h], axis=1), rhs, preferred_element_type=F32)
                y_pair = yh if y_pair is None else y_pair + yh
            y_tiles.append(y_pair)
            yield

        f_hi = f.astype(BF16)
        f_lo = (f - f_hi.astype(F32)).astype(BF16)
        f_exp = (jnp.dot(f_hi, expand, preferred_element_type=F32)
                 + jnp.dot(f_lo, expand, preferred_element_type=F32))
        ea_last = jnp.exp2(a_cs[L - 8:L, :])
        for pair in range(n_pairs):
            g = (2 * pair) // heads_per_group
            sl = slice(pair * LANES, (pair + 1) * LANES)
            bm_g = bm[:, g * SSD_STATE:(g + 1) * SSD_STATE]
            xw = (xs[:, sl] * f_exp[:, sl]).astype(BF16)
            new = lax.dot_general(bm_g, xw, (((0,), (0,)), ((), ())), preferred_element_type=F32)
            chunk_decay = _expand_heads(ea_last, pair)[7:8, :]
            state_ref[:, sl] = state_ref[:, sl] * chunk_decay + new
            if pair % 4 == 3:
                yield

        y = jnp.concatenate(y_tiles, axis=1) + xs * dskip_ref[...]
        gated = y * _silu(z_ref[rows, :].astype(F32))
        for g in range(SSD_GROUPS):
            sl = slice(g * gw, (g + 1) * gw)
            y_ref[rows, sl] = _rms_norm(gated[:, sl], nw_ref[:, sl]).astype(BF16)
        yield


def _attn_kernel(q_ref, k_ref, v_ref, qn_ref, kn_ref, lq1_ref, lk1_ref, lq2_ref, lk2_ref, sw_ref, o_ref,
                 s_ref, acc_ref, qt_ref, *, lambda_init, bounded, interleave=()):
    tq = o_ref.shape[0]
    tk = s_ref.shape[2]
    n_kb = tq // tk
    tc = ATT_COL_TILE
    n_ct = tq // tc
    n_q = q_ref.shape[0] // tq
    qi = pl.program_id(2)
    step = (pl.program_id(0) * pl.num_programs(1) + pl.program_id(1)) * n_q + qi
    head_done = qi == n_q - 1
    cols = [slice(c * tc, (c + 1) * tc) for c in range(n_ct)]

    def stage_q_t(q, qslot):
        lo = lax.broadcasted_iota(jnp.int32, q.shape, 1) < ATT_QK_DIM
        zero = jnp.zeros_like(q)
        qt_ref[qslot, 0] = jnp.where(lo, q, zero).T
        qt_ref[qslot, 1] = jnp.where(lo, zero, q).T

    def keys(row0):
        return k_ref[pl.ds(pl.multiple_of(row0, tk), tk), :]

    ones_rows = jnp.ones((ATT_SUM_ROWS, tk), BF16)

    def values(row0):
        return jnp.concatenate([v_ref[:, pl.ds(pl.multiple_of(row0, tk), tk)], ones_rows], axis=0)

    def qk(k, slot, qslot, c):
        for mp in range(2):
            s_ref[slot, mp, :, cols[c]] = jnp.dot(k, qt_ref[qslot, mp, :, cols[c]], preferred_element_type=F32)

    def softmax_pv(v_ext, slot, m, c, mask=None):
        m = [list(mm) for mm in m]
        for mp in range(2):
            s = s_ref[slot, mp, :, cols[c]]
            if mask is not None:
                s = jnp.where(mask, s, -jnp.inf)
            if bounded:
                p = jnp.exp2(s).astype(BF16)
                acc_ref[mp, :, cols[c]] = acc_ref[mp, :, cols[c]] + jnp.dot(v_ext, p, preferred_element_type=F32)
                continue
            m_old = m[mp][c]
            m_new = jnp.maximum(m_old, jnp.max(s, axis=0, keepdims=True))
            p = jnp.exp2(s - m_new).astype(BF16)
            acc_ref[mp, :, cols[c]] = (jnp.exp2(m_old - m_new) * acc_ref[mp, :, cols[c]]
                                       + jnp.dot(v_ext, p, preferred_element_type=F32))
            m[mp][c] = m_new
        return tuple(tuple(mm) for mm in m)

    q_cur = step % 2
    q_nxt = 1 - q_cur

    @pl.when(step == 0)
    def _():
        stage_q_t(q_ref[0:tq, :], 0)
        k0 = keys(0)
        for c in range(n_ct):
            qk(k0, 0, 0, c)

    acc_ref[...] = jnp.zeros(acc_ref.shape, F32)
    neg_inf = () if bounded else tuple(jnp.full((1, tc), -jnp.inf, F32) for _ in range(n_ct))

    def pair(i, m):
        row0 = i * 2 * tk
        k_next, v_cur = keys(row0 + tk), values(row0)
        for c in range(n_ct):
            qk(k_next, 1, q_cur, c)
            m = softmax_pv(v_cur, 0, m, c)
        k_next, v_cur = keys(row0 + 2 * tk), values(row0 + tk)
        for c in range(n_ct):
            qk(k_next, 0, q_cur, c)
            m = softmax_pv(v_cur, 1, m, c)
        return m

    m = lax.fori_loop(0, qi * (n_kb // 2), pair, (neg_inf, neg_inf))

    row0 = qi * tq
    r = lax.broadcasted_iota(jnp.int32, (tk, tc), 0)
    col = lax.broadcasted_iota(jnp.int32, (tk, tc), 1)
    n_diag = tk // tc
    masks = [r <= col + j * tc for j in range(n_diag)]
    q_same_head = q_ref[pl.ds(pl.multiple_of(jnp.minimum(qi + 1, n_q - 1) * tq, tq), tq), :]
    stage_q_t(jnp.where(head_done, qn_ref[...], q_same_head), q_nxt)
    k_first = jnp.where(head_done, kn_ref[...], keys(0))
    others = iter(interleave)

    lam = (jnp.exp(jnp.sum(lq1_ref[...] * lk1_ref[...], axis=-1, keepdims=True))
           - jnp.exp(jnp.sum(lq2_ref[...] * lk2_ref[...], axis=-1, keepdims=True)) + lambda_init)

    def finish(c):
        o_t = (acc_ref[0, :ATT_V_DIM, cols[c]] / acc_ref[0, ATT_V_DIM:ATT_V_DIM + 1, cols[c]]
               - lam * (acc_ref[1, :ATT_V_DIM, cols[c]] / acc_ref[1, ATT_V_DIM:ATT_V_DIM + 1, cols[c]]))
        o_ref[cols[c], :] = (_rms_norm(o_t.T, sw_ref[...]) * (1.0 - lambda_init)).astype(BF16)

    for d in range(n_kb):
        slot = d % 2
        last = d == n_kb - 1
        v_cur = values(row0 + d * tk)
        k_next = k_first if last else keys(row0 + (d + 1) * tk)
        for c in range(n_ct):
            if last:
                qk(k_next, 0, q_nxt, c)
            elif c >= (d + 1) * n_diag:
                qk(k_next, 1 - slot, q_cur, c)
            if c >= d * n_diag:
                m = softmax_pv(v_cur, slot, m, c, mask=masks[c - d * n_diag] if c < (d + 1) * n_diag else None)
            if c // n_diag == d:
                finish(c)
            for _ in range(0 if d == 0 else 2 if last else 1):
                next(others, None)
    for _ in others:
        pass


def _mix_kernel(z_ref, xbc_ref, halo_ref, dt_ref, cw_ref, cb_ref, dtb_ref, alog_ref, dskip_ref, ssd_nw_ref,
                q_ref, k_ref, v_ref, qn_ref, kn_ref, lq1_ref, lk1_ref, lq2_ref, lk2_ref, sw_ref,
                y_ref, o_ref, state_ref, s_ref, acc_ref, qt_ref, *, lambda_init, bounded):
    first = jnp.logical_and(pl.program_id(1) == 0, pl.program_id(2) == 0)

    @pl.when(first)
    def _():
        state_ref[...] = jnp.zeros(state_ref.shape, F32)

    ssd = _ssd_step(first, z_ref, xbc_ref, halo_ref, dt_ref, cw_ref, cb_ref, dtb_ref, alog_ref, dskip_ref, ssd_nw_ref,
                    y_ref, state_ref)
    _attn_kernel(q_ref, k_ref, v_ref, qn_ref, kn_ref, lq1_ref, lk1_ref, lq2_ref, lk2_ref, sw_ref, o_ref,
                 s_ref, acc_ref, qt_ref, lambda_init=lambda_init, bounded=bounded, interleave=ssd)


def _mix(z, xbc, dt, conv_w, conv_b, dt_bias, a_log, d_skip_exp, ssd_norm_w,
         q, k, v, lq1, lk1, lq2, lk2, subln_w, lambda_init, bounded):
    bsz, s, width = q.shape
    n_heads = width // ATT_V_DIM
    tq = min(ATT_BLOCK_Q, s)
    n_q = s // tq
    rows = s // (n_heads * n_q)
    assert rows % SSD_CHUNK == 0 and rows % CONV_HALO == 0
    ssd_width, cw = z.shape[-1], xbc.shape[-1]
    slab = lambda n: pl.BlockSpec((None, rows, n), lambda b, h, i: (b, h * n_q + i, 0))
    halo = pl.BlockSpec((None, CONV_HALO, cw),
                        lambda b, h, i: (b, jnp.maximum((h * n_q + i) * (rows // CONV_HALO) - 1, 0), 0))
    o_spec = pl.BlockSpec((None, tq, LANES), lambda b, h, i: (b, i, h))
    head_spec = pl.BlockSpec((None, s, LANES), lambda b, h, i: (b, 0, h))
    vt_spec = pl.BlockSpec((ATT_V_DIM, s), lambda b, h, i: (h, b))
    tk = min(ATT_BLOCK_K, tq // 2)

    def next_head(b, h, i):
        nxt = jnp.minimum(b * n_heads + h + 1, bsz * n_heads - 1)
        return nxt // n_heads, 0, nxt % n_heads

    q_next_spec = pl.BlockSpec((None, tq, LANES), next_head)
    k_next_spec = pl.BlockSpec((None, tk, LANES), next_head)
    small = _resident((1, ATT_QK_DIM))
    return pl.pallas_call(
        functools.partial(_mix_kernel, lambda_init=lambda_init, bounded=bounded),
        grid=(bsz, n_heads, n_q),
        in_specs=[slab(ssd_width), slab(cw), halo, slab(LANES), _resident(conv_w.shape), _resident((1, cw)),
                  _resident((1, LANES)), _resident((1, LANES)), _resident((1, ssd_width)), _resident((1, ssd_width)),
                  head_spec, head_spec, vt_spec, q_next_spec, k_next_spec, small, small, small, small,
                  _resident((1, ATT_V_DIM))],
        out_specs=[slab(ssd_width), o_spec],
        out_shape=[jax.ShapeDtypeStruct((bsz, s, ssd_width), BF16), jax.ShapeDtypeStruct((bsz, s, width), BF16)],
        scratch_shapes=[pltpu.VMEM((SSD_STATE, ssd_width), F32),
                        pltpu.VMEM((2, 2, tk, tq), F32),
                        pltpu.VMEM((2, ATT_V_DIM + ATT_SUM_ROWS, tq), F32),
                        pltpu.VMEM((2, 2, LANES, tq), BF16)],
        compiler_params=pltpu.CompilerParams(dimension_semantics=("arbitrary",) * 3, vmem_limit_bytes=VMEM_LIMIT),
        name="mix",
    )(z, xbc, xbc, dt, conv_w, conv_b, dt_bias, a_log, d_skip_exp, ssd_norm_w,
      q, k, v, q, k, lq1, lk1, lq2, lk2, subln_w)


def _out_kernel(x_ref, ys_ref, ya_ref, wos_ref, woa_ref, nw_ref, wg_ref, wu_ref, wd_ref, o_ref):
    x = (x_ref[...] + jnp.dot(ys_ref[...], wos_ref[...], preferred_element_type=F32)
         + jnp.dot(ya_ref[...], woa_ref[...], preferred_element_type=F32))
    h = _rms_norm(x, nw_ref[...]).astype(BF16)
    o_ref[...] = x + 0.5 * _swiglu(h, wg_ref, wu_ref, wd_ref)


def _out(x2d, y_ssd, y_att, wo_ssd, wo_att, norm_w, wg, wu, wd):
    t, d = x2d.shape
    tm = min(FFN_ROWS, t)
    row = lambda n: pl.BlockSpec((tm, n), lambda i: (i, 0))
    return pl.pallas_call(
        _out_kernel,
        grid=(t // tm,),
        in_specs=[row(d), row(y_ssd.shape[1]), row(y_att.shape[1]), _resident(wo_ssd.shape), _resident(wo_att.shape),
                  _resident((1, d)), _resident(wg.shape), _resident(wu.shape), _resident(wd.shape)],
        out_specs=row(d),
        out_shape=jax.ShapeDtypeStruct((t, d), F32),
        compiler_params=pltpu.CompilerParams(dimension_semantics=("arbitrary",), vmem_limit_bytes=VMEM_LIMIT),
        name="out_ffn",
    )(x2d, y_ssd, y_att, wo_ssd, wo_att, norm_w, wg, wu, wd)


def _pad_lanes(v):
    return jnp.zeros((1, LANES), F32).at[0, :v.shape[0]].set(v.astype(F32))


def kernel(x, ffn1_norm_w, ffn1_w_gate, ffn1_w_up, ffn1_w_down, mix_norm_w, w_in, conv_w, conv_b, dt_bias, a_log,
           d_skip, ssd_norm_w, q_norm_w, k_norm_w, lambda_q1, lambda_k1, lambda_q2, lambda_k2, attn_subln_w, w_out,
           ffn2_norm_w, ffn2_w_gate, ffn2_w_up, ffn2_w_down):
    bsz, s, d = x.shape
    depth = w_in.shape[0]
    ssd_heads = dt_bias.shape[1]
    ssd_width = ssd_heads * SSD_HEAD_DIM
    conv_ch = conv_w.shape[2]
    att_width = w_out.shape[1] - ssd_width
    qk_width = (att_width // ATT_V_DIM) * 2 * ATT_QK_DIM
    sizes = (ssd_width, conv_ch, ssd_heads, qk_width, qk_width, att_width)
    offs = [0]
    for n in sizes:
        offs.append(offs[-1] + n)
    row = lambda v: v.astype(F32).reshape(1, -1)

    x2d = x.reshape(bsz * s, d)
    for l in range(depth):
        lambda_init = 0.8 - 0.6 * math.exp(-0.3 * l)
        wo_ssd, wo_att = w_out[l][:ssd_width].astype(BF16), w_out[l][ssd_width:].astype(BF16)

        x1 = _ffn(x2d, row(ffn1_norm_w[l]), ffn1_w_gate[l], ffn1_w_up[l], ffn1_w_down[l])
        z, xbc, dt, q, k, v = _proj(x1, row(mix_norm_w[l]), jnp.swapaxes(w_in, 1, 2), l, tuple(offs),
                                    row(jnp.tile(q_norm_w[l], 2)), row(jnp.tile(k_norm_w[l], 2)))
        score_bound = (ATT_QK_DIM * Q_SCALE * ATT_BOUND_MARGIN * jnp.max(jnp.abs(q_norm_w[l]))
                       * jnp.max(jnp.abs(k_norm_w[l]))).astype(F32)
        mix_args = (z.reshape(bsz, s, -1), xbc.reshape(bsz, s, -1), dt.reshape(bsz, s, -1),
                    conv_w[l].astype(F32), row(conv_b[l]), _pad_lanes(dt_bias[l]), _pad_lanes(a_log[l]),
                    row(jnp.repeat(d_skip[l], SSD_HEAD_DIM)), row(ssd_norm_w[l]),
                    q.reshape(bsz, s, -1), k.reshape(bsz, s, -1), v,
                    row(lambda_q1[l]), row(lambda_k1[l]), row(lambda_q2[l]), row(lambda_k2[l]),
                    row(attn_subln_w[l]))
        y_ssd, y_att = lax.cond(score_bound <= ATT_MAX_BOUND,
                                lambda *a: _mix(*a, lambda_init, True), lambda *a: _mix(*a, lambda_init, False),
                                *mix_args)
        x2d = _out(x1, y_ssd.reshape(bsz * s, -1), y_att.reshape(bsz * s, -1), wo_ssd, wo_att,
                   row(ffn2_norm_w[l]), ffn2_w_gate[l], ffn2_w_up[l], ffn2_w_down[l])
    return x2d.reshape(bsz, s, d)
```

```python
import functools
import math

import jax
import jax.numpy as jnp
from jax import lax
from jax.experimental import pallas as pl
from jax.experimental.pallas import tpu as pltpu

F32 = jnp.float32
BF16 = jnp.bfloat16

NORM_EPS = 1e-6
SSD_HEAD_DIM = 64
SSD_GROUPS = 2
SSD_STATE = 128
SSD_CONV = 4
SSD_CHUNK = 128
ATT_QK_DIM = 64
ATT_V_DIM = 128
LANES = 128
CONV_HALO = 16
LOG2_E = math.log2(math.e)
Q_SCALE = ATT_QK_DIM ** -0.5 * LOG2_E
ATT_BOUND_MARGIN = 1.01
ATT_MAX_BOUND = 60.0

FFN_ROWS = 512
FFN_COLS = 256
ATT_BLOCK_Q = 2048
ATT_BLOCK_K = 512
ATT_COL_TILE = 256
ATT_SUM_ROWS = 16
VMEM_LIMIT = 56 * 1024 * 1024


def _resident(shape):
    nd = len(shape)
    return pl.BlockSpec(shape, lambda *_: (0,) * nd, pipeline_mode=pl.Buffered(1))


def _rms_norm(x, w):
    return x * lax.rsqrt(jnp.mean(x * x, axis=-1, keepdims=True) + NORM_EPS) * w


def _silu(x):
    h = 0.5 * x
    return h + h * jnp.tanh(h)


def _swiglu(h, wg_ref, wu_ref, wd_ref):
    d_ff = wg_ref.shape[1]
    acc = None
    for c in range(d_ff // FFN_COLS):
        sl = slice(c * FFN_COLS, (c + 1) * FFN_COLS)
        g = jnp.dot(h, wg_ref[:, sl].astype(BF16), preferred_element_type=F32)
        u = jnp.dot(h, wu_ref[:, sl].astype(BF16), preferred_element_type=F32)
        a = (_silu(g) * u).astype(BF16)
        d = jnp.dot(a, wd_ref[sl, :].astype(BF16), preferred_element_type=F32)
        acc = d if acc is None else acc + d
    return acc


def _ffn_kernel(x_ref, nw_ref, wg_ref, wu_ref, wd_ref, o_ref):
    x = x_ref[...]
    h = _rms_norm(x, nw_ref[...]).astype(BF16)
    o_ref[...] = x + 0.5 * _swiglu(h, wg_ref, wu_ref, wd_ref)


def _ffn(x2d, norm_w, wg, wu, wd):
    t, d = x2d.shape
    tm = min(FFN_ROWS, t)
    row = pl.BlockSpec((tm, d), lambda i: (i, 0))
    return pl.pallas_call(
        _ffn_kernel,
        grid=(t // tm,),
        in_specs=[row, _resident((1, d)), _resident(wg.shape), _resident(wu.shape), _resident(wd.shape)],
        out_specs=row,
        out_shape=jax.ShapeDtypeStruct((t, d), F32),
        compiler_params=pltpu.CompilerParams(dimension_semantics=("arbitrary",), vmem_limit_bytes=VMEM_LIMIT),
        name="ffn",
    )(x2d, norm_w, wg, wu, wd)


def _qk_norm(t, w2, scale):
    lo = lax.broadcasted_iota(jnp.int32, t.shape, 1) < ATT_QK_DIM
    x2 = t * t
    s_lo = jnp.sum(jnp.where(lo, x2, 0.0), axis=-1, keepdims=True)
    s_hi = jnp.sum(jnp.where(lo, 0.0, x2), axis=-1, keepdims=True)
    r = lax.rsqrt(jnp.where(lo, s_lo, s_hi) * (1.0 / ATT_QK_DIM) + NORM_EPS)
    return t * r * (w2 * scale)


def _proj_kernel(x_ref, nw_ref, wt_ref, qw_ref, kw_ref, z_ref, xbc_ref, dt_ref, q_ref, k_ref, v_ref, *, offsets):
    o_z, o_xbc, o_dt, o_q, o_k, o_v, o_end = offsets
    h = _rms_norm(x_ref[...], nw_ref[...]).astype(BF16)

    def project(w_rows):
        return lax.dot_general(h, w_rows.astype(BF16), (((1,), (1,)), ((), ())), preferred_element_type=F32)

    n_heads = (o_k - o_q) // LANES
    for lo_row, g_ref, o_ref, scale in ((o_q, qw_ref, q_ref, Q_SCALE), (o_k, kw_ref, k_ref, 1.0)):
        t = project(wt_ref[lo_row:lo_row + o_k - o_q, :])
        for hd in range(n_heads):
            sl = slice(hd * LANES, (hd + 1) * LANES)
            o_ref[:, sl] = _qk_norm(t[:, sl], g_ref[...], scale).astype(BF16)
    w_dt = wt_ref[o_dt:o_dt + LANES, :]
    w_dt = jnp.where(lax.broadcasted_iota(jnp.int32, w_dt.shape, 0) < o_q - o_dt, w_dt, 0.0)
    dt_ref[...] = project(w_dt)
    v_ref[...] = project(wt_ref[o_v:o_end, :]).astype(BF16)
    z_ref[...] = project(wt_ref[o_z:o_xbc, :]).astype(BF16)
    xbc_ref[...] = project(wt_ref[o_xbc:o_dt, :]).astype(BF16)


def _proj(x2d, norm_w, wt_stack, layer, offsets, qw2, kw2):
    t, d = x2d.shape
    w_spec = pl.BlockSpec((None,) + wt_stack.shape[1:], lambda i: (layer, 0, 0), pipeline_mode=pl.Buffered(1))
    tm = min(FFN_ROWS, t)
    row = lambda n: pl.BlockSpec((tm, n), lambda i: (i, 0))
    o_z, o_xbc, o_dt, o_q, o_k, o_v, o_end = offsets
    assert all(o % 16 == 0 for o in offsets) and o_q - o_dt <= LANES and o_dt + LANES <= o_end
    widths = (o_xbc - o_z, o_dt - o_xbc, LANES, o_k - o_q, o_v - o_k, o_end - o_v)
    dtypes = (BF16, BF16, F32, BF16, BF16, BF16)
    return pl.pallas_call(
        functools.partial(_proj_kernel, offsets=offsets),
        grid=(t // tm,),
        in_specs=[row(d), _resident((1, d)), w_spec, _resident((1, LANES)), _resident((1, LANES))],
        out_specs=[row(n) for n in widths],
        out_shape=[jax.ShapeDtypeStruct((t, n), dt) for n, dt in zip(widths, dtypes)],
        compiler_params=pltpu.CompilerParams(dimension_semantics=("arbitrary",), vmem_limit_bytes=VMEM_LIMIT),
        name="proj",
    )(x2d, norm_w, wt_stack, qw2, kw2)


def _lane_bcast(x, col):
    return jnp.broadcast_to(x[:, col:col + 1], x.shape)


def _expand_heads(x, pair):
    lo = lax.broadcasted_iota(jnp.int32, x.shape, 1) < SSD_HEAD_DIM
    return jnp.where(lo, _lane_bcast(x, 2 * pair), _lane_bcast(x, 2 * pair + 1))


def _ssd_step(first, z_ref, xbc_ref, halo_ref, dt_ref, cw_ref, cb_ref, dtb_ref, alog_ref, dskip_ref, nw_ref,
              y_ref, state_ref):
    L = SSD_CHUNK
    n_chunks = xbc_ref.shape[0] // L
    width = z_ref.shape[-1]
    n_pairs = width // LANES
    heads_per_group = (width // SSD_HEAD_DIM) // SSD_GROUPS
    gw = width // SSD_GROUPS

    out_t = lax.broadcasted_iota(jnp.int32, (L, CONV_HALO + L), 0)
    in_t = lax.broadcasted_iota(jnp.int32, (L, CONV_HALO + L), 1) - CONV_HALO
    shifts = [jnp.where(in_t == out_t - (SSD_CONV - 1 - kk), 1.0, 0.0).astype(BF16) for kk in range(SSD_CONV - 1)]
    causal = lax.broadcasted_iota(jnp.int32, (L, L), 0) >= lax.broadcasted_iota(jnp.int32, (L, L), 1)
    tril = causal.astype(F32)
    lo = lax.broadcasted_iota(jnp.int32, (L, LANES), 1) < SSD_HEAD_DIM
    zero = jnp.zeros((L, LANES), BF16)
    head_of_lane = lax.broadcasted_iota(jnp.int32, (LANES, width), 1) // SSD_HEAD_DIM
    expand = (lax.broadcasted_iota(jnp.int32, (LANES, width), 0) == head_of_lane).astype(BF16)
    neg_a = -jnp.exp(alog_ref[...])

    halo = halo_ref[...]
    tail = jnp.where(first, jnp.zeros_like(halo), halo)
    for ci in range(n_chunks):
        rows = slice(ci * L, (ci + 1) * L)

        x_cur = xbc_ref[rows, :]
        x_ext = jnp.concatenate([tail, x_cur], axis=0)
        tail = x_cur[L - CONV_HALO:, :]
        conv = cb_ref[...] + cw_ref[SSD_CONV - 1:SSD_CONV, :] * x_cur.astype(F32)
        for kk in range(SSD_CONV - 1):
            conv = conv + cw_ref[kk:kk + 1, :] * jnp.dot(shifts[kk], x_ext, preferred_element_type=F32)
        act = _silu(conv)
        yield
        xs = act[:, :width]
        bm = act[:, width:width + SSD_GROUPS * SSD_STATE].astype(BF16)
        cm = act[:, width + SSD_GROUPS * SSD_STATE:]

        raw = dt_ref[rows, :] + dtb_ref[...]
        dt = jnp.maximum(raw, 0.0) + jnp.log1p(jnp.exp(-jnp.abs(raw)))
        a_cs = LOG2_E * jnp.dot(tril, dt * neg_a, preferred_element_type=F32, precision=lax.Precision.HIGHEST)
        a_last = a_cs[L - 1:L, :]
        f = dt * jnp.exp2(a_last - a_cs)
        a_src_t = a_cs.T - jnp.log2(dt).T

        yield
        cm_bf = cm.astype(BF16)
        cb = [lax.dot_general(cm_bf[:, g * SSD_STATE:(g + 1) * SSD_STATE], bm[:, g * SSD_STATE:(g + 1) * SSD_STATE],
                              (((1,), (1,)), ((), ())), preferred_element_type=F32) for g in range(SSD_GROUPS)]
        xs_bf = xs.astype(BF16)
        y_tiles = []
        for pair in range(n_pairs):
            g = (2 * pair) // heads_per_group
            cm_g = cm[:, g * SSD_STATE:(g + 1) * SSD_STATE]
            sl = slice(pair * LANES, (pair + 1) * LANES)
            x_pair = xs_bf[:, sl]
            prev_pair = state_ref[:, sl].astype(BF16)
            y_pair = None
            for half in range(2):
                hd = 2 * pair + half
                a_col = _lane_bcast(a_cs, hd)
                seg = a_col - a_src_t[hd:hd + 1, :]
                m_h = (cb[g] * jnp.exp2(jnp.where(causal, seg, -jnp.inf))).astype(BF16)
                c_h = (cm_g * jnp.exp2(a_col)).astype(BF16)
                keep = lo if half == 0 else jnp.logical_not(lo)
                rhs = jnp.concatenate([jnp.where(keep, x_pair, zero), jnp.where(keep, prev_pair, zero)], axis=0)
                yh = jnp.dot(jnp.concatenate([m_h, c_h], axis=1), rhs, preferred_element_type=F32)
                y_pair = yh if y_pair is None else y_pair + yh
            y_tiles.append(y_pair)
            yield

        f_hi = f.astype(BF16)
        f_lo = (f - f_hi.astype(F32)).astype(BF16)
        f_exp = (jnp.dot(f_hi, expand, preferred_element_type=F32)
                 + jnp.dot(f_lo, expand, preferred_element_type=F32))
        ea_last = jnp.exp2(a_cs[L - 8:L, :])
        for pair in range(n_pairs):
            g = (2 * pair) // heads_per_group
            sl = slice(pair * LANES, (pair + 1) * LANES)
            bm_g = bm[:, g * SSD_STATE:(g + 1) * SSD_STATE]
            xw = (xs[:, sl] * f_exp[:, sl]).astype(BF16)
            new = lax.dot_general(bm_g, xw, (((0,), (0,)), ((), ())), preferred_element_type=F32)
            chunk_decay = _expand_heads(ea_last, pair)[7:8, :]
            state_ref[:, sl] = state_ref[:, sl] * chunk_decay + new
            if pair % 4 == 3:
                yield

        y = jnp.concatenate(y_tiles, axis=1) + xs * dskip_ref[...]
        gated = y * _silu(z_ref[rows, :].astype(F32))
        for g in range(SSD_GROUPS):
            sl = slice(g * gw, (g + 1) * gw)
            y_ref[rows, sl] = _rms_norm(gated[:, sl], nw_ref[:, sl]).astype(BF16)
        yield


def _attn_kernel(q_ref, k_ref, v_ref, qn_ref, kn_ref, lq1_ref, lk1_ref, lq2_ref, lk2_ref, sw_ref, o_ref,
                 s_ref, acc_ref, qt_ref, *, lambda_init, bounded, interleave=()):
    tq = o_ref.shape[0]
    tk = s_ref.shape[2]
    n_kb = tq // tk
    tc = ATT_COL_TILE
    n_ct = tq // tc
    n_q = q_ref.shape[0] // tq
    qi = pl.program_id(2)
    step = (pl.program_id(0) * pl.num_programs(1) + pl.program_id(1)) * n_q + qi
    head_done = qi == n_q - 1
    cols = [slice(c * tc, (c + 1) * tc) for c in range(n_ct)]

    def stage_q_t(q, qslot):
        lo = lax.broadcasted_iota(jnp.int32, q.shape, 1) < ATT_QK_DIM
        zero = jnp.zeros_like(q)
        qt_ref[qslot, 0] = jnp.where(lo, q, zero).T
        qt_ref[qslot, 1] = jnp.where(lo, zero, q).T

    def keys(row0):
        return k_ref[pl.ds(pl.multiple_of(row0, tk), tk), :]

    ones_rows = jnp.ones((ATT_SUM_ROWS, tk), BF16)

    def values(row0):
        return jnp.concatenate([v_ref[pl.ds(pl.multiple_of(row0, tk), tk), :].T, ones_rows], axis=0)

    def qk(k, slot, qslot, c, rows=tk):
        for mp in range(2):
            s_ref[slot, mp, :rows, cols[c]] = jnp.dot(k[:rows], qt_ref[qslot, mp, :, cols[c]],
                                                      preferred_element_type=F32)

    def softmax_pv(v_ext, slot, m, c, mask=None, rows=tk):
        m = [list(mm) for mm in m]
        v_ext = v_ext[:, :rows]
        for mp in range(2):
            s = s_ref[slot, mp, :rows, cols[c]]
            if mask is not None:
                s = jnp.where(mask[:rows], s, -jnp.inf)
            if bounded:
                p = jnp.exp2(s).astype(BF16)
                acc_ref[mp, :, cols[c]] = acc_ref[mp, :, cols[c]] + jnp.dot(v_ext, p, preferred_element_type=F32)
                continue
            m_old = m[mp][c]
            m_new = jnp.maximum(m_old, jnp.max(s, axis=0, keepdims=True))
            p = jnp.exp2(s - m_new).astype(BF16)
            acc_ref[mp, :, cols[c]] = (jnp.exp2(m_old - m_new) * acc_ref[mp, :, cols[c]]
                                       + jnp.dot(v_ext, p, preferred_element_type=F32))
            m[mp][c] = m_new
        return tuple(tuple(mm) for mm in m)

    q_cur = step % 2
    q_nxt = 1 - q_cur

    @pl.when(step == 0)
    def _():
        stage_q_t(q_ref[0:tq, :], 0)
        k0 = keys(0)
        for c in range(n_ct):
            qk(k0, 0, 0, c)

    acc_ref[...] = jnp.zeros(acc_ref.shape, F32)
    neg_inf = () if bounded else tuple(jnp.full((1, tc), -jnp.inf, F32) for _ in range(n_ct))

    def pair(i, m):
        row0 = i * 2 * tk
        k_next, v_cur = keys(row0 + tk), values(row0)
        for c in range(n_ct):
            qk(k_next, 1, q_cur, c)
            m = softmax_pv(v_cur, 0, m, c)
        k_next, v_cur = keys(row0 + 2 * tk), values(row0 + tk)
        for c in range(n_ct):
            qk(k_next, 0, q_cur, c)
            m = softmax_pv(v_cur, 1, m, c)
        return m

    m = lax.fori_loop(0, qi * (n_kb // 2), pair, (neg_inf, neg_inf))

    row0 = qi * tq
    r = lax.broadcasted_iota(jnp.int32, (tk, tc), 0)
    col = lax.broadcasted_iota(jnp.int32, (tk, tc), 1)
    n_diag = tk // tc
    masks = [r <= col + j * tc for j in range(n_diag)]
    q_same_head = q_ref[pl.ds(pl.multiple_of(jnp.minimum(qi + 1, n_q - 1) * tq, tq), tq), :]
    stage_q_t(jnp.where(head_done, qn_ref[...], q_same_head), q_nxt)
    k_first = jnp.where(head_done, kn_ref[...], keys(0))
    others = iter(interleave)

    lam = (jnp.exp(jnp.sum(lq1_ref[...] * lk1_ref[...], axis=-1, keepdims=True))
           - jnp.exp(jnp.sum(lq2_ref[...] * lk2_ref[...], axis=-1, keepdims=True)) + lambda_init)

    def finish(c):
        o_t = (acc_ref[0, :ATT_V_DIM, cols[c]] / acc_ref[0, ATT_V_DIM:ATT_V_DIM + 1, cols[c]]
               - lam * (acc_ref[1, :ATT_V_DIM, cols[c]] / acc_ref[1, ATT_V_DIM:ATT_V_DIM + 1, cols[c]]))
        o_ref[cols[c], :] = (_rms_norm(o_t.T, sw_ref[...]) * (1.0 - lambda_init)).astype(BF16)

    for d in range(n_kb):
        slot = d % 2
        last = d == n_kb - 1
        v_cur = values(row0 + d * tk)
        k_next = k_first if last else keys(row0 + (d + 1) * tk)
        for c in range(n_ct):
            if last:
                qk(k_next, 0, q_nxt, c)
            elif c >= (d + 1) * n_diag:
                qk(k_next, 1 - slot, q_cur, c, rows=min(tk, (c - (d + 1) * n_diag + 1) * tc))
            if c >= d * n_diag:
                diag = c < (d + 1) * n_diag
                m = softmax_pv(v_cur, slot, m, c, mask=masks[c - d * n_diag] if diag else None,
                               rows=min(tk, (c - d * n_diag + 1) * tc))
            if c // n_diag == d:
                finish(c)
            for _ in range(0 if d == 0 else 2 if last else 1):
                next(others, None)
    for _ in others:
        pass


def _mix_kernel(z_ref, xbc_ref, halo_ref, dt_ref, cw_ref, cb_ref, dtb_ref, alog_ref, dskip_ref, ssd_nw_ref,
                q_ref, k_ref, v_ref, qn_ref, kn_ref, lq1_ref, lk1_ref, lq2_ref, lk2_ref, sw_ref,
                y_ref, o_ref, state_ref, s_ref, acc_ref, qt_ref, *, lambda_init, bounded):
    first = jnp.logical_and(pl.program_id(1) == 0, pl.program_id(2) == 0)

    @pl.when(first)
    def _():
        state_ref[...] = jnp.zeros(state_ref.shape, F32)

    ssd = _ssd_step(first, z_ref, xbc_ref, halo_ref, dt_ref, cw_ref, cb_ref, dtb_ref, alog_ref, dskip_ref, ssd_nw_ref,
                    y_ref, state_ref)
    _attn_kernel(q_ref, k_ref, v_ref, qn_ref, kn_ref, lq1_ref, lk1_ref, lq2_ref, lk2_ref, sw_ref, o_ref,
                 s_ref, acc_ref, qt_ref, lambda_init=lambda_init, bounded=bounded, interleave=ssd)


def _mix(z, xbc, dt, conv_w, conv_b, dt_bias, a_log, d_skip_exp, ssd_norm_w,
         q, k, v, lq1, lk1, lq2, lk2, subln_w, lambda_init, bounded):
    bsz, s, width = q.shape
    n_heads = width // ATT_V_DIM
    tq = min(ATT_BLOCK_Q, s)
    n_q = s // tq
    rows = s // (n_heads * n_q)
    assert rows % SSD_CHUNK == 0 and rows % CONV_HALO == 0
    ssd_width, cw = z.shape[-1], xbc.shape[-1]
    slab = lambda n: pl.BlockSpec((None, rows, n), lambda b, h, i: (b, h * n_q + i, 0))
    halo = pl.BlockSpec((None, CONV_HALO, cw),
                        lambda b, h, i: (b, jnp.maximum((h * n_q + i) * (rows // CONV_HALO) - 1, 0), 0))
    o_spec = pl.BlockSpec((None, tq, LANES), lambda b, h, i: (b, i, h))
    head_spec = pl.BlockSpec((None, s, LANES), lambda b, h, i: (b, 0, h))
    tk = min(ATT_BLOCK_K, tq // 2)

    def next_head(b, h, i):
        nxt = jnp.minimum(b * n_heads + h + 1, bsz * n_heads - 1)
        return nxt // n_heads, 0, nxt % n_heads

    q_next_spec = pl.BlockSpec((None, tq, LANES), next_head)
    k_next_spec = pl.BlockSpec((None, tk, LANES), next_head)
    small = _resident((1, ATT_QK_DIM))
    return pl.pallas_call(
        functools.partial(_mix_kernel, lambda_init=lambda_init, bounded=bounded),
        grid=(bsz, n_heads, n_q),
        in_specs=[slab(ssd_width), slab(cw), halo, slab(LANES), _resident(conv_w.shape), _resident((1, cw)),
                  _resident((1, LANES)), _resident((1, LANES)), _resident((1, ssd_width)), _resident((1, ssd_width)),
                  head_spec, head_spec, head_spec, q_next_spec, k_next_spec, small, small, small, small,
                  _resident((1, ATT_V_DIM))],
        out_specs=[slab(ssd_width), o_spec],
        out_shape=[jax.ShapeDtypeStruct((bsz, s, ssd_width), BF16), jax.ShapeDtypeStruct((bsz, s, width), BF16)],
        scratch_shapes=[pltpu.VMEM((SSD_STATE, ssd_width), F32),
                        pltpu.VMEM((2, 2, tk, tq), F32),
                        pltpu.VMEM((2, ATT_V_DIM + ATT_SUM_ROWS, tq), F32),
                        pltpu.VMEM((2, 2, LANES, tq), BF16)],
        compiler_params=pltpu.CompilerParams(dimension_semantics=("arbitrary",) * 3, vmem_limit_bytes=VMEM_LIMIT),
        name="mix",
    )(z, xbc, xbc, dt, conv_w, conv_b, dt_bias, a_log, d_skip_exp, ssd_norm_w,
      q, k, v, q, k, lq1, lk1, lq2, lk2, subln_w)


def _out_kernel(x_ref, ys_ref, ya_ref, wos_ref, woa_ref, nw_ref, wg_ref, wu_ref, wd_ref, o_ref):
    x = (x_ref[...] + jnp.dot(ys_ref[...], wos_ref[...], preferred_element_type=F32)
         + jnp.dot(ya_ref[...], woa_ref[...], preferred_element_type=F32))
    h = _rms_norm(x, nw_ref[...]).astype(BF16)
    o_ref[...] = x + 0.5 * _swiglu(h, wg_ref, wu_ref, wd_ref)


def _out(x2d, y_ssd, y_att, wo_ssd, wo_att, norm_w, wg, wu, wd):
    t, d = x2d.shape
    tm = min(FFN_ROWS, t)
    row = lambda n: pl.BlockSpec((tm, n), lambda i: (i, 0))
    return pl.pallas_call(
        _out_kernel,
        grid=(t // tm,),
        in_specs=[row(d), row(y_ssd.shape[1]), row(y_att.shape[1]), _resident(wo_ssd.shape), _resident(wo_att.shape),
                  _resident((1, d)), _resident(wg.shape), _resident(wu.shape), _resident(wd.shape)],
        out_specs=row(d),
        out_shape=jax.ShapeDtypeStruct((t, d), F32),
        compiler_params=pltpu.CompilerParams(dimension_semantics=("arbitrary",), vmem_limit_bytes=VMEM_LIMIT),
        name="out_ffn",
    )(x2d, y_ssd, y_att, wo_ssd, wo_att, norm_w, wg, wu, wd)


def _pad_lanes(v):
    return jnp.zeros((1, LANES), F32).at[0, :v.shape[0]].set(v.astype(F32))


def kernel(x, ffn1_norm_w, ffn1_w_gate, ffn1_w_up, ffn1_w_down, mix_norm_w, w_in, conv_w, conv_b, dt_bias, a_log,
           d_skip, ssd_norm_w, q_norm_w, k_norm_w, lambda_q1, lambda_k1, lambda_q2, lambda_k2, attn_subln_w, w_out,
           ffn2_norm_w, ffn2_w_gate, ffn2_w_up, ffn2_w_down):
    bsz, s, d = x.shape
    depth = w_in.shape[0]
    ssd_heads = dt_bias.shape[1]
    ssd_width = ssd_heads * SSD_HEAD_DIM
    conv_ch = conv_w.shape[2]
    att_width = w_out.shape[1] - ssd_width
    qk_width = (att_width // ATT_V_DIM) * 2 * ATT_QK_DIM
    sizes = (ssd_width, conv_ch, ssd_heads, qk_width, qk_width, att_width)
    offs = [0]
    for n in sizes:
        offs.append(offs[-1] + n)
    row = lambda v: v.astype(F32).reshape(1, -1)

    x2d = x.reshape(bsz * s, d)
    for l in range(depth):
        lambda_init = 0.8 - 0.6 * math.exp(-0.3 * l)
        wo_ssd, wo_att = w_out[l][:ssd_width].astype(BF16), w_out[l][ssd_width:].astype(BF16)

        x1 = _ffn(x2d, row(ffn1_norm_w[l]), ffn1_w_gate[l], ffn1_w_up[l], ffn1_w_down[l])
        z, xbc, dt, q, k, v = _proj(x1, row(mix_norm_w[l]), jnp.swapaxes(w_in, 1, 2), l, tuple(offs),
                                    row(jnp.tile(q_norm_w[l], 2)), row(jnp.tile(k_norm_w[l], 2)))
        score_bound = (ATT_QK_DIM * Q_SCALE * ATT_BOUND_MARGIN * jnp.max(jnp.abs(q_norm_w[l]))
                       * jnp.max(jnp.abs(k_norm_w[l]))).astype(F32)
        mix_args = (z.reshape(bsz, s, -1), xbc.reshape(bsz, s, -1), dt.reshape(bsz, s, -1),
                    conv_w[l].astype(F32), row(conv_b[l]), _pad_lanes(dt_bias[l]), _pad_lanes(a_log[l]),
                    row(jnp.repeat(d_skip[l], SSD_HEAD_DIM)), row(ssd_norm_w[l]),
                    q.reshape(bsz, s, -1), k.reshape(bsz, s, -1), v.reshape(bsz, s, -1),
                    row(lambda_q1[l]), row(lambda_k1[l]), row(lambda_q2[l]), row(lambda_k2[l]),
                    row(attn_subln_w[l]))
        y_ssd, y_att = lax.cond(score_bound <= ATT_MAX_BOUND,
                                lambda *a: _mix(*a, lambda_init, True), lambda *a: _mix(*a, lambda_init, False),
                                *mix_args)
        x2d = _out(x1, y_ssd.reshape(bsz * s, -1), y_att.reshape(bsz * s, -1), wo_ssd, wo_att,
                   row(ffn2_norm_w[l]), ffn2_w_gate[l], ffn2_w_up[l], ffn2_w_down[l])
    return x2d.reshape(bsz, s, d)
```
